```python
import jax, jax.numpy as jnp
from jax import lax
import numpy as np

D_MODEL = 1024
BATCH = 4
SEQ = 8192
DEPTH = 4

CTX_LEN = 256
GRID_W = 64
HEAD_DIM = 64
ROPE_FREQS = HEAD_DIM // 4
ROPE_THETA = 10000.0
NORM_EPS = 1e-6
NEG_INF = -1e30
ATTN_SCALE = HEAD_DIM ** -0.5
Q_BLOCK = 128

POOL_WIDTH = D_MODEL // 4
POOL_WINDOWS = (2, 4, 8, 16)
POOL_GROUP = POOL_WIDTH // len(POOL_WINDOWS)

WIN_HEADS = (3 * D_MODEL // 8) // HEAD_DIM
WIN_KV_HEADS = 2
WIN_GROUP = WIN_HEADS // WIN_KV_HEADS
WINDOW = 128

GLB_HEADS = (3 * D_MODEL // 8) // HEAD_DIM
GLB_KV_HEADS = 2
GLB_GROUP = GLB_HEADS // GLB_KV_HEADS

WIN_WIDTH = WIN_HEADS * HEAD_DIM
WIN_KV_WIDTH = WIN_KV_HEADS * HEAD_DIM
GLB_WIDTH = GLB_HEADS * HEAD_DIM
GLB_KV_WIDTH = GLB_KV_HEADS * HEAD_DIM
MIX_WIDTH = POOL_WIDTH + WIN_WIDTH + GLB_WIDTH
IN_WIDTH = POOL_WIDTH + WIN_WIDTH + 2 * WIN_KV_WIDTH + GLB_WIDTH + 2 * GLB_KV_WIDTH

D_FF = -(-8 * D_MODEL // (3 * 256)) * 256
N_MOD = 6

kernel_name = "hybrid_pool_window_global_dit_block"


def rms_norm(x, g):
    x32 = x.astype(jnp.float32)
    y = x32 * lax.rsqrt(jnp.mean(x32 * x32, axis=-1, keepdims=True) + NORM_EPS)
    return (y * g.astype(jnp.float32)).astype(x.dtype)


def modulate(h, shift, scale):
    return h * (1.0 + scale) + shift


def rope_tables(n):
    rows = n // GRID_W
    row = jnp.broadcast_to(jnp.arange(rows)[:, None], (rows, GRID_W)).reshape(-1).astype(jnp.float32)
    col = jnp.broadcast_to(jnp.arange(GRID_W)[None, :], (rows, GRID_W)).reshape(-1).astype(jnp.float32)
    freq = ROPE_THETA ** (-jnp.arange(ROPE_FREQS, dtype=jnp.float32) / ROPE_FREQS)
    ang = jnp.stack([row[:, None] * freq, col[:, None] * freq], axis=1)
    ang = ang[:, None, :, None, :]
    return jnp.cos(ang), jnp.sin(ang)


def apply_rope2d(x, cos, sin):
    b, n, h, d = x.shape
    xr = x.reshape(b, n, h, 2, 2, d // 4)
    rot = jnp.stack([-xr[..., 1, :], xr[..., 0, :]], axis=-2)
    return (xr * cos + rot * sin).reshape(b, n, h, d).astype(x.dtype)


def attend(q, keys, values, mask=None, sink=None):
    logits = [jnp.einsum('bqkgd,bckd->bkgqc', q, k).astype(jnp.float32) * ATTN_SCALE for k in keys]
    if mask is not None:
        logits[0] = jnp.where(mask, logits[0], NEG_INF)
    if sink is not None:
        b, nq, kvh, g, _ = q.shape
        logits.append(jnp.broadcast_to(sink.astype(jnp.float32)[None, :, :, None, None], (b, kvh, g, nq, 1)))
    p = jax.nn.softmax(jnp.concatenate(logits, axis=-1), axis=-1)
    out = None
    off = 0
    for v in values:
        size = v.shape[1]
        term = jnp.einsum('bkgqc,bckd->bqkgd', p[..., off:off + size].astype(v.dtype), v)
        out = term if out is None else out + term
        off += size
    return out


def merge_blocks(o):
    nb, b, nq = o.shape[:3]
    return jnp.transpose(o, (1, 0, 2, 3, 4, 5)).reshape(b, nb * nq, -1)


def pool_mix(a, w_pool_l, pool_scale_l):
    b, n, _ = a.shape
    a32 = a.astype(jnp.float32)
    cs = jnp.concatenate([jnp.zeros((b, 1, POOL_WIDTH), jnp.float32), jnp.cumsum(a32, axis=1)], axis=1)
    t = jnp.arange(n)
    feats = []
    for g, w in enumerate(POOL_WINDOWS):
        lo = jnp.clip(t - w // 2, 0, n - 1)
        hi = jnp.clip(t + (w - 1 - w // 2), 0, n - 1)
        cnt = (hi - lo + 1).astype(jnp.float32)[None, :, None]
        sl = slice(g * POOL_GROUP, (g + 1) * POOL_GROUP)
        csg = cs[..., sl]
        feats.append((csg[:, hi + 1] - csg[:, lo]) / cnt - a32[..., sl])
    p = jnp.stack(feats, axis=2)
    y = jnp.einsum('bngc,gcd->bngd', p, w_pool_l.astype(jnp.float32)).reshape(b, n, POOL_WIDTH)
    return (y * pool_scale_l.astype(jnp.float32)).astype(a.dtype)


def window_attention(q, k, v, kc, vc, sink):
    n = q.shape[1]
    pad = ((0, 0), (WINDOW, WINDOW), (0, 0), (0, 0))
    kp = jnp.pad(k, pad)
    vp = jnp.pad(v, pad)
    span = Q_BLOCK + 2 * WINDOW

    def block(i):
        start = i * Q_BLOCK
        qi = lax.dynamic_slice_in_dim(q, start, Q_BLOCK, axis=1)
        ki = lax.dynamic_slice_in_dim(kp, start, span, axis=1)
        vi = lax.dynamic_slice_in_dim(vp, start, span, axis=1)
        qpos = start + jnp.arange(Q_BLOCK)
        kpos = start - WINDOW + jnp.arange(span)
        mask = (jnp.abs(kpos[None, :] - qpos[:, None]) <= WINDOW) & (kpos >= 0)[None, :] & (kpos < n)[None, :]
        return attend(qi, [ki, kc], [vi, vc], mask=mask, sink=sink)

    return merge_blocks(lax.map(block, jnp.arange(n // Q_BLOCK)))


def global_attention(q, k, v, kc, vc):
    n = q.shape[1]

    def block(i):
        qi = lax.dynamic_slice_in_dim(q, i * Q_BLOCK, Q_BLOCK, axis=1)
        return attend(qi, [k, kc], [v, vc])

    return merge_blocks(lax.map(block, jnp.arange(n // Q_BLOCK)))


def mixer_inputs(h, w_in_l, g_qn, g_kn, rope):
    u = h @ w_in_l
    b, n, _ = u.shape
    widths = [POOL_WIDTH, WIN_WIDTH, WIN_KV_WIDTH, WIN_KV_WIDTH, GLB_WIDTH, GLB_KV_WIDTH, GLB_KV_WIDTH]
    offs = [int(o) for o in np.cumsum(widths)[:-1]]
    a, qw, kw, vw, qg, kg, vg = jnp.split(u, offs, axis=-1)
    qw = qw.reshape(b, n, WIN_HEADS, HEAD_DIM)
    kw = kw.reshape(b, n, WIN_KV_HEADS, HEAD_DIM)
    vw = vw.reshape(b, n, WIN_KV_HEADS, HEAD_DIM)
    qg = rms_norm(qg.reshape(b, n, GLB_HEADS, HEAD_DIM), g_qn)
    kg = rms_norm(kg.reshape(b, n, GLB_KV_HEADS, HEAD_DIM), g_kn)
    vg = vg.reshape(b, n, GLB_KV_HEADS, HEAD_DIM)
    if rope is not None:
        cos, sin = rope
        qw = apply_rope2d(qw, cos, sin)
        kw = apply_rope2d(kw, cos, sin)
        qg = apply_rope2d(qg, cos, sin)
        kg = apply_rope2d(kg, cos, sin)
    qw = qw.reshape(b, n, WIN_KV_HEADS, WIN_GROUP, HEAD_DIM)
    qg = qg.reshape(b, n, GLB_KV_HEADS, GLB_GROUP, HEAD_DIM)
    return a, qw, kw, vw, qg, kg, vg


def swiglu_sublayer(x, shift, scale, gate, g_pre, g_post, w_gate_l, w_up_l, w_down_l):
    h = modulate(rms_norm(x, g_pre), shift, scale)
    f = (jax.nn.silu(h @ w_gate_l) * (h @ w_up_l)) @ w_down_l
    return x + gate * rms_norm(f, g_post)


def setup_inputs(seed: int = 0) -> dict:
    key = jax.random.key(seed)
    ks = jax.random.split(key, 20)
    f32 = jnp.float32

    def nrm(k, shape, scale):
        return jax.random.normal(k, shape, f32) * scale

    def gain(k, shape):
        return 1.0 + 0.05 * jax.random.normal(k, shape, f32)

    return {
        "x": nrm(ks[0], (BATCH, SEQ, D_MODEL), 1.0),
        "c": nrm(ks[1], (BATCH, D_MODEL), 1.0),
        "ctx": nrm(ks[2], (BATCH, CTX_LEN, D_MODEL), 1.0),
        "c_ctx": nrm(ks[3], (D_MODEL,), 1.0),
        "w_mod": nrm(ks[4], (DEPTH, D_MODEL, N_MOD * D_MODEL), 0.5 * D_MODEL ** -0.5),
        "b_mod": nrm(ks[5], (DEPTH, N_MOD * D_MODEL), 0.01),
        "g_pre_mix": gain(ks[6], (DEPTH, D_MODEL)),
        "g_post_mix": gain(ks[7], (DEPTH, D_MODEL)),
        "g_pre_ffn": gain(ks[8], (DEPTH, D_MODEL)),
        "g_post_ffn": gain(ks[9], (DEPTH, D_MODEL)),
        "w_in": nrm(ks[10], (DEPTH, D_MODEL, IN_WIDTH), D_MODEL ** -0.5),
        "w_pool": nrm(ks[11], (DEPTH, len(POOL_WINDOWS), POOL_GROUP, POOL_GROUP), POOL_GROUP ** -0.5),
        "pool_scale": gain(ks[12], (DEPTH, POOL_WIDTH)),
        "win_sink": nrm(ks[13], (DEPTH, WIN_HEADS), 0.5),
        "g_qnorm": gain(ks[14], (DEPTH, HEAD_DIM)),
        "g_knorm": gain(ks[15], (DEPTH, HEAD_DIM)),
        "w_out": nrm(ks[16], (DEPTH, MIX_WIDTH, D_MODEL), MIX_WIDTH ** -0.5),
        "w_gate": nrm(ks[17], (DEPTH, D_MODEL, D_FF), D_MODEL ** -0.5),
        "w_up": nrm(ks[18], (DEPTH, D_MODEL, D_FF), D_MODEL ** -0.5),
        "w_down": nrm(ks[19], (DEPTH, D_FF, D_MODEL), D_FF ** -0.5),
    }


def reference(x, c, ctx, c_ctx, w_mod, b_mod, g_pre_mix, g_post_mix, g_pre_ffn, g_post_ffn,
              w_in, w_pool, pool_scale, win_sink, g_qnorm, g_knorm, w_out, w_gate, w_up, w_down):
    b, n, d = x.shape
    rope = rope_tables(n)
    silu_c = jax.nn.silu(c)
    silu_cc = jax.nn.silu(c_ctx)
    xc = ctx
    for l in range(DEPTH):
        last = l == DEPTH - 1
        m = (silu_c @ w_mod[l] + b_mod[l]).reshape(b, N_MOD, 1, d)
        mc = (silu_cc @ w_mod[l] + b_mod[l]).reshape(N_MOD, 1, 1, d)
        sink = win_sink[l].reshape(WIN_KV_HEADS, WIN_GROUP)

        h = modulate(rms_norm(x, g_pre_mix[l]), m[:, 0], m[:, 1])
        hc = modulate(rms_norm(xc, g_pre_mix[l]), mc[0], mc[1])
        a, qw, kw, vw, qg, kg, vg = mixer_inputs(h, w_in[l], g_qnorm[l], g_knorm[l], rope)
        ac, qwc, kwc, vwc, qgc, kgc, vgc = mixer_inputs(hc, w_in[l], g_qnorm[l], g_knorm[l], None)

        y = jnp.concatenate([
            pool_mix(a, w_pool[l], pool_scale[l]),
            window_attention(qw, kw, vw, kwc, vwc, sink),
            global_attention(qg, kg, vg, kgc, vgc),
        ], axis=-1) @ w_out[l]
        x_new = x + m[:, 2] * rms_norm(y, g_post_mix[l])

        if not last:
            lc = xc.shape[1]
            yc = jnp.concatenate([
                pool_mix(ac, w_pool[l], pool_scale[l]),
                attend(qwc, [kwc], [vwc], sink=sink).reshape(b, lc, -1),
                attend(qgc, [kgc], [vgc]).reshape(b, lc, -1),
            ], axis=-1) @ w_out[l]
            xc = xc + mc[2] * rms_norm(yc, g_post_mix[l])
            xc = swiglu_sublayer(xc, mc[3], mc[4], mc[5], g_pre_ffn[l], g_post_ffn[l],
                                 w_gate[l], w_up[l], w_down[l])

        x = swiglu_sublayer(x_new, m[:, 3], m[:, 4], m[:, 5], g_pre_ffn[l], g_post_ffn[l],
                            w_gate[l], w_up[l], w_down[l])
    return x
```

```python
import functools

import jax
import jax.numpy as jnp
from jax import lax
from jax.experimental import pallas as pl
from jax.experimental.pallas import tpu as pltpu

F32 = jnp.float32
BF16 = jnp.bfloat16

HEAD_DIM = 64
GRID_W = 64
ROPE_FREQS = HEAD_DIM // 4
ROPE_THETA = 10000.0
NORM_EPS = 1e-6
NEG_INF = -1e30
ATTN_SCALE = HEAD_DIM ** -0.5
WINDOW = 128
POOL_WINDOWS = (2, 4, 8, 16)
POOL_HALO = 16
N_Q_HEADS = 6
N_KV_HEADS = 2
KV_GROUP = N_Q_HEADS // N_KV_HEADS
N_MOD = 6

LANES = 128
VMEM_LIMIT = 56 * 1024 * 1024


def _cparams(*sem):
    return pltpu.CompilerParams(dimension_semantics=sem, vmem_limit_bytes=VMEM_LIMIT)


def _resident(block_shape, index_map):
    return pl.BlockSpec(block_shape, index_map, pipeline_mode=pl.Buffered(1))


def _rms(x, g):
    ms = jnp.mean(x * x, axis=-1, keepdims=True)
    return (x * lax.rsqrt(ms + NORM_EPS)) * g


def _mod_kernel(c_ref, w_ref, b_ref, o_ref):
    c = c_ref[...]
    s = c * jax.nn.sigmoid(c)
    o_ref[...] = jnp.dot(s.astype(BF16), w_ref[...].astype(BF16), preferred_element_type=F32) + b_ref[...]


def _modulation(cc, w_mod, b_mod):
    depth, d, dm = w_mod.shape
    rows = cc.shape[0]
    tn = 1536
    return pl.pallas_call(
        _mod_kernel,
        out_shape=jax.ShapeDtypeStruct((depth, rows, dm), F32),
        grid=(depth, dm // tn),
        in_specs=[
            pl.BlockSpec((rows, d), lambda l, j: (0, 0)),
            pl.BlockSpec((None, d, tn), lambda l, j: (l, 0, j)),
            pl.BlockSpec((None, 1, tn), lambda l, j: (l, 0, j)),
        ],
        out_specs=pl.BlockSpec((None, rows, tn), lambda l, j: (l, 0, j)),
        compiler_params=_cparams("arbitrary", "arbitrary"),
        name="modulation",
    )(cc, w_mod, b_mod.reshape(depth, 1, dm))


def _in_proj_kernel(x_ref, mod_ref, g_ref, w_ref, cos_ref, sin_ref, gq_ref, gk_ref,
                    a_ref, qw_ref, ktw_ref, vw_ref, qg_ref, ktg_ref, vg_ref):
    tm = x_ref.shape[0]
    h = _rms(x_ref[...], g_ref[...])
    h = h * (1.0 + mod_ref[1:2, :]) + mod_ref[0:1, :]
    u = jnp.dot(h.astype(BF16), w_ref[...], preferred_element_type=F32)

    lane = lax.broadcasted_iota(jnp.int32, (tm, LANES), 1)
    lo = lane < HEAD_DIM
    first = (lane & ROPE_FREQS) == 0
    cos = cos_ref[...]
    sin = sin_ref[...]

    def rope(t):
        partner = jnp.where(first, pltpu.roll(t, LANES - ROPE_FREQS, 1), pltpu.roll(t, ROPE_FREQS, 1))
        return t * cos + partner * sin

    def head_norm(t, g):
        sq = t * t
        s_lo = jnp.sum(jnp.where(lo, sq, 0.0), axis=-1, keepdims=True)
        s_hi = jnp.sum(jnp.where(lo, 0.0, sq), axis=-1, keepdims=True)
        ms = jnp.where(lo, s_lo, s_hi) * (1.0 / HEAD_DIM)
        return (t * lax.rsqrt(ms + NORM_EPS)) * g

    def store_q(q_ref, chunks):
        for hd in range(N_Q_HEADS):
            t = chunks[hd // 2]
            src_hi, dst_hi = hd % 2, hd // KV_GROUP
            if src_hi != dst_hi:
                t = pltpu.roll(t, HEAD_DIM, 1)
            keep = lo if dst_hi == 0 else jnp.logical_not(lo)
            q_ref[hd] = jnp.where(keep, t, 0.0).astype(BF16)

    def store_kt(kt_ref, k):
        for r in range(tm // LANES):
            kt_ref[r] = jnp.transpose(k[r * LANES:(r + 1) * LANES, :]).astype(BF16)

    a_ref[...] = u[:, 0:256]

    store_q(qw_ref, [rope(u[:, 256 + LANES * c:384 + LANES * c]) * ATTN_SCALE for c in range(3)])
    store_kt(ktw_ref, rope(u[:, 640:768]))
    vw_ref[...] = u[:, 768:896].astype(BF16)

    gq = gq_ref[...]
    store_q(qg_ref, [rope(head_norm(u[:, 896 + LANES * c:1024 + LANES * c], gq)) * ATTN_SCALE for c in range(3)])
    store_kt(ktg_ref, rope(head_norm(u[:, 1280:1408], gk_ref[...])))
    vg_ref[...] = u[:, 1408:1536].astype(BF16)


def _in_proj(x, mod, g_pre, w_in, cos, sin, gq, gk, tm):
    b, n, d = x.shape
    nw = w_in.shape[1]
    qshape = jax.ShapeDtypeStruct((b, N_Q_HEADS, n, LANES), BF16)
    ktshape = jax.ShapeDtypeStruct((b, n // LANES, LANES, LANES), BF16)
    vshape = jax.ShapeDtypeStruct((b, n, LANES), BF16)
    qspec = pl.BlockSpec((None, N_Q_HEADS, tm, LANES), lambda bi, i: (bi, 0, i, 0))
    ktspec = pl.BlockSpec((None, tm // LANES, LANES, LANES), lambda bi, i: (bi, i, 0, 0))
    vspec = pl.BlockSpec((None, tm, LANES), lambda bi, i: (bi, i, 0))
    row = lambda bi, i: (0, 0)
    return pl.pallas_call(
        _in_proj_kernel,
        out_shape=(jax.ShapeDtypeStruct((b, n, 256), F32), qshape, ktshape, vshape, qshape, ktshape, vshape),
        grid=(b, n // tm),
        in_specs=[
            pl.BlockSpec((None, tm, d), lambda bi, i: (bi, i, 0)),
            pl.BlockSpec((None, N_MOD, d), lambda bi, i: (bi, 0, 0)),
            pl.BlockSpec((1, d), row),
            _resident((d, nw), row),
            pl.BlockSpec((tm, LANES), lambda bi, i: (i, 0)),
            pl.BlockSpec((tm, LANES), lambda bi, i: (i, 0)),
            pl.BlockSpec((1, LANES), row),
            pl.BlockSpec((1, LANES), row),
        ],
        out_specs=(pl.BlockSpec((None, tm, 256), lambda bi, i: (bi, i, 0)),
                   qspec, ktspec, vspec, qspec, ktspec, vspec),
        compiler_params=_cparams("arbitrary", "arbitrary"),
        name="in_proj",
    )(x, mod, g_pre, w_in, cos, sin, gq, gk)


def _merge_heads(outs, lo):
    chunks = []
    for c in range(N_Q_HEADS // 2):
        halves = []
        for dst_hi in (0, 1):
            hd = 2 * c + dst_hi
            t = outs[hd]
            if hd // KV_GROUP != dst_hi:
                t = pltpu.roll(t, HEAD_DIM, 1)
            halves.append(t)
        chunks.append(jnp.where(lo, halves[0], halves[1]))
    return chunks


def _key_tile(kt_ref, first_block, n_blocks):
    return jnp.concatenate([kt_ref[first_block + j] for j in range(n_blocks)], axis=1)


def _local_attn_kernel(*refs, band, has_ctx, has_sink):
    refs = list(refs)
    q_ref, kt_ref, v_ref = refs[:3]
    pos = 3
    if has_ctx:
        ktc_ref, vc_ref = refs[pos:pos + 2]
        pos += 2
    if has_sink:
        sink_ref = refs[pos]
        pos += 1
    o_ref = refs[pos]

    tq = q_ref.shape[1]
    nkb = kt_ref.shape[0]
    lane = lax.broadcasted_iota(jnp.int32, (tq, LANES), 1)
    lo = lane < HEAD_DIM

    if band:
        span = tq + 2 * WINDOW
        sb = span // LANES
        i = pl.program_id(1)
        kb = jnp.clip(i * (tq // LANES) - WINDOW // LANES, 0, nkb - sb)
        kt = _key_tile(kt_ref, kb, sb)
        k0 = pl.multiple_of(kb * LANES, LANES)
        v = v_ref[pl.ds(k0, span), :]
        qpos = i * tq + lax.broadcasted_iota(jnp.int32, (tq, span), 0)
        kpos = k0 + lax.broadcasted_iota(jnp.int32, (tq, span), 1)
        mask = jnp.abs(kpos - qpos) <= WINDOW
    else:
        kt = _key_tile(kt_ref, 0, nkb)
        v = v_ref[...]
    if has_ctx:
        ktc = _key_tile(ktc_ref, 0, ktc_ref.shape[0])
        vc = vc_ref[...]

    outs = []
    for hd in range(N_Q_HEADS):
        q = q_ref[hd]
        s = jnp.dot(q, kt, preferred_element_type=F32)
        if band:
            s = jnp.where(mask, s, NEG_INF)
        m = jnp.max(s, axis=-1, keepdims=True)
        if has_ctx:
            sc = jnp.dot(q, ktc, preferred_element_type=F32)
            m = jnp.maximum(m, jnp.max(sc, axis=-1, keepdims=True))
        if has_sink:
            sk = sink_ref[hd:hd + 1, 0:1]
            m = jnp.maximum(m, sk)
        p = jnp.exp(s - m)
        l = jnp.sum(p, axis=-1, keepdims=True)
        pv = jnp.dot(p.astype(BF16), v, preferred_element_type=F32)
        if has_ctx:
            pc = jnp.exp(sc - m)
            l = l + jnp.sum(pc, axis=-1, keepdims=True)
            pv = pv + jnp.dot(pc.astype(BF16), vc, preferred_element_type=F32)
        if has_sink:
            l = l + jnp.exp(sk - m)
        outs.append(pv / l)

    for c, chunk in enumerate(_merge_heads(outs, lo)):
        o_ref[:, c * LANES:(c + 1) * LANES] = chunk.astype(BF16)


def _local_attn(q, kt, v, ctx_kv, sink, band, tq):
    b, _, n, _ = q.shape
    nk = v.shape[1]
    batch3 = lambda bi, i: (bi, 0, 0)
    batch4 = lambda bi, i: (bi, 0, 0, 0)
    args = [q, kt, v]
    in_specs = [
        pl.BlockSpec((None, N_Q_HEADS, tq, LANES), lambda bi, i: (bi, 0, i, 0)),
        pl.BlockSpec((None, nk // LANES, LANES, LANES), batch4),
        pl.BlockSpec((None, nk, LANES), batch3),
    ]
    if ctx_kv is not None:
        ktc, vc = ctx_kv
        lc = vc.shape[1]
        args += [ktc, vc]
        in_specs += [pl.BlockSpec((None, lc // LANES, LANES, LANES), batch4),
                     pl.BlockSpec((None, lc, LANES), batch3)]
    if sink is not None:
        args.append(sink)
        in_specs.append(pl.BlockSpec(sink.shape, lambda bi, i: (0, 0)))
    if band:
        assert nk == n and n >= tq + 2 * WINDOW
    return pl.pallas_call(
        functools.partial(_local_attn_kernel, band=band, has_ctx=ctx_kv is not None, has_sink=sink is not None),
        out_shape=jax.ShapeDtypeStruct((b, n, N_Q_HEADS * HEAD_DIM), BF16),
        grid=(b, n // tq),
        in_specs=in_specs,
        out_specs=pl.BlockSpec((None, tq, N_Q_HEADS * HEAD_DIM), lambda bi, i: (bi, i, 0)),
        compiler_params=_cparams("arbitrary", "arbitrary"),
        name="window_attn" if band else "ctx_attn",
    )(*args)


def _global_attn_kernel(q_ref, kt_ref, v_ref, ktc_ref, vc_ref, o_ref, m_ref, l_ref, acc_ref, *, tk):
    tq = q_ref.shape[1]
    kb_per_step = tk // LANES
    n_steps = kt_ref.shape[0] // kb_per_step
    lane = lax.broadcasted_iota(jnp.int32, (tq, LANES), 1)
    lo = lane < HEAD_DIM

    m_ref[...] = jnp.full(m_ref.shape, NEG_INF, F32)
    l_ref[...] = jnp.zeros(l_ref.shape, F32)
    acc_ref[...] = jnp.zeros(acc_ref.shape, F32)

    def update(kt, v):
        width = kt.shape[1]
        for hd in range(N_Q_HEADS):
            s = jnp.dot(q_ref[hd], kt, preferred_element_type=F32)
            m_old = m_ref[hd]
            m_new = jnp.maximum(m_old, jnp.max(s, axis=-1, keepdims=True))
            alpha = jnp.exp(m_old - m_new)
            p = jnp.exp(s - m_new)
            psum = p[:, 0:LANES]
            for j in range(1, width // LANES):
                psum = psum + p[:, j * LANES:(j + 1) * LANES]
            l_ref[hd] = alpha * l_ref[hd] + psum
            acc_ref[hd] = alpha * acc_ref[hd] + jnp.dot(p.astype(BF16), v, preferred_element_type=F32)
            m_ref[hd] = m_new

    def step(kc, carry):
        kt = _key_tile(kt_ref, kc * kb_per_step, kb_per_step)
        v = v_ref[pl.ds(pl.multiple_of(kc * tk, tk), tk), :]
        update(kt, v)
        return carry

    lax.fori_loop(0, n_steps, step, 0)
    update(_key_tile(ktc_ref, 0, ktc_ref.shape[0]), vc_ref[...])

    outs = [acc_ref[hd] / jnp.sum(l_ref[hd], axis=-1, keepdims=True) for hd in range(N_Q_HEADS)]
    for c, chunk in enumerate(_merge_heads(outs, lo)):
        o_ref[:, c * LANES:(c + 1) * LANES] = chunk.astype(BF16)


def _global_attn(q, kt, v, ktc, vc, tq, tk):
    b, _, n, _ = q.shape
    lc = vc.shape[1]
    batch3 = lambda bi, i: (bi, 0, 0)
    batch4 = lambda bi, i: (bi, 0, 0, 0)
    return pl.pallas_call(
        functools.partial(_global_attn_kernel, tk=tk),
        out_shape=jax.ShapeDtypeStruct((b, n, N_Q_HEADS * HEAD_DIM), BF16),
        grid=(b, n // tq),
        in_specs=[
            pl.BlockSpec((None, N_Q_HEADS, tq, LANES), lambda bi, i: (bi, 0, i, 0)),
            pl.BlockSpec((None, n // LANES, LANES, LANES), batch4),
            pl.BlockSpec((None, n, LANES), batch3),
            pl.BlockSpec((None, lc // LANES, LANES, LANES), batch4),
            pl.BlockSpec((None, lc, LANES), batch3),
        ],
        out_specs=pl.BlockSpec((None, tq, N_Q_HEADS * HEAD_DIM), lambda bi, i: (bi, i, 0)),
        scratch_shapes=[
            pltpu.VMEM((N_Q_HEADS, tq, 1), F32),
            pltpu.VMEM((N_Q_HEADS, tq, LANES), F32),
            pltpu.VMEM((N_Q_HEADS, tq, LANES), F32),
        ],
        compiler_params=_cparams("arbitrary", "arbitrary"),
        name="global_attn",
    )(q, kt, v, ktc, vc)


def _out_proj_kernel(x_ref, a_ref, ap_ref, an_ref, ow_ref, og_ref, wpool_ref, pscale_ref, wo_ref, g_ref, mod_ref,
                     o_ref, ext_ref, *, n):
    i = pl.program_id(1)
    tm = x_ref.shape[0]
    ext_ref[0:POOL_HALO, :] = jnp.where(i > 0, ap_ref[...], 0.0)
    ext_ref[POOL_HALO:POOL_HALO + tm, :] = a_ref[...]
    ext_ref[POOL_HALO + tm:2 * POOL_HALO + tm, :] = jnp.where(i < pl.num_programs(1) - 1, an_ref[...], 0.0)

    t = i * tm + lax.broadcasted_iota(jnp.int32, (tm, 1), 0)
    lo = lax.broadcasted_iota(jnp.int32, (tm, LANES), 1) < HEAD_DIM

    def count(w):
        hi_t = jnp.minimum(t + (w - 1 - w // 2), n - 1)
        lo_t = jnp.maximum(t - w // 2, 0)
        return (hi_t - lo_t + 1).astype(F32)

    feats = []
    for c in range(2):
        w_a, w_b = POOL_WINDOWS[2 * c], POOL_WINDOWS[2 * c + 1]

        def shifted(k, c=c):
            return ext_ref[pl.ds(POOL_HALO + k, tm), c * LANES:(c + 1) * LANES]

        x0 = shifted(0)
        acc = x0
        sums = {}
        reach = 0
        for w in (w_a, w_b):
            while reach < w // 2:
                reach += 1
                acc = acc + shifted(-reach)
                if reach - 1 >= 1:
                    acc = acc + shifted(reach - 1)
            sums[w] = acc
        mean = jnp.where(lo, sums[w_a] / count(w_a), sums[w_b] / count(w_b))
        feats.append(mean - x0)
    feats = jnp.concatenate(feats, axis=1).astype(BF16)
    y_pool = jnp.dot(feats, wpool_ref[...], preferred_element_type=F32) * pscale_ref[...]

    pw = y_pool.shape[1]
    aw = ow_ref.shape[1]
    y = jnp.dot(y_pool.astype(BF16), wo_ref[0:pw, :], preferred_element_type=F32)
    y = y + jnp.dot(ow_ref[...], wo_ref[pw:pw + aw, :], preferred_element_type=F32)
    y = y + jnp.dot(og_ref[...], wo_ref[pw + aw:pw + 2 * aw, :], preferred_element_type=F32)
    o_ref[...] = x_ref[...] + mod_ref[2:3, :] * _rms(y, g_ref[...])


def _out_proj(x, a, o_win, o_glb, w_pool_bd, pool_scale, w_out, g_post, mod, tm):
    b, n, d = x.shape
    pw = a.shape[2]
    aw = o_win.shape[2]
    hb = tm // POOL_HALO
    row = lambda bi, i: (0, 0)
    tile = lambda bi, i: (bi, i, 0)
    return pl.pallas_call(
        functools.partial(_out_proj_kernel, n=n),
        out_shape=jax.ShapeDtypeStruct((b, n, d), F32),
        grid=(b, n // tm),
        in_specs=[
            pl.BlockSpec((None, tm, d), tile),
            pl.BlockSpec((None, tm, pw), tile),
            pl.BlockSpec((None, POOL_HALO, pw), lambda bi, i: (bi, jnp.maximum(i * hb - 1, 0), 0)),
            pl.BlockSpec((None, POOL_HALO, pw), lambda bi, i: (bi, jnp.minimum((i + 1) * hb, n // POOL_HALO - 1), 0)),
            pl.BlockSpec((None, tm, aw), tile),
            pl.BlockSpec((None, tm, aw), tile),
            _resident((pw, pw), row),
            pl.BlockSpec((1, pw), row),
            _resident((d, d), row),
            pl.BlockSpec((1, d), row),
            pl.BlockSpec((None, N_MOD, d), lambda bi, i: (bi, 0, 0)),
        ],
        out_specs=pl.BlockSpec((None, tm, d), tile),
        scratch_shapes=[pltpu.VMEM((tm + 2 * POOL_HALO, pw), F32)],
        compiler_params=_cparams("arbitrary", "arbitrary"),
        name="out_proj",
    )(x, a, a, a, o_win, o_glb, w_pool_bd, pool_scale, w_out, g_post, mod)


def _ffn_kernel(x_ref, mod_ref, gpre_ref, gpost_ref, wg_ref, wu_ref, wd_ref, o_ref, *, chunk):
    x = x_ref[...]
    h = _rms(x, gpre_ref[...])
    h = (h * (1.0 + mod_ref[4:5, :]) + mod_ref[3:4, :]).astype(BF16)
    f = None
    for c in range(wg_ref.shape[1] // chunk):
        cols = slice(c * chunk, (c + 1) * chunk)
        g = jnp.dot(h, wg_ref[:, cols], preferred_element_type=F32)
        u = jnp.dot(h, wu_ref[:, cols], preferred_element_type=F32)
        act = ((g * jax.nn.sigmoid(g)) * u).astype(BF16)
        part = jnp.dot(act, wd_ref[cols, :], preferred_element_type=F32)
        f = part if f is None else f + part
    o_ref[...] = x + mod_ref[5:6, :] * _rms(f, gpost_ref[...])


def _ffn(x, mod, g_pre, g_post, w_gate, w_up, w_down, tm, chunk):
    b, n, d = x.shape
    dff = w_gate.shape[1]
    row = lambda bi, i: (0, 0)
    tile = lambda bi, i: (bi, i, 0)
    return pl.pallas_call(
        functools.partial(_ffn_kernel, chunk=chunk),
        out_shape=jax.ShapeDtypeStruct((b, n, d), F32),
        grid=(b, n // tm),
        in_specs=[
            pl.BlockSpec((None, tm, d), tile),
            pl.BlockSpec((None, N_MOD, d), lambda bi, i: (bi, 0, 0)),
            pl.BlockSpec((1, d), row),
            pl.BlockSpec((1, d), row),
            _resident((d, dff), row),
            _resident((d, dff), row),
            _resident((dff, d), row),
        ],
        out_specs=pl.BlockSpec((None, tm, d), tile),
        compiler_params=_cparams("arbitrary", "arbitrary"),
        name="ffn",
    )(x, mod, g_pre, g_post, w_gate, w_up, w_down)


def _rope_tables(n):
    rows = n // GRID_W
    row = jnp.broadcast_to(jnp.arange(rows)[:, None], (rows, GRID_W)).reshape(-1).astype(F32)
    col = jnp.broadcast_to(jnp.arange(GRID_W)[None, :], (rows, GRID_W)).reshape(-1).astype(F32)
    freq = ROPE_THETA ** (-jnp.arange(ROPE_FREQS, dtype=F32) / ROPE_FREQS)
    ar, ac = row[:, None] * freq, col[:, None] * freq
    cos = jnp.concatenate([jnp.cos(ar), jnp.cos(ar), jnp.cos(ac), jnp.cos(ac)], axis=1)
    sin = jnp.concatenate([-jnp.sin(ar), jnp.sin(ar), -jnp.sin(ac), jnp.sin(ac)], axis=1)
    return jnp.tile(cos, (1, LANES // HEAD_DIM)), jnp.tile(sin, (1, LANES // HEAD_DIM))


def _tile(n, pref):
    t = min(n, pref)
    assert n % t == 0, (n, t)
    return t


def kernel(x, c, ctx, c_ctx, w_mod, b_mod, g_pre_mix, g_post_mix, g_pre_ffn, g_post_ffn, w_in, w_pool, pool_scale, win_sink, g_qnorm, g_knorm, w_out, w_gate, w_up, w_down):
    b, n, d = x.shape
    lc = ctx.shape[1]
    depth = w_mod.shape[0]
    assert d == 1024 and w_in.shape[2] == 1536 and n % GRID_W == 0

    cos, sin = _rope_tables(n)
    cos_c, sin_c = jnp.ones((lc, LANES), F32), jnp.zeros((lc, LANES), F32)

    mod_rows = -(-(b + 1) // 8) * 8
    cc = jnp.concatenate([c, c_ctx[None, :], jnp.zeros((mod_rows - b - 1, d), F32)], axis=0)
    mod_all = _modulation(cc, w_mod, b_mod)

    tm = _tile(n, 512)
    tq_win = _tile(n, 256)
    tq_glb = _tile(n, 256)
    tk_glb = _tile(n, 512)
    ffn_chunk = 256

    xc = ctx
    for l in range(depth):
        last = l == depth - 1
        m = mod_all[l, :b].reshape(b, N_MOD, d)
        mc = jnp.broadcast_to(mod_all[l, b].reshape(1, N_MOD, d), (b, N_MOD, d))
        w_in_l = w_in[l].astype(BF16)
        w_out_l = w_out[l].astype(BF16)
        w_gate_l, w_up_l, w_down_l = w_gate[l].astype(BF16), w_up[l].astype(BF16), w_down[l].astype(BF16)
        w_pool_bd = jax.scipy.linalg.block_diag(*[w_pool[l, g] for g in range(len(POOL_WINDOWS))]).astype(BF16)
        pscale = pool_scale[l][None, :]
        gq = jnp.tile(g_qnorm[l], LANES // HEAD_DIM)[None, :]
        gk = jnp.tile(g_knorm[l], LANES // HEAD_DIM)[None, :]
        sink = jnp.zeros((8, LANES), F32).at[:N_Q_HEADS].set(jnp.broadcast_to(win_sink[l][:, None], (N_Q_HEADS, LANES)))
        g_pre, g_post = g_pre_mix[l][None, :], g_post_mix[l][None, :]
        g_pre_f, g_post_f = g_pre_ffn[l][None, :], g_post_ffn[l][None, :]

        a, qw, ktw, vw, qg, ktg, vg = _in_proj(x, m, g_pre, w_in_l, cos, sin, gq, gk, tm)
        ac, qwc, ktwc, vwc, qgc, ktgc, vgc = _in_proj(xc, mc, g_pre, w_in_l, cos_c, sin_c, gq, gk, lc)

        o_win = _local_attn(qw, ktw, vw, (ktwc, vwc), sink, True, tq_win)
        o_glb = _global_attn(qg, ktg, vg, ktgc, vgc, tq_glb, tk_glb)
        x_new = _out_proj(x, a, o_win, o_glb, w_pool_bd, pscale, w_out_l, g_post, m, tm)

        if not last:
            oc_win = _local_attn(qwc, ktwc, vwc, None, sink, False, lc)
            oc_glb = _local_attn(qgc, ktgc, vgc, None, None, False, lc)
            xc = _out_proj(xc, ac, oc_win, oc_glb, w_pool_bd, pscale, w_out_l, g_post, mc, lc)
            xc = _ffn(xc, mc, g_pre_f, g_post_f, w_gate_l, w_up_l, w_down_l, lc, ffn_chunk)

        x = _ffn(x_new, m, g_pre_f, g_post_f, w_gate_l, w_up_l, w_down_l, tm, ffn_chunk)
    return x
```

```python
import functools

import jax
import jax.numpy as jnp
from jax import lax
from jax.experimental import pallas as pl
from jax.experimental.pallas import tpu as pltpu

F32 = jnp.float32
BF16 = jnp.bfloat16

HEAD_DIM = 64
GRID_W = 64
ROPE_FREQS = HEAD_DIM // 4
ROPE_THETA = 10000.0
NORM_EPS = 1e-6
NEG_INF = -1e30
ATTN_SCALE = HEAD_DIM ** -0.5
WINDOW = 128
POOL_WINDOWS = (2, 4, 8, 16)
POOL_HALO = 16
N_Q_HEADS = 6
N_KV_HEADS = 2
KV_GROUP = N_Q_HEADS // N_KV_HEADS
N_MOD = 6
N_SHIFT = 3
SAFE_SHIFT = 30.0

LANES = 128
VMEM_LIMIT = 56 * 1024 * 1024


def _cparams(*sem):
    return pltpu.CompilerParams(dimension_semantics=sem, vmem_limit_bytes=VMEM_LIMIT)


def _resident(block_shape, index_map):
    return pl.BlockSpec(block_shape, index_map, pipeline_mode=pl.Buffered(1))


def _rms(x, g):
    ms = jnp.mean(x * x, axis=-1, keepdims=True)
    return (x * lax.rsqrt(ms + NORM_EPS)) * g


def _mod_kernel(c_ref, w_ref, b_ref, o_ref):
    c = c_ref[...]
    s = c * jax.nn.sigmoid(c)
    o_ref[...] = jnp.dot(s.astype(BF16), w_ref[...].astype(BF16), preferred_element_type=F32) + b_ref[...]


def _modulation(cc, w_mod, b_mod):
    depth, d, dm = w_mod.shape
    rows = cc.shape[0]
    tn = 1536
    return pl.pallas_call(
        _mod_kernel,
        out_shape=jax.ShapeDtypeStruct((depth, rows, dm), F32),
        grid=(depth, dm // tn),
        in_specs=[
            pl.BlockSpec((rows, d), lambda l, j: (0, 0)),
            pl.BlockSpec((None, d, tn), lambda l, j: (l, 0, j)),
            pl.BlockSpec((None, 1, tn), lambda l, j: (l, 0, j)),
        ],
        out_specs=pl.BlockSpec((None, rows, tn), lambda l, j: (l, 0, j)),
        compiler_params=_cparams("arbitrary", "arbitrary"),
        name="modulation",
    )(cc, w_mod, b_mod.reshape(depth, 1, dm))


def _in_proj_kernel(x_ref, mod_ref, g_ref, w_ref, cos_ref, sin_ref, gq_ref, gk_ref,
                    a_ref, qw_ref, ktw_ref, vw_ref, qg_ref, ktg_ref, vg_ref):
    tm = x_ref.shape[0]
    h = _rms(x_ref[...], g_ref[...])
    h = h * (1.0 + mod_ref[1:2, :]) + mod_ref[0:1, :]
    u = jnp.dot(h.astype(BF16), w_ref[...], preferred_element_type=F32)

    lane = lax.broadcasted_iota(jnp.int32, (tm, LANES), 1)
    lo = lane < HEAD_DIM
    first = (lane & ROPE_FREQS) == 0
    cos = cos_ref[...]
    sin = sin_ref[...]

    def rope(t):
        partner = jnp.where(first, pltpu.roll(t, LANES - ROPE_FREQS, 1), pltpu.roll(t, ROPE_FREQS, 1))
        return t * cos + partner * sin

    def head_norm(t, g):
        sq = t * t
        s_lo = jnp.sum(jnp.where(lo, sq, 0.0), axis=-1, keepdims=True)
        s_hi = jnp.sum(jnp.where(lo, 0.0, sq), axis=-1, keepdims=True)
        ms = jnp.where(lo, s_lo, s_hi) * (1.0 / HEAD_DIM)
        return (t * lax.rsqrt(ms + NORM_EPS)) * g

    def store_q(q_ref, chunks, at_kv_half):
        for hd in range(N_Q_HEADS):
            t = chunks[hd // 2]
            src_hi, dst_hi = hd % 2, (hd // KV_GROUP if at_kv_half else 0)
            if src_hi != dst_hi:
                t = pltpu.roll(t, HEAD_DIM, 1)
            keep = lo if dst_hi == 0 else jnp.logical_not(lo)
            q_ref[hd] = jnp.where(keep, t, 0.0).astype(BF16)

    a_ref[...] = u[:, 0:256]

    store_q(qw_ref, [rope(u[:, 256 + LANES * c:384 + LANES * c]) * ATTN_SCALE for c in range(3)], True)
    kw = rope(u[:, 640:768])
    for r in range(tm // LANES):
        ktw_ref[r] = jnp.transpose(kw[r * LANES:(r + 1) * LANES, :]).astype(BF16)
    vw_ref[...] = u[:, 768:896].astype(BF16)

    gq = gq_ref[...]
    store_q(qg_ref, [rope(head_norm(u[:, 896 + LANES * c:1024 + LANES * c], gq)) * ATTN_SCALE for c in range(3)], False)
    kg = rope(head_norm(u[:, 1280:1408], gk_ref[...]))
    shift_rows = (lax.broadcasted_iota(jnp.int32, (HEAD_DIM, LANES), 0) < N_SHIFT).astype(BF16)
    for r in range(tm // LANES):
        kt = jnp.transpose(kg[r * LANES:(r + 1) * LANES, :]).astype(BF16)
        for kv in range(N_KV_HEADS):
            ktg_ref[kv, r, 0:HEAD_DIM, :] = kt[kv * HEAD_DIM:(kv + 1) * HEAD_DIM, :]
            ktg_ref[kv, r, HEAD_DIM:LANES, :] = shift_rows
    vg = u[:, 1408:1536]
    sum_lane = (lane == HEAD_DIM).astype(F32)
    vg_ref[0] = jnp.where(lo, vg, sum_lane).astype(BF16)
    vg_ref[1] = jnp.where(lo, pltpu.roll(vg, HEAD_DIM, 1), sum_lane).astype(BF16)


def _in_proj(x, mod, g_pre, w_in, cos, sin, gq, gk, tm):
    b, n, d = x.shape
    nw = w_in.shape[1]
    qshape = jax.ShapeDtypeStruct((b, N_Q_HEADS, n, LANES), BF16)
    ktshape = jax.ShapeDtypeStruct((b, n // LANES, LANES, LANES), BF16)
    vshape = jax.ShapeDtypeStruct((b, n, LANES), BF16)
    qspec = pl.BlockSpec((None, N_Q_HEADS, tm, LANES), lambda bi, i: (bi, 0, i, 0))
    ktspec = pl.BlockSpec((None, tm // LANES, LANES, LANES), lambda bi, i: (bi, i, 0, 0))
    vspec = pl.BlockSpec((None, tm, LANES), lambda bi, i: (bi, i, 0))
    ktg_shape = jax.ShapeDtypeStruct((b, N_KV_HEADS, n // LANES, LANES, LANES), BF16)
    vg_shape = jax.ShapeDtypeStruct((b, N_KV_HEADS, n, LANES), BF16)
    ktg_spec = pl.BlockSpec((None, N_KV_HEADS, tm // LANES, LANES, LANES), lambda bi, i: (bi, 0, i, 0, 0))
    vg_spec = pl.BlockSpec((None, N_KV_HEADS, tm, LANES), lambda bi, i: (bi, 0, i, 0))
    row = lambda bi, i: (0, 0)
    return pl.pallas_call(
        _in_proj_kernel,
        out_shape=(jax.ShapeDtypeStruct((b, n, 256), F32), qshape, ktshape, vshape, qshape, ktg_shape, vg_shape),
        grid=(b, n // tm),
        in_specs=[
            pl.BlockSpec((None, tm, d), lambda bi, i: (bi, i, 0)),
            pl.BlockSpec((None, N_MOD, d), lambda bi, i: (bi, 0, 0)),
            pl.BlockSpec((1, d), row),
            _resident((d, nw), row),
            pl.BlockSpec((tm, LANES), lambda bi, i: (i, 0)),
            pl.BlockSpec((tm, LANES), lambda bi, i: (i, 0)),
            pl.BlockSpec((1, LANES), row),
            pl.BlockSpec((1, LANES), row),
        ],
        out_specs=(pl.BlockSpec((None, tm, 256), lambda bi, i: (bi, i, 0)),
                   qspec, ktspec, vspec, qspec, ktg_spec, vg_spec),
        compiler_params=_cparams("arbitrary", "arbitrary"),
        name="in_proj",
    )(x, mod, g_pre, w_in, cos, sin, gq, gk)


def _merge_heads(outs, lo, at_kv_half):
    chunks = []
    for c in range(N_Q_HEADS // 2):
        halves = []
        for dst_hi in (0, 1):
            hd = 2 * c + dst_hi
            t = outs[hd]
            src_hi = hd // KV_GROUP if at_kv_half else 0
            if src_hi != dst_hi:
                t = pltpu.roll(t, HEAD_DIM, 1)
            halves.append(t)
        chunks.append(jnp.where(lo, halves[0], halves[1]))
    return chunks


def _key_tile(kt_ref, first_block, n_blocks):
    return jnp.concatenate([kt_ref[first_block + j] for j in range(n_blocks)], axis=1)


def _local_attn_kernel(*refs, band, has_ctx, has_sink, per_kv):
    assert not (per_kv and (band or has_ctx))
    refs = list(refs)
    q_ref, kt_ref, v_ref = refs[:3]
    pos = 3
    if has_ctx:
        ktc_ref, vc_ref = refs[pos:pos + 2]
        pos += 2
    if has_sink:
        sink_ref = refs[pos]
        pos += 1
    o_ref = refs[pos]

    tq = q_ref.shape[1]
    nkb = kt_ref.shape[-3]
    lane = lax.broadcasted_iota(jnp.int32, (tq, LANES), 1)
    lo = lane < HEAD_DIM

    if per_kv:
        tiles = [(_key_tile(kt_ref.at[kv], 0, nkb), v_ref[kv]) for kv in range(N_KV_HEADS)]
    elif band:
        span = tq + 2 * WINDOW
        sb = span // LANES
        i = pl.program_id(1)
        kb = jnp.clip(i * (tq // LANES) - WINDOW // LANES, 0, nkb - sb)
        kt = _key_tile(kt_ref, kb, sb)
        k0 = pl.multiple_of(kb * LANES, LANES)
        v = v_ref[pl.ds(k0, span), :]
        qpos = i * tq + lax.broadcasted_iota(jnp.int32, (tq, span), 0)
        kpos = k0 + lax.broadcasted_iota(jnp.int32, (tq, span), 1)
        mask = jnp.abs(kpos - qpos) <= WINDOW
    else:
        kt = _key_tile(kt_ref, 0, nkb)
        v = v_ref[...]
    if has_ctx:
        ktc = _key_tile(ktc_ref, 0, ktc_ref.shape[0])
        vc = vc_ref[...]

    outs = []
    for hd in range(N_Q_HEADS):
        q = q_ref[hd]
        if per_kv:
            kt, v = tiles[hd // KV_GROUP]
        s = jnp.dot(q, kt, preferred_element_type=F32)
        if band:
            s = jnp.where(mask, s, NEG_INF)
        m = jnp.max(s, axis=-1, keepdims=True)
        if has_ctx:
            sc = jnp.dot(q, ktc, preferred_element_type=F32)
            m = jnp.maximum(m, jnp.max(sc, axis=-1, keepdims=True))
        if has_sink:
            sk = sink_ref[hd:hd + 1, 0:1]
            m = jnp.maximum(m, sk)
        p = jnp.exp(s - m)
        l = jnp.sum(p, axis=-1, keepdims=True)
        pv = jnp.dot(p.astype(BF16), v, preferred_element_type=F32)
        if has_ctx:
            pc = jnp.exp(sc - m)
            l = l + jnp.sum(pc, axis=-1, keepdims=True)
            pv = pv + jnp.dot(pc.astype(BF16), vc, preferred_element_type=F32)
        if has_sink:
            l = l + jnp.exp(sk - m)
        outs.append(pv / l)

    for c, chunk in enumerate(_merge_heads(outs, lo, not per_kv)):
        o_ref[:, c * LANES:(c + 1) * LANES] = chunk.astype(BF16)


def _local_attn(q, kt, v, ctx_kv, sink, band, tq):
    b, _, n, _ = q.shape
    per_kv = v.ndim == 4
    nk = v.shape[-2]
    batch3 = lambda bi, i: (bi, 0, 0)
    batch4 = lambda bi, i: (bi, 0, 0, 0)
    args = [q, kt, v]
    in_specs = [
        pl.BlockSpec((None, N_Q_HEADS, tq, LANES), lambda bi, i: (bi, 0, i, 0)),
        pl.BlockSpec((None,) + kt.shape[1:], lambda bi, i: (bi,) + (0,) * (kt.ndim - 1)),
        pl.BlockSpec((None,) + v.shape[1:], lambda bi, i: (bi,) + (0,) * (v.ndim - 1)),
    ]
    if ctx_kv is not None:
        ktc, vc = ctx_kv
        lc = vc.shape[1]
        args += [ktc, vc]
        in_specs += [pl.BlockSpec((None, lc // LANES, LANES, LANES), batch4),
                     pl.BlockSpec((None, lc, LANES), batch3)]
    if sink is not None:
        args.append(sink)
        in_specs.append(pl.BlockSpec(sink.shape, lambda bi, i: (0, 0)))
    if band:
        assert nk == n and n >= tq + 2 * WINDOW
    return pl.pallas_call(
        functools.partial(_local_attn_kernel, band=band, has_ctx=ctx_kv is not None, has_sink=sink is not None,
                          per_kv=per_kv),
        out_shape=jax.ShapeDtypeStruct((b, n, N_Q_HEADS * HEAD_DIM), BF16),
        grid=(b, n // tq),
        in_specs=in_specs,
        out_specs=pl.BlockSpec((None, tq, N_Q_HEADS * HEAD_DIM), lambda bi, i: (bi, i, 0)),
        compiler_params=_cparams("arbitrary", "arbitrary"),
        name="window_attn" if band else "ctx_attn",
    )(*args)


def _global_attn_kernel(*refs, tk, rb, exact_max):
    if exact_max:
        q_ref, kt_ref, v_ref, ktc_ref, vc_ref, o_ref, qs_ref, acc_ref, mx_ref = refs
    else:
        bound_ref, q_ref, kt_ref, v_ref, ktc_ref, vc_ref, o_ref, qs_ref, acc_ref = refs
    tq = q_ref.shape[1]
    rows = KV_GROUP * tq
    kb_per_step = tk // LANES
    n_steps = kt_ref.shape[1] // kb_per_step
    lane = lax.broadcasted_iota(jnp.int32, (rows, LANES), 1)

    def key_steps(fn):
        def step(kc, carry):
            for kv in range(N_KV_HEADS):
                fn(kv, _key_tile(kt_ref.at[kv], kc * kb_per_step, kb_per_step),
                   v_ref[kv, pl.ds(pl.multiple_of(kc * tk, tk), tk), :])
            return carry
        lax.fori_loop(0, n_steps, step, 0)
        for kv in range(N_KV_HEADS):
            fn(kv, _key_tile(ktc_ref.at[kv], 0, ktc_ref.shape[1]), vc_ref[kv])

    def q_rows(kv):
        return q_ref[kv * KV_GROUP:(kv + 1) * KV_GROUP].reshape(rows, LANES)

    if exact_max:
        mx_ref[...] = jnp.full(mx_ref.shape, NEG_INF, F32)

        def track_max(kv, kt, v):
            s = jnp.dot(q_rows(kv), kt, preferred_element_type=F32)
            mx = mx_ref[kv]
            for j in range(kt.shape[1] // LANES):
                mx = jnp.maximum(mx, s[:, j * LANES:(j + 1) * LANES])
            mx_ref[kv] = mx

        key_steps(track_max)

    for kv in range(N_KV_HEADS):
        if exact_max:
            m = jnp.max(mx_ref[kv], axis=-1, keepdims=True)
        else:
            m = jnp.broadcast_to(bound_ref[0:1, 0:1], (rows, 1))
        qs = q_rows(kv).astype(F32)
        rest = m
        for piece in range(N_SHIFT):
            part = rest.astype(BF16).astype(F32)
            qs = jnp.where(lane == HEAD_DIM + piece, -part, qs)
            rest = rest - part
        qs_ref[kv] = qs.astype(BF16)

    acc_ref[...] = jnp.zeros(acc_ref.shape, F32)

    def accumulate(kv, kt, v):
        for r in range(rows // rb):
            rsl = slice(r * rb, (r + 1) * rb)
            s = jnp.dot(qs_ref[kv, rsl, :], kt, preferred_element_type=F32)
            p = jnp.exp(s).astype(BF16)
            acc_ref[kv, rsl, :] += jnp.dot(p, v, preferred_element_type=F32)

    key_steps(accumulate)

    outs = []
    for hd in range(N_Q_HEADS):
        a = acc_ref[hd // KV_GROUP, (hd % KV_GROUP) * tq:(hd % KV_GROUP + 1) * tq, :]
        outs.append(a / a[:, HEAD_DIM:HEAD_DIM + 1])
    lo = lax.broadcasted_iota(jnp.int32, (tq, LANES), 1) < HEAD_DIM
    for c, chunk in enumerate(_merge_heads(outs, lo, False)):
        o_ref[:, c * LANES:(c + 1) * LANES] = chunk.astype(BF16)


def _global_attn(bound, q, kt, v, ktc, vc, *, tq, tk, rb, exact_max):
    b, _, n, _ = q.shape
    lc = vc.shape[2]
    rows = KV_GROUP * tq
    batch4 = lambda bi, i: (bi, 0, 0, 0)
    batch5 = lambda bi, i: (bi, 0, 0, 0, 0)
    args = [q, kt, v, ktc, vc]
    in_specs = [
        pl.BlockSpec((None, N_Q_HEADS, tq, LANES), lambda bi, i: (bi, 0, i, 0)),
        pl.BlockSpec((None, N_KV_HEADS, n // LANES, LANES, LANES), batch5),
        pl.BlockSpec((None, N_KV_HEADS, n, LANES), batch4),
        pl.BlockSpec((None, N_KV_HEADS, lc // LANES, LANES, LANES), batch5),
        pl.BlockSpec((None, N_KV_HEADS, lc, LANES), batch4),
    ]
    scratch = [pltpu.VMEM((N_KV_HEADS, rows, LANES), BF16), pltpu.VMEM((N_KV_HEADS, rows, LANES), F32)]
    if exact_max:
        scratch.append(pltpu.VMEM((N_KV_HEADS, rows, LANES), F32))
    else:
        args.insert(0, bound)
        in_specs.insert(0, pl.BlockSpec(bound.shape, lambda bi, i: (0, 0)))
    return pl.pallas_call(
        functools.partial(_global_attn_kernel, tk=tk, rb=rb, exact_max=exact_max),
        out_shape=jax.ShapeDtypeStruct((b, n, N_Q_HEADS * HEAD_DIM), BF16),
        grid=(b, n // tq),
        in_specs=in_specs,
        out_specs=pl.BlockSpec((None, tq, N_Q_HEADS * HEAD_DIM), lambda bi, i: (bi, i, 0)),
        scratch_shapes=scratch,
        compiler_params=_cparams("arbitrary", "arbitrary"),
        name="global_attn_exact_max" if exact_max else "global_attn",
    )(*args)


def _out_proj_kernel(x_ref, a_ref, ap_ref, an_ref, ow_ref, og_ref, wpool_ref, pscale_ref, wo_ref, g_ref, mod_ref,
                     o_ref, ext_ref, *, n):
    i = pl.program_id(1)
    tm = x_ref.shape[0]
    ext_ref[0:POOL_HALO, :] = jnp.where(i > 0, ap_ref[...], 0.0)
    ext_ref[POOL_HALO:POOL_HALO + tm, :] = a_ref[...]
    ext_ref[POOL_HALO + tm:2 * POOL_HALO + tm, :] = jnp.where(i < pl.num_programs(1) - 1, an_ref[...], 0.0)

    t = i * tm + lax.broadcasted_iota(jnp.int32, (tm, 1), 0)
    lo = lax.broadcasted_iota(jnp.int32, (tm, LANES), 1) < HEAD_DIM

    def count(w):
        hi_t = jnp.minimum(t + (w - 1 - w // 2), n - 1)
        lo_t = jnp.maximum(t - w // 2, 0)
        return (hi_t - lo_t + 1).astype(F32)

    feats = []
    for c in range(2):
        w_a, w_b = POOL_WINDOWS[2 * c], POOL_WINDOWS[2 * c + 1]

        def shifted(k, c=c):
            return ext_ref[pl.ds(POOL_HALO + k, tm), c * LANES:(c + 1) * LANES]

        x0 = shifted(0)
        acc = x0
        sums = {}
        reach = 0
        for w in (w_a, w_b):
            while reach < w // 2:
                reach += 1
                acc = acc + shifted(-reach)
                if reach - 1 >= 1:
                    acc = acc + shifted(reach - 1)
            sums[w] = acc
        mean = jnp.where(lo, sums[w_a] / count(w_a), sums[w_b] / count(w_b))
        feats.append(mean - x0)
    feats = jnp.concatenate(feats, axis=1).astype(BF16)
    y_pool = jnp.dot(feats, wpool_ref[...], preferred_element_type=F32) * pscale_ref[...]

    pw = y_pool.shape[1]
    aw = ow_ref.shape[1]
    y = jnp.dot(y_pool.astype(BF16), wo_ref[0:pw, :], preferred_element_type=F32)
    y = y + jnp.dot(ow_ref[...], wo_ref[pw:pw + aw, :], preferred_element_type=F32)
    y = y + jnp.dot(og_ref[...], wo_ref[pw + aw:pw + 2 * aw, :], preferred_element_type=F32)
    o_ref[...] = x_ref[...] + mod_ref[2:3, :] * _rms(y, g_ref[...])


def _out_proj(x, a, o_win, o_glb, w_pool_bd, pool_scale, w_out, g_post, mod, tm):
    b, n, d = x.shape
    pw = a.shape[2]
    aw = o_win.shape[2]
    hb = tm // POOL_HALO
    row = lambda bi, i: (0, 0)
    tile = lambda bi, i: (bi, i, 0)
    return pl.pallas_call(
        functools.partial(_out_proj_kernel, n=n),
        out_shape=jax.ShapeDtypeStruct((b, n, d), F32),
        grid=(b, n // tm),
        in_specs=[
            pl.BlockSpec((None, tm, d), tile),
            pl.BlockSpec((None, tm, pw), tile),
            pl.BlockSpec((None, POOL_HALO, pw), lambda bi, i: (bi, jnp.maximum(i * hb - 1, 0), 0)),
            pl.BlockSpec((None, POOL_HALO, pw), lambda bi, i: (bi, jnp.minimum((i + 1) * hb, n // POOL_HALO - 1), 0)),
            pl.BlockSpec((None, tm, aw), tile),
            pl.BlockSpec((None, tm, aw), tile),
            _resident((pw, pw), row),
            pl.BlockSpec((1, pw), row),
            _resident((d, d), row),
            pl.BlockSpec((1, d), row),
            pl.BlockSpec((None, N_MOD, d), lambda bi, i: (bi, 0, 0)),
        ],
        out_specs=pl.BlockSpec((None, tm, d), tile),
        scratch_shapes=[pltpu.VMEM((tm + 2 * POOL_HALO, pw), F32)],
        compiler_params=_cparams("arbitrary", "arbitrary"),
        name="out_proj",
    )(x, a, a, a, o_win, o_glb, w_pool_bd, pool_scale, w_out, g_post, mod)


def _ffn_kernel(x_ref, mod_ref, gpre_ref, gpost_ref, wg_ref, wu_ref, wd_ref, o_ref, *, chunk):
    x = x_ref[...]
    h = _rms(x, gpre_ref[...])
    h = (h * (1.0 + mod_ref[4:5, :]) + mod_ref[3:4, :]).astype(BF16)
    f = None
    for c in range(wg_ref.shape[1] // chunk):
        cols = slice(c * chunk, (c + 1) * chunk)
        g = jnp.dot(h, wg_ref[:, cols], preferred_element_type=F32)
        u = jnp.dot(h, wu_ref[:, cols], preferred_element_type=F32)
        act = ((g * jax.nn.sigmoid(g)) * u).astype(BF16)
        part = jnp.dot(act, wd_ref[cols, :], preferred_element_type=F32)
        f = part if f is None else f + part
    o_ref[...] = x + mod_ref[5:6, :] * _rms(f, gpost_ref[...])


def _ffn(x, mod, g_pre, g_post, w_gate, w_up, w_down, tm, chunk):
    b, n, d = x.shape
    dff = w_gate.shape[1]
    row = lambda bi, i: (0, 0)
    tile = lambda bi, i: (bi, i, 0)
    return pl.pallas_call(
        functools.partial(_ffn_kernel, chunk=chunk),
        out_shape=jax.ShapeDtypeStruct((b, n, d), F32),
        grid=(b, n // tm),
        in_specs=[
            pl.BlockSpec((None, tm, d), tile),
            pl.BlockSpec((None, N_MOD, d), lambda bi, i: (bi, 0, 0)),
            pl.BlockSpec((1, d), row),
            pl.BlockSpec((1, d), row),
            _resident((d, dff), row),
            _resident((d, dff), row),
            _resident((dff, d), row),
        ],
        out_specs=pl.BlockSpec((None, tm, d), tile),
        compiler_params=_cparams("arbitrary", "arbitrary"),
        name="ffn",
    )(x, mod, g_pre, g_post, w_gate, w_up, w_down)


def _rope_tables(n):
    rows = n // GRID_W
    row = jnp.broadcast_to(jnp.arange(rows)[:, None], (rows, GRID_W)).reshape(-1).astype(F32)
    col = jnp.broadcast_to(jnp.arange(GRID_W)[None, :], (rows, GRID_W)).reshape(-1).astype(F32)
    freq = ROPE_THETA ** (-jnp.arange(ROPE_FREQS, dtype=F32) / ROPE_FREQS)
    ar, ac = row[:, None] * freq, col[:, None] * freq
    cos = jnp.concatenate([jnp.cos(ar), jnp.cos(ar), jnp.cos(ac), jnp.cos(ac)], axis=1)
    sin = jnp.concatenate([-jnp.sin(ar), jnp.sin(ar), -jnp.sin(ac), jnp.sin(ac)], axis=1)
    return jnp.tile(cos, (1, LANES // HEAD_DIM)), jnp.tile(sin, (1, LANES // HEAD_DIM))


def _tile(n, pref):
    t = min(n, pref)
    assert n % t == 0, (n, t)
    return t


def kernel(x, c, ctx, c_ctx, w_mod, b_mod, g_pre_mix, g_post_mix, g_pre_ffn, g_post_ffn, w_in, w_pool, pool_scale, win_sink, g_qnorm, g_knorm, w_out, w_gate, w_up, w_down):
    b, n, d = x.shape
    lc = ctx.shape[1]
    depth = w_mod.shape[0]
    assert d == 1024 and w_in.shape[2] == 1536 and n % GRID_W == 0

    cos, sin = _rope_tables(n)
    cos_c, sin_c = jnp.ones((lc, LANES), F32), jnp.zeros((lc, LANES), F32)

    mod_rows = -(-(b + 1) // 8) * 8
    cc = jnp.concatenate([c, c_ctx[None, :], jnp.zeros((mod_rows - b - 1, d), F32)], axis=0)
    mod_all = _modulation(cc, w_mod, b_mod)

    tm = _tile(n, 512)
    tq_win = _tile(n, 256)
    tq_glb = _tile(n, 256)
    tk_glb = _tile(n, 1024)
    rb_glb = 256
    ffn_chunk = 256

    xc = ctx
    for l in range(depth):
        last = l == depth - 1
        m = mod_all[l, :b].reshape(b, N_MOD, d)
        mc = jnp.broadcast_to(mod_all[l, b].reshape(1, N_MOD, d), (b, N_MOD, d))
        w_in_l = w_in[l].astype(BF16)
        w_out_l = w_out[l].astype(BF16)
        w_gate_l, w_up_l, w_down_l = w_gate[l].astype(BF16), w_up[l].astype(BF16), w_down[l].astype(BF16)
        w_pool_bd = jax.scipy.linalg.block_diag(*[w_pool[l, g] for g in range(len(POOL_WINDOWS))]).astype(BF16)
        pscale = pool_scale[l][None, :]
        gq = jnp.tile(g_qnorm[l], LANES // HEAD_DIM)[None, :]
        gk = jnp.tile(g_knorm[l], LANES // HEAD_DIM)[None, :]
        sink = jnp.zeros((8, LANES), F32).at[:N_Q_HEADS].set(jnp.broadcast_to(win_sink[l][:, None], (N_Q_HEADS, LANES)))
        g_pre, g_post = g_pre_mix[l][None, :], g_post_mix[l][None, :]
        g_pre_f, g_post_f = g_pre_ffn[l][None, :], g_post_ffn[l][None, :]

        a, qw, ktw, vw, qg, ktg, vg = _in_proj(x, m, g_pre, w_in_l, cos, sin, gq, gk, tm)
        ac, qwc, ktwc, vwc, qgc, ktgc, vgc = _in_proj(xc, mc, g_pre, w_in_l, cos_c, sin_c, gq, gk, lc)

        o_win = _local_attn(qw, ktw, vw, (ktwc, vwc), sink, True, tq_win)
        bound = ATTN_SCALE * HEAD_DIM * jnp.max(jnp.abs(g_qnorm[l])) * jnp.max(jnp.abs(g_knorm[l]))
        glb = functools.partial(_global_attn, tq=tq_glb, tk=tk_glb, rb=rb_glb)
        o_glb = lax.cond(bound <= SAFE_SHIFT,
                         functools.partial(glb, exact_max=False), functools.partial(glb, exact_max=True),
                         jnp.full((1, LANES), bound, F32), qg, ktg, vg, ktgc, vgc)
        x_new = _out_proj(x, a, o_win, o_glb, w_pool_bd, pscale, w_out_l, g_post, m, tm)

        if not last:
            oc_win = _local_attn(qwc, ktwc, vwc, None, sink, False, lc)
            oc_glb = _local_attn(qgc, ktgc, vgc, None, None, False, lc)
            xc = _out_proj(xc, ac, oc_win, oc_glb, w_pool_bd, pscale, w_out_l, g_post, mc, lc)
            xc = _ffn(xc, mc, g_pre_f, g_post_f, w_gate_l, w_up_l, w_down_l, lc, ffn_chunk)

        x = _ffn(x_new, m, g_pre_f, g_post_f, w_gate_l, w_up_l, w_down_l, tm, ffn_chunk)
    return x
```

```python
import functools

import jax
import jax.numpy as jnp
from jax import lax
from jax.experimental import pallas as pl
from jax.experimental.pallas import tpu as pltpu

F32 = jnp.float32
BF16 = jnp.bfloat16

HEAD_DIM = 64
GRID_W = 64
ROPE_FREQS = HEAD_DIM // 4
ROPE_THETA = 10000.0
NORM_EPS = 1e-6
NEG_INF = -1e30
ATTN_SCALE = HEAD_DIM ** -0.5
WINDOW = 128
POOL_WINDOWS = (2, 4, 8, 16)
POOL_HALO = 16
N_Q_HEADS = 6
N_KV_HEADS = 2
KV_GROUP = N_Q_HEADS // N_KV_HEADS
N_MOD = 6
N_SHIFT = 3
SAFE_SHIFT = 30.0

LANES = 128
VMEM_LIMIT = 56 * 1024 * 1024


def _cparams(*sem):
    return pltpu.CompilerParams(dimension_semantics=sem, vmem_limit_bytes=VMEM_LIMIT)


def _resident(block_shape, index_map):
    return pl.BlockSpec(block_shape, index_map, pipeline_mode=pl.Buffered(1))


def _rms(x, g):
    ms = jnp.mean(x * x, axis=-1, keepdims=True)
    return (x * lax.rsqrt(ms + NORM_EPS)) * g


def _mod_kernel(c_ref, w_ref, b_ref, o_ref):
    c = c_ref[...]
    s = c * jax.nn.sigmoid(c)
    o_ref[...] = jnp.dot(s.astype(BF16), w_ref[...].astype(BF16), preferred_element_type=F32) + b_ref[...]


def _modulation(cc, w_mod, b_mod):
    depth, d, dm = w_mod.shape
    rows = cc.shape[0]
    tn = 1536
    return pl.pallas_call(
        _mod_kernel,
        out_shape=jax.ShapeDtypeStruct((depth, rows, dm), F32),
        grid=(depth, dm // tn),
        in_specs=[
            pl.BlockSpec((rows, d), lambda l, j: (0, 0)),
            pl.BlockSpec((None, d, tn), lambda l, j: (l, 0, j)),
            pl.BlockSpec((None, 1, tn), lambda l, j: (l, 0, j)),
        ],
        out_specs=pl.BlockSpec((None, rows, tn), lambda l, j: (l, 0, j)),
        compiler_params=_cparams("arbitrary", "arbitrary"),
        name="modulation",
    )(cc, w_mod, b_mod.reshape(depth, 1, dm))


def _in_proj_kernel(x_ref, mod_ref, g_ref, w_ref, cos_ref, sin_ref, gq_ref, gk_ref,
                    a_ref, qw_ref, ktw_ref, vw_ref, qg_ref, ktg_ref, vg_ref, *, st):
    tm = x_ref.shape[0]
    lane = lax.broadcasted_iota(jnp.int32, (st, LANES), 1)
    lo = lane < HEAD_DIM
    first = (lane & ROPE_FREQS) == 0
    shift_rows = (lax.broadcasted_iota(jnp.int32, (HEAD_DIM, LANES), 0) < N_SHIFT).astype(BF16)
    sum_lane = (lane == HEAD_DIM).astype(F32)
    gq, gk = gq_ref[...], gk_ref[...]

    def head_norm(t, g):
        sq = t * t
        s_lo = jnp.sum(jnp.where(lo, sq, 0.0), axis=-1, keepdims=True)
        s_hi = jnp.sum(jnp.where(lo, 0.0, sq), axis=-1, keepdims=True)
        ms = jnp.where(lo, s_lo, s_hi) * (1.0 / HEAD_DIM)
        return (t * lax.rsqrt(ms + NORM_EPS)) * g

    us = []
    for sub in range(tm // st):
        h = _rms(x_ref[sub * st:(sub + 1) * st, :], g_ref[...])
        h = h * (1.0 + mod_ref[1:2, :]) + mod_ref[0:1, :]
        us.append(jnp.dot(h.astype(BF16), w_ref[...], preferred_element_type=F32))

    for sub, u in enumerate(us):
        rs = slice(sub * st, (sub + 1) * st)
        cos = cos_ref[rs, :]
        sin = sin_ref[rs, :]

        def rope(t):
            partner = jnp.where(first, pltpu.roll(t, LANES - ROPE_FREQS, 1), pltpu.roll(t, ROPE_FREQS, 1))
            return t * cos + partner * sin

        def store_branch(q_chunks, k, v, q_ref, kt_ref, v_ref):
            for hd in range(N_Q_HEADS):
                t = q_chunks[hd // 2]
                if hd % 2:
                    t = pltpu.roll(t, HEAD_DIM, 1)
                q_ref[hd, rs, :] = jnp.where(lo, t, 0.0).astype(BF16)
            for r in range(st // LANES):
                kb = sub * (st // LANES) + r
                kt = jnp.transpose(k[r * LANES:(r + 1) * LANES, :]).astype(BF16)
                for kv in range(N_KV_HEADS):
                    kt_ref[kv, kb, 0:HEAD_DIM, :] = kt[kv * HEAD_DIM:(kv + 1) * HEAD_DIM, :]
                    kt_ref[kv, kb, HEAD_DIM:LANES, :] = shift_rows
            v_ref[0, rs, :] = jnp.where(lo, v, sum_lane).astype(BF16)
            v_ref[1, rs, :] = jnp.where(lo, pltpu.roll(v, HEAD_DIM, 1), sum_lane).astype(BF16)

        a_ref[rs, :] = u[:, 0:256]
        store_branch([rope(u[:, 256 + LANES * c:384 + LANES * c]) * ATTN_SCALE for c in range(3)],
                     rope(u[:, 640:768]), u[:, 768:896], qw_ref, ktw_ref, vw_ref)
        store_branch([rope(head_norm(u[:, 896 + LANES * c:1024 + LANES * c], gq)) * ATTN_SCALE for c in range(3)],
                     rope(head_norm(u[:, 1280:1408], gk)), u[:, 1408:1536], qg_ref, ktg_ref, vg_ref)


def _in_proj(x, mod, g_pre, w_in, cos, sin, gq, gk, tm):
    b, n, d = x.shape
    nw = w_in.shape[1]
    qshape = jax.ShapeDtypeStruct((b, N_Q_HEADS, n, LANES), BF16)
    ktshape = jax.ShapeDtypeStruct((b, N_KV_HEADS, n // LANES, LANES, LANES), BF16)
    vshape = jax.ShapeDtypeStruct((b, N_KV_HEADS, n, LANES), BF16)
    qspec = pl.BlockSpec((None, N_Q_HEADS, tm, LANES), lambda bi, i: (bi, 0, i, 0))
    ktspec = pl.BlockSpec((None, N_KV_HEADS, tm // LANES, LANES, LANES), lambda bi, i: (bi, 0, i, 0, 0))
    vspec = pl.BlockSpec((None, N_KV_HEADS, tm, LANES), lambda bi, i: (bi, 0, i, 0))
    row = lambda bi, i: (0, 0)
    return pl.pallas_call(
        functools.partial(_in_proj_kernel, st=min(tm, 256)),
        out_shape=(jax.ShapeDtypeStruct((b, n, 256), F32), qshape, ktshape, vshape, qshape, ktshape, vshape),
        grid=(b, n // tm),
        in_specs=[
            pl.BlockSpec((None, tm, d), lambda bi, i: (bi, i, 0)),
            pl.BlockSpec((None, N_MOD, d), lambda bi, i: (bi, 0, 0)),
            pl.BlockSpec((1, d), row),
            _resident((d, nw), row),
            pl.BlockSpec((tm, LANES), lambda bi, i: (i, 0)),
            pl.BlockSpec((tm, LANES), lambda bi, i: (i, 0)),
            pl.BlockSpec((1, LANES), row),
            pl.BlockSpec((1, LANES), row),
        ],
        out_specs=(pl.BlockSpec((None, tm, 256), lambda bi, i: (bi, i, 0)),
                   qspec, ktspec, vspec, qspec, ktspec, vspec),
        compiler_params=_cparams("arbitrary", "arbitrary"),
        name="in_proj",
    )(x, mod, g_pre, w_in, cos, sin, gq, gk)


def _merge_heads(outs, lo):
    return [jnp.where(lo, outs[2 * c], pltpu.roll(outs[2 * c + 1], HEAD_DIM, 1)) for c in range(N_Q_HEADS // 2)]


def _key_tile(kt_ref, first_block, n_blocks):
    return jnp.concatenate([kt_ref[first_block + j] for j in range(n_blocks)], axis=1)


def _stacked_sink(sink_ref, kv, rows_per_head):
    return jnp.concatenate([jnp.broadcast_to(sink_ref[hd:hd + 1, 0:1], (rows_per_head, 1))
                            for hd in range(kv * KV_GROUP, (kv + 1) * KV_GROUP)], axis=0)


def _window_attn_kernel(q_ref, kt_ref, v_ref, ktc_ref, vc_ref, sink_ref, o_ref):
    tq = q_ref.shape[1]
    nkb = kt_ref.shape[1]
    sub = WINDOW
    rows = KV_GROUP * sub
    first_blk = pl.program_id(1) * (tq // sub)
    r_in = lax.broadcasted_iota(jnp.int32, (rows, LANES), 0) & (sub - 1)
    c_in = lax.broadcasted_iota(jnp.int32, (rows, LANES), 1)
    lo = lax.broadcasted_iota(jnp.int32, (sub, LANES), 1) < HEAD_DIM

    ctx = [(_key_tile(ktc_ref.at[kv], 0, ktc_ref.shape[1]), vc_ref[kv], _stacked_sink(sink_ref, kv, sub))
           for kv in range(N_KV_HEADS)]

    def logits(j, kv):
        blk = first_blk + j
        prev_blk, next_blk = jnp.maximum(blk - 1, 0), jnp.minimum(blk + 1, nkb - 1)
        q = q_ref[kv * KV_GROUP:(kv + 1) * KV_GROUP, j * sub:(j + 1) * sub, :].reshape(rows, LANES)
        kt = jnp.concatenate([kt_ref[kv, prev_blk], kt_ref[kv, blk], kt_ref[kv, next_blk]], axis=1)
        s = jnp.dot(q, kt, preferred_element_type=F32)
        sc = jnp.dot(q, ctx[kv][0], preferred_element_type=F32)
        return s, sc

    def attend(j, kv, s, sc):
        blk = first_blk + j
        prev_blk, next_blk = jnp.maximum(blk - 1, 0), jnp.minimum(blk + 1, nkb - 1)
        _, vc, sk = ctx[kv]
        v = jnp.concatenate([v_ref[kv, pl.ds(pl.multiple_of(kb * sub, sub), sub), :]
                             for kb in (prev_blk, blk, next_blk)], axis=0)
        s_prev = jnp.where(jnp.logical_and(c_in >= r_in, blk > 0), s[:, 0:sub], NEG_INF)
        s_own = s[:, sub:2 * sub]
        s_next = jnp.where(jnp.logical_and(c_in <= r_in, blk < nkb - 1), s[:, 2 * sub:3 * sub], NEG_INF)
        mx = jnp.maximum(jnp.maximum(s_prev, s_own), s_next)
        for t in range(sc.shape[1] // LANES):
            mx = jnp.maximum(mx, sc[:, t * LANES:(t + 1) * LANES])
        m = jnp.maximum(jnp.max(mx, axis=-1, keepdims=True), sk)
        p = jnp.concatenate([jnp.exp(s_prev - m), jnp.exp(s_own - m), jnp.exp(s_next - m)], axis=1)
        pv = jnp.dot(p.astype(BF16), v, preferred_element_type=F32)
        pv = pv + jnp.dot(jnp.exp(sc - m).astype(BF16), vc, preferred_element_type=F32)
        l = pv[:, HEAD_DIM:HEAD_DIM + 1] + jnp.exp(sk - m)
        o = pv / l
        return [o[g * sub:(g + 1) * sub] for g in range(KV_GROUP)]

    units = [(j, kv) for j in range(tq // sub) for kv in range(N_KV_HEADS)]
    pending = logits(*units[0])
    outs = []
    for idx, (j, kv) in enumerate(units):
        ahead = logits(*units[idx + 1]) if idx + 1 < len(units) else None
        outs += attend(j, kv, *pending)
        pending = ahead
        if kv == N_KV_HEADS - 1:
            for c, chunk in enumerate(_merge_heads(outs, lo)):
                o_ref[j * sub:(j + 1) * sub, c * LANES:(c + 1) * LANES] = chunk.astype(BF16)
            outs = []


def _window_attn(q, kt, v, ktc, vc, sink, tq):
    b, _, n, _ = q.shape
    lc = vc.shape[2]
    assert tq % WINDOW == 0 and WINDOW == LANES
    batch4 = lambda bi, i: (bi, 0, 0, 0)
    batch5 = lambda bi, i: (bi, 0, 0, 0, 0)
    return pl.pallas_call(
        _window_attn_kernel,
        out_shape=jax.ShapeDtypeStruct((b, n, N_Q_HEADS * HEAD_DIM), BF16),
        grid=(b, n // tq),
        in_specs=[
            pl.BlockSpec((None, N_Q_HEADS, tq, LANES), lambda bi, i: (bi, 0, i, 0)),
            pl.BlockSpec((None, N_KV_HEADS, n // LANES, LANES, LANES), batch5),
            pl.BlockSpec((None, N_KV_HEADS, n, LANES), batch4),
            pl.BlockSpec((None, N_KV_HEADS, lc // LANES, LANES, LANES), batch5),
            pl.BlockSpec((None, N_KV_HEADS, lc, LANES), batch4),
            pl.BlockSpec(sink.shape, lambda bi, i: (0, 0)),
        ],
        out_specs=pl.BlockSpec((None, tq, N_Q_HEADS * HEAD_DIM), lambda bi, i: (bi, i, 0)),
        compiler_params=_cparams("arbitrary", "arbitrary"),
        name="window_attn",
    )(q, kt, v, ktc, vc, sink)


def _ctx_attn_kernel(*refs, has_sink):
    if has_sink:
        q_ref, kt_ref, v_ref, sink_ref, o_ref = refs
    else:
        q_ref, kt_ref, v_ref, o_ref = refs
    tq = q_ref.shape[1]
    rows = KV_GROUP * tq
    lo = lax.broadcasted_iota(jnp.int32, (tq, LANES), 1) < HEAD_DIM
    outs = []
    for kv in range(N_KV_HEADS):
        q = q_ref[kv * KV_GROUP:(kv + 1) * KV_GROUP].reshape(rows, LANES)
        s = jnp.dot(q, _key_tile(kt_ref.at[kv], 0, kt_ref.shape[1]), preferred_element_type=F32)
        m = jnp.max(s, axis=-1, keepdims=True)
        if has_sink:
            sk = _stacked_sink(sink_ref, kv, tq)
            m = jnp.maximum(m, sk)
        pv = jnp.dot(jnp.exp(s - m).astype(BF16), v_ref[kv], preferred_element_type=F32)
        l = pv[:, HEAD_DIM:HEAD_DIM + 1]
        if has_sink:
            l = l + jnp.exp(sk - m)
        o = pv / l
        outs += [o[g * tq:(g + 1) * tq] for g in range(KV_GROUP)]
    for c, chunk in enumerate(_merge_heads(outs, lo)):
        o_ref[:, c * LANES:(c + 1) * LANES] = chunk.astype(BF16)


def _ctx_attn(q, kt, v, sink):
    b, _, n, _ = q.shape
    args = [q, kt, v]
    in_specs = [pl.BlockSpec((None,) + a.shape[1:], lambda bi, nd=a.ndim: (bi,) + (0,) * (nd - 1)) for a in args]
    if sink is not None:
        args.append(sink)
        in_specs.append(pl.BlockSpec(sink.shape, lambda bi: (0, 0)))
    return pl.pallas_call(
        functools.partial(_ctx_attn_kernel, has_sink=sink is not None),
        out_shape=jax.ShapeDtypeStruct((b, n, N_Q_HEADS * HEAD_DIM), BF16),
        grid=(b,),
        in_specs=in_specs,
        out_specs=pl.BlockSpec((None, n, N_Q_HEADS * HEAD_DIM), lambda bi: (bi, 0, 0)),
        compiler_params=_cparams("arbitrary"),
        name="ctx_attn",
    )(*args)


def _global_attn_kernel(*refs, tk, rb, exact_max):
    if exact_max:
        q_ref, kt_ref, v_ref, ktc_ref, vc_ref, o_ref, qs_ref, acc_ref, mx_ref = refs
    else:
        bound_ref, q_ref, kt_ref, v_ref, ktc_ref, vc_ref, o_ref, qs_ref, acc_ref = refs
    tq = q_ref.shape[1]
    rows = KV_GROUP * tq
    kb_per_step = tk // LANES
    n_steps = kt_ref.shape[1] // kb_per_step
    lane = lax.broadcasted_iota(jnp.int32, (rows, LANES), 1)

    def key_steps(fn):
        def step(kc, carry):
            for kv in range(N_KV_HEADS):
                fn(kv, _key_tile(kt_ref.at[kv], kc * kb_per_step, kb_per_step),
                   v_ref[kv, pl.ds(pl.multiple_of(kc * tk, tk), tk), :])
            return carry
        lax.fori_loop(0, n_steps, step, 0)
        for kv in range(N_KV_HEADS):
            fn(kv, _key_tile(ktc_ref.at[kv], 0, ktc_ref.shape[1]), vc_ref[kv])

    def q_rows(kv):
        return q_ref[kv * KV_GROUP:(kv + 1) * KV_GROUP].reshape(rows, LANES)

    if exact_max:
        mx_ref[...] = jnp.full(mx_ref.shape, NEG_INF, F32)

        def track_max(kv, kt, v):
            s = jnp.dot(q_rows(kv), kt, preferred_element_type=F32)
            mx = mx_ref[kv]
            for j in range(kt.shape[1] // LANES):
                mx = jnp.maximum(mx, s[:, j * LANES:(j + 1) * LANES])
            mx_ref[kv] = mx

        key_steps(track_max)

    for kv in range(N_KV_HEADS):
        if exact_max:
            m = jnp.max(mx_ref[kv], axis=-1, keepdims=True)
        else:
            m = jnp.broadcast_to(bound_ref[0:1, 0:1], (rows, 1))
        qs = q_rows(kv).astype(F32)
        rest = m
        for piece in range(N_SHIFT):
            part = rest.astype(BF16).astype(F32)
            qs = jnp.where(lane == HEAD_DIM + piece, -part, qs)
            rest = rest - part
        qs_ref[kv] = qs.astype(BF16)

    acc_ref[...] = jnp.zeros(acc_ref.shape, F32)

    def accumulate(kv, kt, v):
        for r in range(rows // rb):
            rsl = slice(r * rb, (r + 1) * rb)
            s = jnp.dot(qs_ref[kv, rsl, :], kt, preferred_element_type=F32)
            p = jnp.exp(s).astype(BF16)
            acc_ref[kv, rsl, :] += jnp.dot(p, v, preferred_element_type=F32)

    key_steps(accumulate)

    outs = []
    for hd in range(N_Q_HEADS):
        a = acc_ref[hd // KV_GROUP, (hd % KV_GROUP) * tq:(hd % KV_GROUP + 1) * tq, :]
        outs.append(a / a[:, HEAD_DIM:HEAD_DIM + 1])
    lo = lax.broadcasted_iota(jnp.int32, (tq, LANES), 1) < HEAD_DIM
    for c, chunk in enumerate(_merge_heads(outs, lo)):
        o_ref[:, c * LANES:(c + 1) * LANES] = chunk.astype(BF16)


def _global_attn(bound, q, kt, v, ktc, vc, *, tq, tk, rb, exact_max):
    b, _, n, _ = q.shape
    lc = vc.shape[2]
    rows = KV_GROUP * tq
    batch4 = lambda bi, i: (bi, 0, 0, 0)
    batch5 = lambda bi, i: (bi, 0, 0, 0, 0)
    args = [q, kt, v, ktc, vc]
    in_specs = [
        pl.BlockSpec((None, N_Q_HEADS, tq, LANES), lambda bi, i: (bi, 0, i, 0)),
        pl.BlockSpec((None, N_KV_HEADS, n // LANES, LANES, LANES), batch5),
        pl.BlockSpec((None, N_KV_HEADS, n, LANES), batch4),
        pl.BlockSpec((None, N_KV_HEADS, lc // LANES, LANES, LANES), batch5),
        pl.BlockSpec((None, N_KV_HEADS, lc, LANES), batch4),
    ]
    scratch = [pltpu.VMEM((N_KV_HEADS, rows, LANES), BF16), pltpu.VMEM((N_KV_HEADS, rows, LANES), F32)]
    if exact_max:
        scratch.append(pltpu.VMEM((N_KV_HEADS, rows, LANES), F32))
    else:
        args.insert(0, bound)
        in_specs.insert(0, pl.BlockSpec(bound.shape, lambda bi, i: (0, 0)))
    return pl.pallas_call(
        functools.partial(_global_attn_kernel, tk=tk, rb=rb, exact_max=exact_max),
        out_shape=jax.ShapeDtypeStruct((b, n, N_Q_HEADS * HEAD_DIM), BF16),
        grid=(b, n // tq),
        in_specs=in_specs,
        out_specs=pl.BlockSpec((None, tq, N_Q_HEADS * HEAD_DIM), lambda bi, i: (bi, i, 0)),
        scratch_shapes=scratch,
        compiler_params=_cparams("arbitrary", "arbitrary"),
        name="global_attn_exact_max" if exact_max else "global_attn",
    )(*args)


def _out_proj_kernel(x_ref, a_ref, ap_ref, an_ref, ow_ref, og_ref, wpool_ref, pscale_ref, wo_ref, g_ref, mod_ref,
                     o_ref, ext_ref, *, n):
    i = pl.program_id(1)
    tm = x_ref.shape[0]
    ext_ref[0:POOL_HALO, :] = jnp.where(i > 0, ap_ref[...], 0.0)
    ext_ref[POOL_HALO:POOL_HALO + tm, :] = a_ref[...]
    ext_ref[POOL_HALO + tm:2 * POOL_HALO + tm, :] = jnp.where(i < pl.num_programs(1) - 1, an_ref[...], 0.0)

    pw = a_ref.shape[1]
    aw = ow_ref.shape[1]
    y = jnp.dot(ow_ref[...], wo_ref[pw:pw + aw, :], preferred_element_type=F32)
    y = y + jnp.dot(og_ref[...], wo_ref[pw + aw:pw + 2 * aw, :], preferred_element_type=F32)

    t = i * tm + lax.broadcasted_iota(jnp.int32, (tm, 1), 0)
    lo = lax.broadcasted_iota(jnp.int32, (tm, LANES), 1) < HEAD_DIM

    def count(w):
        hi_t = jnp.minimum(t + (w - 1 - w // 2), n - 1)
        lo_t = jnp.maximum(t - w // 2, 0)
        return (hi_t - lo_t + 1).astype(F32)

    feats = []
    for c in range(2):
        w_a, w_b = POOL_WINDOWS[2 * c], POOL_WINDOWS[2 * c + 1]

        def shifted(k, c=c):
            return ext_ref[pl.ds(POOL_HALO + k, tm), c * LANES:(c + 1) * LANES]

        x0 = shifted(0)
        acc = x0
        sums = {}
        reach = 0
        for w in (w_a, w_b):
            while reach < w // 2:
                reach += 1
                acc = acc + shifted(-reach)
                if reach - 1 >= 1:
                    acc = acc + shifted(reach - 1)
            sums[w] = acc
        mean = jnp.where(lo, sums[w_a] / count(w_a), sums[w_b] / count(w_b))
        feats.append(mean - x0)
    feats = jnp.concatenate(feats, axis=1).astype(BF16)
    y_pool = jnp.dot(feats, wpool_ref[...], preferred_element_type=F32) * pscale_ref[...]

    y = y + jnp.dot(y_pool.astype(BF16), wo_ref[0:pw, :], preferred_element_type=F32)
    o_ref[...] = x_ref[...] + mod_ref[2:3, :] * _rms(y, g_ref[...])


def _out_proj(x, a, o_win, o_glb, w_pool_bd, pool_scale, w_out, g_post, mod, tm):
    b, n, d = x.shape
    pw = a.shape[2]
    aw = o_win.shape[2]
    hb = tm // POOL_HALO
    row = lambda bi, i: (0, 0)
    tile = lambda bi, i: (bi, i, 0)
    return pl.pallas_call(
        functools.partial(_out_proj_kernel, n=n),
        out_shape=jax.ShapeDtypeStruct((b, n, d), F32),
        grid=(b, n // tm),
        in_specs=[
            pl.BlockSpec((None, tm, d), tile),
            pl.BlockSpec((None, tm, pw), tile),
            pl.BlockSpec((None, POOL_HALO, pw), lambda bi, i: (bi, jnp.maximum(i * hb - 1, 0), 0)),
            pl.BlockSpec((None, POOL_HALO, pw), lambda bi, i: (bi, jnp.minimum((i + 1) * hb, n // POOL_HALO - 1), 0)),
            pl.BlockSpec((None, tm, aw), tile),
            pl.BlockSpec((None, tm, aw), tile),
            _resident((pw, pw), row),
            pl.BlockSpec((1, pw), row),
            _resident((d, d), row),
            pl.BlockSpec((1, d), row),
            pl.BlockSpec((None, N_MOD, d), lambda bi, i: (bi, 0, 0)),
        ],
        out_specs=pl.BlockSpec((None, tm, d), tile),
        scratch_shapes=[pltpu.VMEM((tm + 2 * POOL_HALO, pw), F32)],
        compiler_params=_cparams("arbitrary", "arbitrary"),
        name="out_proj",
    )(x, a, a, a, o_win, o_glb, w_pool_bd, pool_scale, w_out, g_post, mod)


def _ffn_kernel(x_ref, mod_ref, gpre_ref, gpost_ref, wg_ref, wu_ref, wd_ref, o_ref, *, chunk):
    x = x_ref[...]
    h = _rms(x, gpre_ref[...])
    h = (h * (1.0 + mod_ref[4:5, :]) + mod_ref[3:4, :]).astype(BF16)
    f = None
    for c in range(wg_ref.shape[1] // chunk):
        cols = slice(c * chunk, (c + 1) * chunk)
        g = jnp.dot(h, wg_ref[:, cols], preferred_element_type=F32)
        u = jnp.dot(h, wu_ref[:, cols], preferred_element_type=F32)
        act = ((g * jax.nn.sigmoid(g)) * u).astype(BF16)
        part = jnp.dot(act, wd_ref[cols, :], preferred_element_type=F32)
        f = part if f is None else f + part
    o_ref[...] = x + mod_ref[5:6, :] * _rms(f, gpost_ref[...])


def _ffn(x, mod, g_pre, g_post, w_gate, w_up, w_down, tm, chunk):
    b, n, d = x.shape
    dff = w_gate.shape[1]
    row = lambda bi, i: (0, 0)
    tile = lambda bi, i: (bi, i, 0)
    return pl.pallas_call(
        functools.partial(_ffn_kernel, chunk=chunk),
        out_shape=jax.ShapeDtypeStruct((b, n, d), F32),
        grid=(b, n // tm),
        in_specs=[
            pl.BlockSpec((None, tm, d), tile),
            pl.BlockSpec((None, N_MOD, d), lambda bi, i: (bi, 0, 0)),
            pl.BlockSpec((1, d), row),
            pl.BlockSpec((1, d), row),
            _resident((d, dff), row),
            _resident((d, dff), row),
            _resident((dff, d), row),
        ],
        out_specs=pl.BlockSpec((None, tm, d), tile),
        compiler_params=_cparams("arbitrary", "arbitrary"),
        name="ffn",
    )(x, mod, g_pre, g_post, w_gate, w_up, w_down)


def _rope_tables(n):
    rows = n // GRID_W
    row = jnp.broadcast_to(jnp.arange(rows)[:, None], (rows, GRID_W)).reshape(-1).astype(F32)
    col = jnp.broadcast_to(jnp.arange(GRID_W)[None, :], (rows, GRID_W)).reshape(-1).astype(F32)
    freq = ROPE_THETA ** (-jnp.arange(ROPE_FREQS, dtype=F32) / ROPE_FREQS)
    ar, ac = row[:, None] * freq, col[:, None] * freq
    cos = jnp.concatenate([jnp.cos(ar), jnp.cos(ar), jnp.cos(ac), jnp.cos(ac)], axis=1)
    sin = jnp.concatenate([-jnp.sin(ar), jnp.sin(ar), -jnp.sin(ac), jnp.sin(ac)], axis=1)
    return jnp.tile(cos, (1, LANES // HEAD_DIM)), jnp.tile(sin, (1, LANES // HEAD_DIM))


def _tile(n, pref):
    t = min(n, pref)
    assert n % t == 0, (n, t)
    return t


def kernel(x, c, ctx, c_ctx, w_mod, b_mod, g_pre_mix, g_post_mix, g_pre_ffn, g_post_ffn, w_in, w_pool, pool_scale, win_sink, g_qnorm, g_knorm, w_out, w_gate, w_up, w_down):
    b, n, d = x.shape
    lc = ctx.shape[1]
    depth = w_mod.shape[0]
    assert d == 1024 and w_in.shape[2] == 1536 and n % GRID_W == 0

    cos, sin = _rope_tables(n)
    cos_c, sin_c = jnp.ones((lc, LANES), F32), jnp.zeros((lc, LANES), F32)

    mod_rows = -(-(b + 1) // 8) * 8
    cc = jnp.concatenate([c, c_ctx[None, :], jnp.zeros((mod_rows - b - 1, d), F32)], axis=0)
    mod_all = _modulation(cc, w_mod, b_mod)

    tm = _tile(n, 512)
    tq_win = _tile(n, 512)
    tq_glb = _tile(n, 512)
    tk_glb = _tile(n, 1024)
    rb_glb = 256
    ffn_chunk = 256

    xc = ctx
    for l in range(depth):
        last = l == depth - 1
        m = mod_all[l, :b].reshape(b, N_MOD, d)
        mc = jnp.broadcast_to(mod_all[l, b].reshape(1, N_MOD, d), (b, N_MOD, d))
        w_in_l = w_in[l].astype(BF16)
        w_out_l = w_out[l].astype(BF16)
        w_gate_l, w_up_l, w_down_l = w_gate[l].astype(BF16), w_up[l].astype(BF16), w_down[l].astype(BF16)
        w_pool_bd = jax.scipy.linalg.block_diag(*[w_pool[l, g] for g in range(len(POOL_WINDOWS))]).astype(BF16)
        pscale = pool_scale[l][None, :]
        gq = jnp.tile(g_qnorm[l], LANES // HEAD_DIM)[None, :]
        gk = jnp.tile(g_knorm[l], LANES // HEAD_DIM)[None, :]
        sink = jnp.zeros((8, LANES), F32).at[:N_Q_HEADS].set(jnp.broadcast_to(win_sink[l][:, None], (N_Q_HEADS, LANES)))
        g_pre, g_post = g_pre_mix[l][None, :], g_post_mix[l][None, :]
        g_pre_f, g_post_f = g_pre_ffn[l][None, :], g_post_ffn[l][None, :]

        a, qw, ktw, vw, qg, ktg, vg = _in_proj(x, m, g_pre, w_in_l, cos, sin, gq, gk, tm)
        ac, qwc, ktwc, vwc, qgc, ktgc, vgc = _in_proj(xc, mc, g_pre, w_in_l, cos_c, sin_c, gq, gk, lc)

        o_win = _window_attn(qw, ktw, vw, ktwc, vwc, sink, tq_win)
        bound = ATTN_SCALE * HEAD_DIM * jnp.max(jnp.abs(g_qnorm[l])) * jnp.max(jnp.abs(g_knorm[l]))
        glb = functools.partial(_global_attn, tq=tq_glb, tk=tk_glb, rb=rb_glb)
        o_glb = lax.cond(bound <= SAFE_SHIFT,
                         functools.partial(glb, exact_max=False), functools.partial(glb, exact_max=True),
                         jnp.full((1, LANES), bound, F32), qg, ktg, vg, ktgc, vgc)
        x_new = _out_proj(x, a, o_win, o_glb, w_pool_bd, pscale, w_out_l, g_post, m, tm)

        if not last:
            oc_win = _ctx_attn(qwc, ktwc, vwc, sink)
            oc_glb = _ctx_attn(qgc, ktgc, vgc, None)
            xc = _out_proj(xc, ac, oc_win, oc_glb, w_pool_bd, pscale, w_out_l, g_post, mc, lc)
            xc = _ffn(xc, mc, g_pre_f, g_post_f, w_gate_l, w_up_l, w_down_l, lc, ffn_chunk)

        x = _ffn(x_new, m, g_pre_f, g_post_f, w_gate_l, w_up_l, w_down_l, tm, ffn_chunk)
    return x
```

```python
import functools

import jax
import jax.numpy as jnp
from jax import lax
from jax.experimental import pallas as pl
from jax.experimental.pallas import tpu as pltpu

F32 = jnp.float32
BF16 = jnp.bfloat16

HEAD_DIM = 64
GRID_W = 64
ROPE_FREQS = HEAD_DIM // 4
ROPE_THETA = 10000.0
NORM_EPS = 1e-6
NEG_INF = -1e30
ATTN_SCALE = HEAD_DIM ** -0.5
WINDOW = 128
POOL_WINDOWS = (2, 4, 8, 16)
POOL_HALO = 16
N_Q_HEADS = 6
N_KV_HEADS = 2
KV_GROUP = N_Q_HEADS // N_KV_HEADS
N_MOD = 6
N_SHIFT = 3
SAFE_SHIFT = 30.0

LANES = 128
VMEM_LIMIT = 56 * 1024 * 1024


def _cparams(*sem):
    return pltpu.CompilerParams(dimension_semantics=sem, vmem_limit_bytes=VMEM_LIMIT)


def _resident(block_shape, index_map):
    return pl.BlockSpec(block_shape, index_map, pipeline_mode=pl.Buffered(1))


def _rms(x, g):
    ms = jnp.mean(x * x, axis=-1, keepdims=True)
    return (x * lax.rsqrt(ms + NORM_EPS)) * g


def _mod_kernel(c_ref, w_ref, b_ref, o_ref):
    c = c_ref[...]
    s = c * jax.nn.sigmoid(c)
    o_ref[...] = jnp.dot(s.astype(BF16), w_ref[...].astype(BF16), preferred_element_type=F32) + b_ref[...]


def _modulation(cc, w_mod, b_mod):
    depth, d, dm = w_mod.shape
    rows = cc.shape[0]
    tn = 1536
    return pl.pallas_call(
        _mod_kernel,
        out_shape=jax.ShapeDtypeStruct((depth, rows, dm), F32),
        grid=(depth, dm // tn),
        in_specs=[
            pl.BlockSpec((rows, d), lambda l, j: (0, 0)),
            pl.BlockSpec((None, d, tn), lambda l, j: (l, 0, j)),
            pl.BlockSpec((None, 1, tn), lambda l, j: (l, 0, j)),
        ],
        out_specs=pl.BlockSpec((None, rows, tn), lambda l, j: (l, 0, j)),
        compiler_params=_cparams("arbitrary", "arbitrary"),
        name="modulation",
    )(cc, w_mod, b_mod.reshape(depth, 1, dm))


def _in_proj_epilogue(u_of, n_rows, st, cos_ref, sin_ref, gq_ref, gk_ref,
                      a_ref, qw_ref, ktw_ref, vw_ref, qg_ref, ktg_ref, vg_ref):
    lane = lax.broadcasted_iota(jnp.int32, (st, LANES), 1)
    lo = lane < HEAD_DIM
    first = (lane & ROPE_FREQS) == 0
    shift_rows = (lax.broadcasted_iota(jnp.int32, (HEAD_DIM, LANES), 0) < N_SHIFT).astype(BF16)
    sum_lane = (lane == HEAD_DIM).astype(F32)
    gq, gk = gq_ref[...], gk_ref[...]

    def head_norm(t, g):
        sq = t * t
        s_lo = jnp.sum(jnp.where(lo, sq, 0.0), axis=-1, keepdims=True)
        s_hi = jnp.sum(jnp.where(lo, 0.0, sq), axis=-1, keepdims=True)
        ms = jnp.where(lo, s_lo, s_hi) * (1.0 / HEAD_DIM)
        return (t * lax.rsqrt(ms + NORM_EPS)) * g

    for sub in range(n_rows // st):
        rs = slice(sub * st, (sub + 1) * st)
        u = u_of(sub)
        cos = cos_ref[rs, :]
        sin = sin_ref[rs, :]

        def rope(t):
            partner = jnp.where(first, pltpu.roll(t, LANES - ROPE_FREQS, 1), pltpu.roll(t, ROPE_FREQS, 1))
            return t * cos + partner * sin

        def store_branch(q_chunks, k, v, q_ref, kt_ref, v_ref):
            for hd in range(N_Q_HEADS):
                t = q_chunks[hd // 2]
                if hd % 2:
                    t = pltpu.roll(t, HEAD_DIM, 1)
                q_ref[hd, rs, :] = jnp.where(lo, t, 0.0).astype(BF16)
            for r in range(st // LANES):
                kb = sub * (st // LANES) + r
                kt = jnp.transpose(k[r * LANES:(r + 1) * LANES, :]).astype(BF16)
                for kv in range(N_KV_HEADS):
                    kt_ref[kv, kb, 0:HEAD_DIM, :] = kt[kv * HEAD_DIM:(kv + 1) * HEAD_DIM, :]
                    kt_ref[kv, kb, HEAD_DIM:LANES, :] = shift_rows
            v_ref[0, rs, :] = jnp.where(lo, v, sum_lane).astype(BF16)
            v_ref[1, rs, :] = jnp.where(lo, pltpu.roll(v, HEAD_DIM, 1), sum_lane).astype(BF16)

        a_ref[rs, :] = u[:, 0:256]
        store_branch([rope(u[:, 256 + LANES * c:384 + LANES * c]) * ATTN_SCALE for c in range(3)],
                     rope(u[:, 640:768]), u[:, 768:896], qw_ref, ktw_ref, vw_ref)
        store_branch([rope(head_norm(u[:, 896 + LANES * c:1024 + LANES * c], gq)) * ATTN_SCALE for c in range(3)],
                     rope(head_norm(u[:, 1280:1408], gk)), u[:, 1408:1536], qg_ref, ktg_ref, vg_ref)


def _project(x, mod_ref, g_ref, w_ref):
    h = _rms(x, g_ref[...])
    h = h * (1.0 + mod_ref[1:2, :]) + mod_ref[0:1, :]
    return jnp.dot(h.astype(BF16), w_ref[...], preferred_element_type=F32)


def _in_proj_kernel(x_ref, mod_ref, g_ref, w_ref, cos_ref, sin_ref, gq_ref, gk_ref, *out_refs, st):
    tm = x_ref.shape[0]
    us = [_project(x_ref[sub * st:(sub + 1) * st, :], mod_ref, g_ref, w_ref) for sub in range(tm // st)]
    _in_proj_epilogue(lambda sub: us[sub], tm, st, cos_ref, sin_ref, gq_ref, gk_ref, *out_refs)


def _in_proj_outputs(b, n, tm, tile_of):
    qshape = jax.ShapeDtypeStruct((b, N_Q_HEADS, n, LANES), BF16)
    ktshape = jax.ShapeDtypeStruct((b, N_KV_HEADS, n // LANES, LANES, LANES), BF16)
    vshape = jax.ShapeDtypeStruct((b, N_KV_HEADS, n, LANES), BF16)

    def spec(block, where):
        return pl.BlockSpec(block, lambda *g: where(*tile_of(*g)))

    aspec = spec((None, tm, 256), lambda bi, i: (bi, i, 0))
    qspec = spec((None, N_Q_HEADS, tm, LANES), lambda bi, i: (bi, 0, i, 0))
    ktspec = spec((None, N_KV_HEADS, tm // LANES, LANES, LANES), lambda bi, i: (bi, 0, i, 0, 0))
    vspec = spec((None, N_KV_HEADS, tm, LANES), lambda bi, i: (bi, 0, i, 0))
    shapes = (jax.ShapeDtypeStruct((b, n, 256), F32), qshape, ktshape, vshape, qshape, ktshape, vshape)
    return shapes, (aspec, qspec, ktspec, vspec, qspec, ktspec, vspec)


def _in_proj(x, mod, g_pre, w_in, cos, sin, gq, gk, tm):
    b, n, d = x.shape
    nw = w_in.shape[1]
    out_shape, out_specs = _in_proj_outputs(b, n, tm, lambda bi, i: (bi, i))
    row = lambda bi, i: (0, 0)
    return pl.pallas_call(
        functools.partial(_in_proj_kernel, st=min(tm, 256)),
        out_shape=out_shape,
        grid=(b, n // tm),
        in_specs=[
            pl.BlockSpec((None, tm, d), lambda bi, i: (bi, i, 0)),
            pl.BlockSpec((None, N_MOD, d), lambda bi, i: (bi, 0, 0)),
            pl.BlockSpec((1, d), row),
            _resident((d, nw), row),
            pl.BlockSpec((tm, LANES), lambda bi, i: (i, 0)),
            pl.BlockSpec((tm, LANES), lambda bi, i: (i, 0)),
            pl.BlockSpec((1, LANES), row),
            pl.BlockSpec((1, LANES), row),
        ],
        out_specs=out_specs,
        compiler_params=_cparams("arbitrary", "arbitrary"),
        name="in_proj",
    )(x, mod, g_pre, w_in, cos, sin, gq, gk)


def _merge_heads(outs, lo):
    return [jnp.where(lo, outs[2 * c], pltpu.roll(outs[2 * c + 1], HEAD_DIM, 1)) for c in range(N_Q_HEADS // 2)]


def _key_tile(kt_ref, first_block, n_blocks):
    return jnp.concatenate([kt_ref[first_block + j] for j in range(n_blocks)], axis=1)


def _stacked_sink(sink_ref, kv, rows_per_head):
    return jnp.concatenate([jnp.broadcast_to(sink_ref[hd:hd + 1, 0:1], (rows_per_head, 1))
                            for hd in range(kv * KV_GROUP, (kv + 1) * KV_GROUP)], axis=0)


def _window_attn_kernel(q_ref, kt_ref, v_ref, ktc_ref, vc_ref, sink_ref, o_ref):
    tq = q_ref.shape[1]
    nkb = kt_ref.shape[1]
    sub = WINDOW
    rows = KV_GROUP * sub
    first_blk = pl.program_id(1) * (tq // sub)
    r_in = lax.broadcasted_iota(jnp.int32, (rows, LANES), 0) & (sub - 1)
    c_in = lax.broadcasted_iota(jnp.int32, (rows, LANES), 1)
    lo = lax.broadcasted_iota(jnp.int32, (sub, LANES), 1) < HEAD_DIM

    ctx = [(_key_tile(ktc_ref.at[kv], 0, ktc_ref.shape[1]), vc_ref[kv], _stacked_sink(sink_ref, kv, sub))
           for kv in range(N_KV_HEADS)]

    def logits(j, kv):
        blk = first_blk + j
        prev_blk, next_blk = jnp.maximum(blk - 1, 0), jnp.minimum(blk + 1, nkb - 1)
        q = q_ref[kv * KV_GROUP:(kv + 1) * KV_GROUP, j * sub:(j + 1) * sub, :].reshape(rows, LANES)
        kt = jnp.concatenate([kt_ref[kv, prev_blk], kt_ref[kv, blk], kt_ref[kv, next_blk]], axis=1)
        s = jnp.dot(q, kt, preferred_element_type=F32)
        sc = jnp.dot(q, ctx[kv][0], preferred_element_type=F32)
        return s, sc

    def attend(j, kv, s, sc):
        blk = first_blk + j
        prev_blk, next_blk = jnp.maximum(blk - 1, 0), jnp.minimum(blk + 1, nkb - 1)
        _, vc, sk = ctx[kv]
        v = jnp.concatenate([v_ref[kv, pl.ds(pl.multiple_of(kb * sub, sub), sub), :]
                             for kb in (prev_blk, blk, next_blk)], axis=0)
        s_prev = jnp.where(jnp.logical_and(c_in >= r_in, blk > 0), s[:, 0:sub], NEG_INF)
        s_own = s[:, sub:2 * sub]
        s_next = jnp.where(jnp.logical_and(c_in <= r_in, blk < nkb - 1), s[:, 2 * sub:3 * sub], NEG_INF)
        mx = jnp.maximum(jnp.maximum(s_prev, s_own), s_next)
        for t in range(sc.shape[1] // LANES):
            mx = jnp.maximum(mx, sc[:, t * LANES:(t + 1) * LANES])
        m = jnp.maximum(jnp.max(mx, axis=-1, keepdims=True), sk)
        p = jnp.concatenate([jnp.exp(s_prev - m), jnp.exp(s_own - m), jnp.exp(s_next - m)], axis=1)
        pv = jnp.dot(p.astype(BF16), v, preferred_element_type=F32)
        pv = pv + jnp.dot(jnp.exp(sc - m).astype(BF16), vc, preferred_element_type=F32)
        l = pv[:, HEAD_DIM:HEAD_DIM + 1] + jnp.exp(sk - m)
        o = pv / l
        return [o[g * sub:(g + 1) * sub] for g in range(KV_GROUP)]

    units = [(j, kv) for j in range(tq // sub) for kv in range(N_KV_HEADS)]
    pending = logits(*units[0])
    outs = []
    for idx, (j, kv) in enumerate(units):
        ahead = logits(*units[idx + 1]) if idx + 1 < len(units) else None
        outs += attend(j, kv, *pending)
        pending = ahead
        if kv == N_KV_HEADS - 1:
            for c, chunk in enumerate(_merge_heads(outs, lo)):
                o_ref[j * sub:(j + 1) * sub, c * LANES:(c + 1) * LANES] = chunk.astype(BF16)
            outs = []


def _window_attn(q, kt, v, ktc, vc, sink, tq):
    b, _, n, _ = q.shape
    lc = vc.shape[2]
    assert tq % WINDOW == 0 and WINDOW == LANES
    batch4 = lambda bi, i: (bi, 0, 0, 0)
    batch5 = lambda bi, i: (bi, 0, 0, 0, 0)
    return pl.pallas_call(
        _window_attn_kernel,
        out_shape=jax.ShapeDtypeStruct((b, n, N_Q_HEADS * HEAD_DIM), BF16),
        grid=(b, n // tq),
        in_specs=[
            pl.BlockSpec((None, N_Q_HEADS, tq, LANES), lambda bi, i: (bi, 0, i, 0)),
            pl.BlockSpec((None, N_KV_HEADS, n // LANES, LANES, LANES), batch5),
            pl.BlockSpec((None, N_KV_HEADS, n, LANES), batch4),
            pl.BlockSpec((None, N_KV_HEADS, lc // LANES, LANES, LANES), batch5),
            pl.BlockSpec((None, N_KV_HEADS, lc, LANES), batch4),
            pl.BlockSpec(sink.shape, lambda bi, i: (0, 0)),
        ],
        out_specs=pl.BlockSpec((None, tq, N_Q_HEADS * HEAD_DIM), lambda bi, i: (bi, i, 0)),
        compiler_params=_cparams("arbitrary", "arbitrary"),
        name="window_attn",
    )(q, kt, v, ktc, vc, sink)


def _ctx_attn_kernel(*refs, has_sink):
    if has_sink:
        q_ref, kt_ref, v_ref, sink_ref, o_ref = refs
    else:
        q_ref, kt_ref, v_ref, o_ref = refs
    tq = q_ref.shape[1]
    rows = KV_GROUP * tq
    lo = lax.broadcasted_iota(jnp.int32, (tq, LANES), 1) < HEAD_DIM
    outs = []
    for kv in range(N_KV_HEADS):
        q = q_ref[kv * KV_GROUP:(kv + 1) * KV_GROUP].reshape(rows, LANES)
        s = jnp.dot(q, _key_tile(kt_ref.at[kv], 0, kt_ref.shape[1]), preferred_element_type=F32)
        m = jnp.max(s, axis=-1, keepdims=True)
        if has_sink:
            sk = _stacked_sink(sink_ref, kv, tq)
            m = jnp.maximum(m, sk)
        pv = jnp.dot(jnp.exp(s - m).astype(BF16), v_ref[kv], preferred_element_type=F32)
        l = pv[:, HEAD_DIM:HEAD_DIM + 1]
        if has_sink:
            l = l + jnp.exp(sk - m)
        o = pv / l
        outs += [o[g * tq:(g + 1) * tq] for g in range(KV_GROUP)]
    for c, chunk in enumerate(_merge_heads(outs, lo)):
        o_ref[:, c * LANES:(c + 1) * LANES] = chunk.astype(BF16)


def _ctx_attn(q, kt, v, sink):
    b, _, n, _ = q.shape
    args = [q, kt, v]
    in_specs = [pl.BlockSpec((None,) + a.shape[1:], lambda bi, nd=a.ndim: (bi,) + (0,) * (nd - 1)) for a in args]
    if sink is not None:
        args.append(sink)
        in_specs.append(pl.BlockSpec(sink.shape, lambda bi: (0, 0)))
    return pl.pallas_call(
        functools.partial(_ctx_attn_kernel, has_sink=sink is not None),
        out_shape=jax.ShapeDtypeStruct((b, n, N_Q_HEADS * HEAD_DIM), BF16),
        grid=(b,),
        in_specs=in_specs,
        out_specs=pl.BlockSpec((None, n, N_Q_HEADS * HEAD_DIM), lambda bi: (bi, 0, 0)),
        compiler_params=_cparams("arbitrary"),
        name="ctx_attn",
    )(*args)


def _global_attn_kernel(*refs, tk, rb, exact_max):
    if exact_max:
        q_ref, kt_ref, v_ref, ktc_ref, vc_ref, o_ref, qs_ref, acc_ref, mx_ref = refs
    else:
        bound_ref, q_ref, kt_ref, v_ref, ktc_ref, vc_ref, o_ref, qs_ref, acc_ref = refs
    tq = q_ref.shape[1]
    rows = KV_GROUP * tq
    kb_per_step = tk // LANES
    n_steps = kt_ref.shape[1] // kb_per_step
    lane = lax.broadcasted_iota(jnp.int32, (rows, LANES), 1)

    def key_steps(fn):
        def step(kc, carry):
            for kv in range(N_KV_HEADS):
                fn(kv, _key_tile(kt_ref.at[kv], kc * kb_per_step, kb_per_step),
                   v_ref[kv, pl.ds(pl.multiple_of(kc * tk, tk), tk), :])
            return carry
        lax.fori_loop(0, n_steps, step, 0)
        for kv in range(N_KV_HEADS):
            fn(kv, _key_tile(ktc_ref.at[kv], 0, ktc_ref.shape[1]), vc_ref[kv])

    def q_rows(kv):
        return q_ref[kv * KV_GROUP:(kv + 1) * KV_GROUP].reshape(rows, LANES)

    if exact_max:
        mx_ref[...] = jnp.full(mx_ref.shape, NEG_INF, F32)

        def track_max(kv, kt, v):
            s = jnp.dot(q_rows(kv), kt, preferred_element_type=F32)
            mx = mx_ref[kv]
            for j in range(kt.shape[1] // LANES):
                mx = jnp.maximum(mx, s[:, j * LANES:(j + 1) * LANES])
            mx_ref[kv] = mx

        key_steps(track_max)

    for kv in range(N_KV_HEADS):
        if exact_max:
            m = jnp.max(mx_ref[kv], axis=-1, keepdims=True)
        else:
            m = jnp.broadcast_to(bound_ref[0:1, 0:1], (rows, 1))
        qs = q_rows(kv).astype(F32)
        rest = m
        for piece in range(N_SHIFT):
            part = rest.astype(BF16).astype(F32)
            qs = jnp.where(lane == HEAD_DIM + piece, -part, qs)
            rest = rest - part
        qs_ref[kv] = qs.astype(BF16)

    acc_ref[...] = jnp.zeros(acc_ref.shape, F32)

    def accumulate(kv, kt, v):
        for r in range(rows // rb):
            rsl = slice(r * rb, (r + 1) * rb)
            s = jnp.dot(qs_ref[kv, rsl, :], kt, preferred_element_type=F32)
            p = jnp.exp(s).astype(BF16)
            acc_ref[kv, rsl, :] += jnp.dot(p, v, preferred_element_type=F32)

    key_steps(accumulate)

    outs = []
    for hd in range(N_Q_HEADS):
        a = acc_ref[hd // KV_GROUP, (hd % KV_GROUP) * tq:(hd % KV_GROUP + 1) * tq, :]
        outs.append(a / a[:, HEAD_DIM:HEAD_DIM + 1])
    lo = lax.broadcasted_iota(jnp.int32, (tq, LANES), 1) < HEAD_DIM
    for c, chunk in enumerate(_merge_heads(outs, lo)):
        o_ref[:, c * LANES:(c + 1) * LANES] = chunk.astype(BF16)


def _global_attn(bound, q, kt, v, ktc, vc, *, tq, tk, rb, exact_max):
    b, _, n, _ = q.shape
    lc = vc.shape[2]
    rows = KV_GROUP * tq
    batch4 = lambda bi, i: (bi, 0, 0, 0)
    batch5 = lambda bi, i: (bi, 0, 0, 0, 0)
    args = [q, kt, v, ktc, vc]
    in_specs = [
        pl.BlockSpec((None, N_Q_HEADS, tq, LANES), lambda bi, i: (bi, 0, i, 0)),
        pl.BlockSpec((None, N_KV_HEADS, n // LANES, LANES, LANES), batch5),
        pl.BlockSpec((None, N_KV_HEADS, n, LANES), batch4),
        pl.BlockSpec((None, N_KV_HEADS, lc // LANES, LANES, LANES), batch5),
        pl.BlockSpec((None, N_KV_HEADS, lc, LANES), batch4),
    ]
    scratch = [pltpu.VMEM((N_KV_HEADS, rows, LANES), BF16), pltpu.VMEM((N_KV_HEADS, rows, LANES), F32)]
    if exact_max:
        scratch.append(pltpu.VMEM((N_KV_HEADS, rows, LANES), F32))
    else:
        args.insert(0, bound)
        in_specs.insert(0, pl.BlockSpec(bound.shape, lambda bi, i: (0, 0)))
    return pl.pallas_call(
        functools.partial(_global_attn_kernel, tk=tk, rb=rb, exact_max=exact_max),
        out_shape=jax.ShapeDtypeStruct((b, n, N_Q_HEADS * HEAD_DIM), BF16),
        grid=(b, n // tq),
        in_specs=in_specs,
        out_specs=pl.BlockSpec((None, tq, N_Q_HEADS * HEAD_DIM), lambda bi, i: (bi, i, 0)),
        scratch_shapes=scratch,
        compiler_params=_cparams("arbitrary", "arbitrary"),
        name="global_attn_exact_max" if exact_max else "global_attn",
    )(*args)


def _out_proj_kernel(x_ref, a_ref, ap_ref, an_ref, ow_ref, og_ref, wpool_ref, pscale_ref, wo_ref, g_ref, mod_ref,
                     o_ref, ext_ref, *, n):
    i = pl.program_id(1)
    tm = x_ref.shape[0]
    ext_ref[0:POOL_HALO, :] = jnp.where(i > 0, ap_ref[...], 0.0)
    ext_ref[POOL_HALO:POOL_HALO + tm, :] = a_ref[...]
    ext_ref[POOL_HALO + tm:2 * POOL_HALO + tm, :] = jnp.where(i < pl.num_programs(1) - 1, an_ref[...], 0.0)

    pw = a_ref.shape[1]
    aw = ow_ref.shape[1]
    y = jnp.dot(ow_ref[...], wo_ref[pw:pw + aw, :], preferred_element_type=F32)
    y = y + jnp.dot(og_ref[...], wo_ref[pw + aw:pw + 2 * aw, :], preferred_element_type=F32)

    t = i * tm + lax.broadcasted_iota(jnp.int32, (tm, 1), 0)
    lo = lax.broadcasted_iota(jnp.int32, (tm, LANES), 1) < HEAD_DIM

    def count(w):
        hi_t = jnp.minimum(t + (w - 1 - w // 2), n - 1)
        lo_t = jnp.maximum(t - w // 2, 0)
        return (hi_t - lo_t + 1).astype(F32)

    feats = []
    for c in range(2):
        w_a, w_b = POOL_WINDOWS[2 * c], POOL_WINDOWS[2 * c + 1]

        def shifted(k, c=c):
            return ext_ref[pl.ds(POOL_HALO + k, tm), c * LANES:(c + 1) * LANES]

        x0 = shifted(0)
        acc = x0
        sums = {}
        reach = 0
        for w in (w_a, w_b):
            while reach < w // 2:
                reach += 1
                acc = acc + shifted(-reach)
                if reach - 1 >= 1:
                    acc = acc + shifted(reach - 1)
            sums[w] = acc
        mean = jnp.where(lo, sums[w_a] / count(w_a), sums[w_b] / count(w_b))
        feats.append(mean - x0)
    feats = jnp.concatenate(feats, axis=1).astype(BF16)
    y_pool = jnp.dot(feats, wpool_ref[...], preferred_element_type=F32) * pscale_ref[...]

    y = y + jnp.dot(y_pool.astype(BF16), wo_ref[0:pw, :], preferred_element_type=F32)
    o_ref[...] = x_ref[...] + mod_ref[2:3, :] * _rms(y, g_ref[...])


def _out_proj(x, a, o_win, o_glb, w_pool_bd, pool_scale, w_out, g_post, mod, tm):
    b, n, d = x.shape
    pw = a.shape[2]
    aw = o_win.shape[2]
    hb = tm // POOL_HALO
    row = lambda bi, i: (0, 0)
    tile = lambda bi, i: (bi, i, 0)
    return pl.pallas_call(
        functools.partial(_out_proj_kernel, n=n),
        out_shape=jax.ShapeDtypeStruct((b, n, d), F32),
        grid=(b, n // tm),
        in_specs=[
            pl.BlockSpec((None, tm, d), tile),
            pl.BlockSpec((None, tm, pw), tile),
            pl.BlockSpec((None, POOL_HALO, pw), lambda bi, i: (bi, jnp.maximum(i * hb - 1, 0), 0)),
            pl.BlockSpec((None, POOL_HALO, pw), lambda bi, i: (bi, jnp.minimum((i + 1) * hb, n // POOL_HALO - 1), 0)),
            pl.BlockSpec((None, tm, aw), tile),
            pl.BlockSpec((None, tm, aw), tile),
            _resident((pw, pw), row),
            pl.BlockSpec((1, pw), row),
            _resident((d, d), row),
            pl.BlockSpec((1, d), row),
            pl.BlockSpec((None, N_MOD, d), lambda bi, i: (bi, 0, 0)),
        ],
        out_specs=pl.BlockSpec((None, tm, d), tile),
        scratch_shapes=[pltpu.VMEM((tm + 2 * POOL_HALO, pw), F32)],
        compiler_params=_cparams("arbitrary", "arbitrary"),
        name="out_proj",
    )(x, a, a, a, o_win, o_glb, w_pool_bd, pool_scale, w_out, g_post, mod)


def _ffn_tile(x, mod_ref, gpre_ref, gpost_ref, wg_ref, wu_ref, wd_ref, chunk):
    h = _rms(x, gpre_ref[...])
    h = (h * (1.0 + mod_ref[4:5, :]) + mod_ref[3:4, :]).astype(BF16)
    f = None
    for c in range(wg_ref.shape[1] // chunk):
        cols = slice(c * chunk, (c + 1) * chunk)
        g = jnp.dot(h, wg_ref[:, cols], preferred_element_type=F32)
        u = jnp.dot(h, wu_ref[:, cols], preferred_element_type=F32)
        act = ((g * jax.nn.sigmoid(g)) * u).astype(BF16)
        part = jnp.dot(act, wd_ref[cols, :], preferred_element_type=F32)
        f = part if f is None else f + part
    return x + mod_ref[5:6, :] * _rms(f, gpost_ref[...])


def _ffn_kernel(x_ref, mod_ref, gpre_ref, gpost_ref, wg_ref, wu_ref, wd_ref, o_ref, *, chunk):
    o_ref[...] = _ffn_tile(x_ref[...], mod_ref, gpre_ref, gpost_ref, wg_ref, wu_ref, wd_ref, chunk)


def _ffn(x, mod, g_pre, g_post, w_gate, w_up, w_down, tm, chunk):
    b, n, d = x.shape
    dff = w_gate.shape[1]
    row = lambda bi, i: (0, 0)
    tile = lambda bi, i: (bi, i, 0)
    return pl.pallas_call(
        functools.partial(_ffn_kernel, chunk=chunk),
        out_shape=jax.ShapeDtypeStruct((b, n, d), F32),
        grid=(b, n // tm),
        in_specs=[
            pl.BlockSpec((None, tm, d), tile),
            pl.BlockSpec((None, N_MOD, d), lambda bi, i: (bi, 0, 0)),
            pl.BlockSpec((1, d), row),
            pl.BlockSpec((1, d), row),
            _resident((d, dff), row),
            _resident((d, dff), row),
            _resident((dff, d), row),
        ],
        out_specs=pl.BlockSpec((None, tm, d), tile),
        compiler_params=_cparams("arbitrary", "arbitrary"),
        name="ffn",
    )(x, mod, g_pre, g_post, w_gate, w_up, w_down)


def _ffn_in_proj_kernel(x_ref, modf_ref, gpre_ref, gpost_ref, wg_ref, wu_ref, wd_ref,
                        modi_ref, gin_ref, win_ref, cos_ref, sin_ref, gq_ref, gk_ref,
                        o_ref, a_ref, qw_ref, ktw_ref, vw_ref, qg_ref, ktg_ref, vg_ref, u_ref, *, chunk, st):
    tm = x_ref.shape[0]

    @pl.when(pl.program_id(0) == 0)
    def _():
        u_ref[...] = jnp.zeros(u_ref.shape, F32)

    _in_proj_epilogue(lambda sub: u_ref[sub * st:(sub + 1) * st, :], tm, st, cos_ref, sin_ref, gq_ref, gk_ref,
                      a_ref, qw_ref, ktw_ref, vw_ref, qg_ref, ktg_ref, vg_ref)
    y = _ffn_tile(x_ref[...], modf_ref, gpre_ref, gpost_ref, wg_ref, wu_ref, wd_ref, chunk)
    o_ref[...] = y
    u_ref[...] = _project(y, modi_ref, gin_ref, win_ref)


def _ffn_in_proj(x, mod_f, g_pre_f, g_post_f, w_gate, w_up, w_down, mod_i, g_pre_i, w_in, cos, sin, gq, gk, tm, chunk):
    b, n, d = x.shape
    dff = w_gate.shape[1]
    nw = w_in.shape[1]
    tpb = n // tm
    last = b * tpb - 1
    cur = lambda s: (jnp.minimum(s, last) // tpb, jnp.minimum(s, last) % tpb)
    prev = lambda s: (jnp.maximum(s - 1, 0) // tpb, jnp.maximum(s - 1, 0) % tpb)
    out_shape, out_specs = _in_proj_outputs(b, n, tm, prev)
    row = lambda s: (0, 0)
    outs = pl.pallas_call(
        functools.partial(_ffn_in_proj_kernel, chunk=chunk, st=min(tm, 256)),
        out_shape=(jax.ShapeDtypeStruct((b, n, d), F32),) + out_shape,
        grid=(b * tpb + 1,),
        in_specs=[
            pl.BlockSpec((None, tm, d), lambda s: (cur(s)[0], cur(s)[1], 0)),
            pl.BlockSpec((None, N_MOD, d), lambda s: (cur(s)[0], 0, 0)),
            pl.BlockSpec((1, d), row),
            pl.BlockSpec((1, d), row),
            _resident((d, dff), row),
            _resident((d, dff), row),
            _resident((dff, d), row),
            pl.BlockSpec((None, N_MOD, d), lambda s: (cur(s)[0], 0, 0)),
            pl.BlockSpec((1, d), row),
            _resident((d, nw), row),
            pl.BlockSpec((tm, LANES), lambda s: (prev(s)[1], 0)),
            pl.BlockSpec((tm, LANES), lambda s: (prev(s)[1], 0)),
            pl.BlockSpec((1, LANES), row),
            pl.BlockSpec((1, LANES), row),
        ],
        out_specs=(pl.BlockSpec((None, tm, d), lambda s: (cur(s)[0], cur(s)[1], 0)),) + out_specs,
        scratch_shapes=[pltpu.VMEM((tm, nw), F32)],
        compiler_params=_cparams("arbitrary"),
        name="ffn_in_proj",
    )(x, mod_f, g_pre_f, g_post_f, w_gate, w_up, w_down, mod_i, g_pre_i, w_in, cos, sin, gq, gk)
    return outs[0], outs[1:]


def _rope_tables(n):
    rows = n // GRID_W
    row = jnp.broadcast_to(jnp.arange(rows)[:, None], (rows, GRID_W)).reshape(-1).astype(F32)
    col = jnp.broadcast_to(jnp.arange(GRID_W)[None, :], (rows, GRID_W)).reshape(-1).astype(F32)
    freq = ROPE_THETA ** (-jnp.arange(ROPE_FREQS, dtype=F32) / ROPE_FREQS)
    ar, ac = row[:, None] * freq, col[:, None] * freq
    cos = jnp.concatenate([jnp.cos(ar), jnp.cos(ar), jnp.cos(ac), jnp.cos(ac)], axis=1)
    sin = jnp.concatenate([-jnp.sin(ar), jnp.sin(ar), -jnp.sin(ac), jnp.sin(ac)], axis=1)
    return jnp.tile(cos, (1, LANES // HEAD_DIM)), jnp.tile(sin, (1, LANES // HEAD_DIM))


def _tile(n, pref):
    t = min(n, pref)
    assert n % t == 0, (n, t)
    return t


def kernel(x, c, ctx, c_ctx, w_mod, b_mod, g_pre_mix, g_post_mix, g_pre_ffn, g_post_ffn, w_in, w_pool, pool_scale, win_sink, g_qnorm, g_knorm, w_out, w_gate, w_up, w_down):
    b, n, d = x.shape
    lc = ctx.shape[1]
    depth = w_mod.shape[0]
    assert d == 1024 and w_in.shape[2] == 1536 and n % GRID_W == 0

    cos, sin = _rope_tables(n)
    cos_c, sin_c = jnp.ones((lc, LANES), F32), jnp.zeros((lc, LANES), F32)

    mod_rows = -(-(b + 1) // 8) * 8
    cc = jnp.concatenate([c, c_ctx[None, :], jnp.zeros((mod_rows - b - 1, d), F32)], axis=0)
    mod_all = _modulation(cc, w_mod, b_mod)

    tm = _tile(n, 512)
    tq_win = _tile(n, 1024)
    tq_glb = _tile(n, 512)
    tk_glb = _tile(n, 1024)
    rb_glb = 256
    ffn_chunk = 256

    def layer(l):
        return dict(
            m=mod_all[l, :b].reshape(b, N_MOD, d),
            mc=jnp.broadcast_to(mod_all[l, b].reshape(1, N_MOD, d), (b, N_MOD, d)),
            w_in=w_in[l].astype(BF16),
            gq=jnp.tile(g_qnorm[l], LANES // HEAD_DIM)[None, :],
            gk=jnp.tile(g_knorm[l], LANES // HEAD_DIM)[None, :],
            g_pre=g_pre_mix[l][None, :],
        )

    xc = ctx
    p = layer(0)
    proj = _in_proj(x, p["m"], p["g_pre"], p["w_in"], cos, sin, p["gq"], p["gk"], tm)
    for l in range(depth):
        last = l == depth - 1
        m, mc, gq, gk = p["m"], p["mc"], p["gq"], p["gk"]
        w_out_l = w_out[l].astype(BF16)
        w_gate_l, w_up_l, w_down_l = w_gate[l].astype(BF16), w_up[l].astype(BF16), w_down[l].astype(BF16)
        w_pool_bd = jax.scipy.linalg.block_diag(*[w_pool[l, g] for g in range(len(POOL_WINDOWS))]).astype(BF16)
        pscale = pool_scale[l][None, :]
        sink = jnp.zeros((8, LANES), F32).at[:N_Q_HEADS].set(jnp.broadcast_to(win_sink[l][:, None], (N_Q_HEADS, LANES)))
        g_post = g_post_mix[l][None, :]
        g_pre_f, g_post_f = g_pre_ffn[l][None, :], g_post_ffn[l][None, :]

        a, qw, ktw, vw, qg, ktg, vg = proj
        ac, qwc, ktwc, vwc, qgc, ktgc, vgc = _in_proj(xc, mc, p["g_pre"], p["w_in"], cos_c, sin_c, gq, gk, lc)

        o_win = _window_attn(qw, ktw, vw, ktwc, vwc, sink, tq_win)
        bound = ATTN_SCALE * HEAD_DIM * jnp.max(jnp.abs(g_qnorm[l])) * jnp.max(jnp.abs(g_knorm[l]))
        glb = functools.partial(_global_attn, tq=tq_glb, tk=tk_glb, rb=rb_glb)
        o_glb = lax.cond(bound <= SAFE_SHIFT,
                         functools.partial(glb, exact_max=False), functools.partial(glb, exact_max=True),
                         jnp.full((1, LANES), bound, F32), qg, ktg, vg, ktgc, vgc)
        x_new = _out_proj(x, a, o_win, o_glb, w_pool_bd, pscale, w_out_l, g_post, m, tm)

        if last:
            x = _ffn(x_new, m, g_pre_f, g_post_f, w_gate_l, w_up_l, w_down_l, tm, ffn_chunk)
        else:
            oc_win = _ctx_attn(qwc, ktwc, vwc, sink)
            oc_glb = _ctx_attn(qgc, ktgc, vgc, None)
            xc = _out_proj(xc, ac, oc_win, oc_glb, w_pool_bd, pscale, w_out_l, g_post, mc, lc)
            xc = _ffn(xc, mc, g_pre_f, g_post_f, w_gate_l, w_up_l, w_down_l, lc, ffn_chunk)
            p = layer(l + 1)
            x, proj = _ffn_in_proj(x_new, m, g_pre_f, g_post_f, w_gate_l, w_up_l, w_down_l,
                                   p["m"], p["g_pre"], p["w_in"], cos, sin, p["gq"], p["gk"], tm, ffn_chunk)
    return x
```

```python
import functools

import jax
import jax.numpy as jnp
from jax import lax
from jax.experimental import pallas as pl
from jax.experimental.pallas import tpu as pltpu

F32 = jnp.float32
BF16 = jnp.bfloat16

HEAD_DIM = 64
GRID_W = 64
ROPE_FREQS = HEAD_DIM // 4
ROPE_THETA = 10000.0
NORM_EPS = 1e-6
NEG_INF = -1e30
ATTN_SCALE = HEAD_DIM ** -0.5
WINDOW = 128
POOL_WINDOWS = (2, 4, 8, 16)
POOL_HALO = 16
N_Q_HEADS = 6
N_KV_HEADS = 2
KV_GROUP = N_Q_HEADS // N_KV_HEADS
N_MOD = 6
N_SHIFT = 3
SAFE_SHIFT = 30.0

LANES = 128
VMEM_LIMIT = 56 * 1024 * 1024


def _cparams(*sem):
    return pltpu.CompilerParams(dimension_semantics=sem, vmem_limit_bytes=VMEM_LIMIT)


def _resident(block_shape, index_map):
    return pl.BlockSpec(block_shape, index_map, pipeline_mode=pl.Buffered(1))


def _rms(x, g):
    ms = jnp.mean(x * x, axis=-1, keepdims=True)
    return (x * lax.rsqrt(ms + NORM_EPS)) * g


def _mod_kernel(c_ref, w_ref, b_ref, o_ref):
    c = c_ref[...]
    s = c * jax.nn.sigmoid(c)
    o_ref[...] = jnp.dot(s.astype(BF16), w_ref[...].astype(BF16), preferred_element_type=F32) + b_ref[...]


def _modulation(cc, w_mod, b_mod):
    depth, d, dm = w_mod.shape
    rows = cc.shape[0]
    tn = 1536
    return pl.pallas_call(
        _mod_kernel,
        out_shape=jax.ShapeDtypeStruct((depth, rows, dm), F32),
        grid=(depth, dm // tn),
        in_specs=[
            pl.BlockSpec((rows, d), lambda l, j: (0, 0)),
            pl.BlockSpec((None, d, tn), lambda l, j: (l, 0, j)),
            pl.BlockSpec((None, 1, tn), lambda l, j: (l, 0, j)),
        ],
        out_specs=pl.BlockSpec((None, rows, tn), lambda l, j: (l, 0, j)),
        compiler_params=_cparams("arbitrary", "arbitrary"),
        name="modulation",
    )(cc, w_mod, b_mod.reshape(depth, 1, dm))


def _in_proj_epilogue(u_of, n_rows, st, cos_ref, sin_ref, gq_ref, gk_ref, qfill_ref,
                      a_ref, qw_ref, ktw_ref, vw_ref, qg_ref, ktg_ref, vg_ref):
    lane = lax.broadcasted_iota(jnp.int32, (st, LANES), 1)
    lo = lane < HEAD_DIM
    first = (lane & ROPE_FREQS) == 0
    shift_rows = (lax.broadcasted_iota(jnp.int32, (HEAD_DIM, LANES), 0) < N_SHIFT).astype(BF16)
    sum_lane = (lane == HEAD_DIM).astype(F32)
    gq, gk = gq_ref[...], gk_ref[...]
    qfill = qfill_ref[...]

    def head_norm(t, g):
        sq = t * t
        s_lo = jnp.sum(jnp.where(lo, sq, 0.0), axis=-1, keepdims=True)
        s_hi = jnp.sum(jnp.where(lo, 0.0, sq), axis=-1, keepdims=True)
        ms = jnp.where(lo, s_lo, s_hi) * (1.0 / HEAD_DIM)
        return (t * lax.rsqrt(ms + NORM_EPS)) * g

    for sub in range(n_rows // st):
        rs = slice(sub * st, (sub + 1) * st)
        u = u_of(sub)
        cos = cos_ref[rs, :]
        sin = sin_ref[rs, :]

        def rope(t):
            partner = jnp.where(first, pltpu.roll(t, LANES - ROPE_FREQS, 1), pltpu.roll(t, ROPE_FREQS, 1))
            return t * cos + partner * sin

        def store_branch(q_chunks, k, v, fill, q_ref, kt_ref, v_ref):
            for hd in range(N_Q_HEADS):
                t = q_chunks[hd // 2]
                if hd % 2:
                    t = pltpu.roll(t, HEAD_DIM, 1)
                q_ref[hd, rs, :] = jnp.where(lo, t, fill).astype(BF16)
            for r in range(st // LANES):
                kb = sub * (st // LANES) + r
                kt = jnp.transpose(k[r * LANES:(r + 1) * LANES, :]).astype(BF16)
                for kv in range(N_KV_HEADS):
                    kt_ref[kv, kb, 0:HEAD_DIM, :] = kt[kv * HEAD_DIM:(kv + 1) * HEAD_DIM, :]
                    kt_ref[kv, kb, HEAD_DIM:LANES, :] = shift_rows
            v_ref[0, rs, :] = jnp.where(lo, v, sum_lane).astype(BF16)
            v_ref[1, rs, :] = jnp.where(lo, pltpu.roll(v, HEAD_DIM, 1), sum_lane).astype(BF16)

        a_ref[rs, :] = u[:, 0:256]
        store_branch([rope(u[:, 256 + LANES * c:384 + LANES * c]) * ATTN_SCALE for c in range(3)],
                     rope(u[:, 640:768]), u[:, 768:896], 0.0, qw_ref, ktw_ref, vw_ref)
        store_branch([rope(head_norm(u[:, 896 + LANES * c:1024 + LANES * c], gq)) * ATTN_SCALE for c in range(3)],
                     rope(head_norm(u[:, 1280:1408], gk)), u[:, 1408:1536], qfill, qg_ref, ktg_ref, vg_ref)


def _project(x, mod_ref, g_ref, w_ref):
    h = _rms(x, g_ref[...])
    h = h * (1.0 + mod_ref[1:2, :]) + mod_ref[0:1, :]
    return jnp.dot(h.astype(BF16), w_ref[...], preferred_element_type=F32)


def _in_proj_kernel(x_ref, mod_ref, g_ref, w_ref, cos_ref, sin_ref, gq_ref, gk_ref, qfill_ref, *out_refs, st):
    tm = x_ref.shape[0]
    us = [_project(x_ref[sub * st:(sub + 1) * st, :], mod_ref, g_ref, w_ref) for sub in range(tm // st)]
    _in_proj_epilogue(lambda sub: us[sub], tm, st, cos_ref, sin_ref, gq_ref, gk_ref, qfill_ref, *out_refs)


def _in_proj_outputs(b, n, tm, tile_of):
    qshape = jax.ShapeDtypeStruct((b, N_Q_HEADS, n, LANES), BF16)
    ktshape = jax.ShapeDtypeStruct((b, N_KV_HEADS, n // LANES, LANES, LANES), BF16)
    vshape = jax.ShapeDtypeStruct((b, N_KV_HEADS, n, LANES), BF16)

    def spec(block, where):
        return pl.BlockSpec(block, lambda *g: where(*tile_of(*g)))

    aspec = spec((None, tm, 256), lambda bi, i: (bi, i, 0))
    qspec = spec((None, N_Q_HEADS, tm, LANES), lambda bi, i: (bi, 0, i, 0))
    ktspec = spec((None, N_KV_HEADS, tm // LANES, LANES, LANES), lambda bi, i: (bi, 0, i, 0, 0))
    vspec = spec((None, N_KV_HEADS, tm, LANES), lambda bi, i: (bi, 0, i, 0))
    shapes = (jax.ShapeDtypeStruct((b, n, 256), F32), qshape, ktshape, vshape, qshape, ktshape, vshape)
    return shapes, (aspec, qspec, ktspec, vspec, qspec, ktspec, vspec)


def _in_proj(x, mod, g_pre, w_in, cos, sin, gq, gk, qfill, tm):
    b, n, d = x.shape
    nw = w_in.shape[1]
    out_shape, out_specs = _in_proj_outputs(b, n, tm, lambda bi, i: (bi, i))
    row = lambda bi, i: (0, 0)
    return pl.pallas_call(
        functools.partial(_in_proj_kernel, st=min(tm, 256)),
        out_shape=out_shape,
        grid=(b, n // tm),
        in_specs=[
            pl.BlockSpec((None, tm, d), lambda bi, i: (bi, i, 0)),
            pl.BlockSpec((None, N_MOD, d), lambda bi, i: (bi, 0, 0)),
            pl.BlockSpec((1, d), row),
            _resident((d, nw), row),
            pl.BlockSpec((tm, LANES), lambda bi, i: (i, 0)),
            pl.BlockSpec((tm, LANES), lambda bi, i: (i, 0)),
            pl.BlockSpec((1, LANES), row),
            pl.BlockSpec((1, LANES), row),
            pl.BlockSpec((1, LANES), row),
        ],
        out_specs=out_specs,
        compiler_params=_cparams("arbitrary", "arbitrary"),
        name="in_proj",
    )(x, mod, g_pre, w_in, cos, sin, gq, gk, qfill)


def _merge_heads(outs, lo):
    return [jnp.where(lo, outs[2 * c], pltpu.roll(outs[2 * c + 1], HEAD_DIM, 1)) for c in range(N_Q_HEADS // 2)]


def _key_tile(kt_ref, first_block, n_blocks):
    return jnp.concatenate([kt_ref[first_block + j] for j in range(n_blocks)], axis=1)


def _stacked_sink(sink_ref, kv, rows_per_head):
    return jnp.concatenate([jnp.broadcast_to(sink_ref[hd:hd + 1, 0:1], (rows_per_head, 1))
                            for hd in range(kv * KV_GROUP, (kv + 1) * KV_GROUP)], axis=0)


def _window_units(q_ref, kt_ref, v_ref, ktc_ref, vc_ref, sink_ref, first_blk):
    nkb = kt_ref.shape[1]
    sub = WINDOW
    rows = KV_GROUP * sub
    r_in = lax.broadcasted_iota(jnp.int32, (rows, LANES), 0) & (sub - 1)
    c_in = lax.broadcasted_iota(jnp.int32, (rows, LANES), 1)
    ctx = [(_key_tile(ktc_ref.at[kv], 0, ktc_ref.shape[1]), vc_ref[kv], _stacked_sink(sink_ref, kv, sub))
           for kv in range(N_KV_HEADS)]

    def blocks(j):
        blk = first_blk + j
        return blk, jnp.maximum(blk - 1, 0), jnp.minimum(blk + 1, nkb - 1)

    def logits(j, kv):
        blk, prev_blk, next_blk = blocks(j)
        q = q_ref[kv * KV_GROUP:(kv + 1) * KV_GROUP, pl.ds(pl.multiple_of(j * sub, sub), sub), :].reshape(rows, LANES)
        kt = jnp.concatenate([kt_ref[kv, prev_blk], kt_ref[kv, blk], kt_ref[kv, next_blk]], axis=1)
        s = jnp.dot(q, kt, preferred_element_type=F32)
        sc = jnp.dot(q, ctx[kv][0], preferred_element_type=F32)
        return s, sc

    def attend(j, kv, s, sc):
        blk, prev_blk, next_blk = blocks(j)
        _, vc, sk = ctx[kv]
        v = jnp.concatenate([v_ref[kv, pl.ds(pl.multiple_of(kb * sub, sub), sub), :]
                             for kb in (prev_blk, blk, next_blk)], axis=0)
        s_prev = jnp.where(jnp.logical_and(c_in >= r_in, blk > 0), s[:, 0:sub], NEG_INF)
        s_own = s[:, sub:2 * sub]
        s_next = jnp.where(jnp.logical_and(c_in <= r_in, blk < nkb - 1), s[:, 2 * sub:3 * sub], NEG_INF)
        mx = jnp.maximum(jnp.maximum(s_prev, s_own), s_next)
        for t in range(sc.shape[1] // LANES):
            mx = jnp.maximum(mx, sc[:, t * LANES:(t + 1) * LANES])
        m = jnp.maximum(jnp.max(mx, axis=-1, keepdims=True), sk)
        p = jnp.concatenate([jnp.exp(s_prev - m), jnp.exp(s_own - m), jnp.exp(s_next - m)], axis=1)
        pv = jnp.dot(p.astype(BF16), v, preferred_element_type=F32)
        pv = pv + jnp.dot(jnp.exp(sc - m).astype(BF16), vc, preferred_element_type=F32)
        l = pv[:, HEAD_DIM:HEAD_DIM + 1] + jnp.exp(sk - m)
        o = pv / l
        return [o[g * sub:(g + 1) * sub] for g in range(KV_GROUP)]

    return logits, attend


def _store_heads(o_ref, row0, outs):
    nrows = outs[0].shape[0]
    lo = lax.broadcasted_iota(jnp.int32, (nrows, LANES), 1) < HEAD_DIM
    for c, chunk in enumerate(_merge_heads(outs, lo)):
        o_ref[pl.ds(row0, nrows), c * LANES:(c + 1) * LANES] = chunk.astype(BF16)


def _window_attn_kernel(q_ref, kt_ref, v_ref, ktc_ref, vc_ref, sink_ref, o_ref):
    tq = q_ref.shape[1]
    sub = WINDOW
    logits, attend = _window_units(q_ref, kt_ref, v_ref, ktc_ref, vc_ref, sink_ref, pl.program_id(1) * (tq // sub))
    units = [(j, kv) for j in range(tq // sub) for kv in range(N_KV_HEADS)]
    pending = logits(*units[0])
    outs = []
    for idx, (j, kv) in enumerate(units):
        ahead = logits(*units[idx + 1]) if idx + 1 < len(units) else None
        outs += attend(j, kv, *pending)
        pending = ahead
        if kv == N_KV_HEADS - 1:
            _store_heads(o_ref, j * sub, outs)
            outs = []


def _window_attn(q, kt, v, ktc, vc, sink, tq):
    b, _, n, _ = q.shape
    lc = vc.shape[2]
    assert tq % WINDOW == 0 and WINDOW == LANES
    batch4 = lambda bi, i: (bi, 0, 0, 0)
    batch5 = lambda bi, i: (bi, 0, 0, 0, 0)
    return pl.pallas_call(
        _window_attn_kernel,
        out_shape=jax.ShapeDtypeStruct((b, n, N_Q_HEADS * HEAD_DIM), BF16),
        grid=(b, n // tq),
        in_specs=[
            pl.BlockSpec((None, N_Q_HEADS, tq, LANES), lambda bi, i: (bi, 0, i, 0)),
            pl.BlockSpec((None, N_KV_HEADS, n // LANES, LANES, LANES), batch5),
            pl.BlockSpec((None, N_KV_HEADS, n, LANES), batch4),
            pl.BlockSpec((None, N_KV_HEADS, lc // LANES, LANES, LANES), batch5),
            pl.BlockSpec((None, N_KV_HEADS, lc, LANES), batch4),
            pl.BlockSpec(sink.shape, lambda bi, i: (0, 0)),
        ],
        out_specs=pl.BlockSpec((None, tq, N_Q_HEADS * HEAD_DIM), lambda bi, i: (bi, i, 0)),
        compiler_params=_cparams("arbitrary", "arbitrary"),
        name="window_attn",
    )(q, kt, v, ktc, vc, sink)


def _ctx_attn_kernel(*refs, has_sink):
    if has_sink:
        q_ref, kt_ref, v_ref, sink_ref, o_ref = refs
    else:
        q_ref, kt_ref, v_ref, o_ref = refs
    tq = q_ref.shape[1]
    rows = KV_GROUP * tq
    outs = []
    for kv in range(N_KV_HEADS):
        q = q_ref[kv * KV_GROUP:(kv + 1) * KV_GROUP].reshape(rows, LANES)
        s = jnp.dot(q, _key_tile(kt_ref.at[kv], 0, kt_ref.shape[1]), preferred_element_type=F32)
        m = jnp.max(s, axis=-1, keepdims=True)
        if has_sink:
            sk = _stacked_sink(sink_ref, kv, tq)
            m = jnp.maximum(m, sk)
        pv = jnp.dot(jnp.exp(s - m).astype(BF16), v_ref[kv], preferred_element_type=F32)
        l = pv[:, HEAD_DIM:HEAD_DIM + 1]
        if has_sink:
            l = l + jnp.exp(sk - m)
        o = pv / l
        outs += [o[g * tq:(g + 1) * tq] for g in range(KV_GROUP)]
    _store_heads(o_ref, 0, outs)


def _ctx_attn(q, kt, v, sink):
    b, _, n, _ = q.shape
    args = [q, kt, v]
    in_specs = [pl.BlockSpec((None,) + a.shape[1:], lambda bi, nd=a.ndim: (bi,) + (0,) * (nd - 1)) for a in args]
    if sink is not None:
        args.append(sink)
        in_specs.append(pl.BlockSpec(sink.shape, lambda bi: (0, 0)))
    return pl.pallas_call(
        functools.partial(_ctx_attn_kernel, has_sink=sink is not None),
        out_shape=jax.ShapeDtypeStruct((b, n, N_Q_HEADS * HEAD_DIM), BF16),
        grid=(b,),
        in_specs=in_specs,
        out_specs=pl.BlockSpec((None, n, N_Q_HEADS * HEAD_DIM), lambda bi: (bi, 0, 0)),
        compiler_params=_cparams("arbitrary"),
        name="ctx_attn",
    )(*args)


def _global_attn_kernel(q_ref, kt_ref, v_ref, ktc_ref, vc_ref, o_ref, qs_ref, acc_ref, mx_ref, *, tk, rb):
    tq = q_ref.shape[1]
    rows = KV_GROUP * tq
    kb_per_step = tk // LANES
    n_steps = kt_ref.shape[1] // kb_per_step
    lane = lax.broadcasted_iota(jnp.int32, (rows, LANES), 1)

    def key_steps(fn):
        def step(kc, carry):
            for kv in range(N_KV_HEADS):
                fn(kv, _key_tile(kt_ref.at[kv], kc * kb_per_step, kb_per_step),
                   v_ref[kv, pl.ds(pl.multiple_of(kc * tk, tk), tk), :])
            return carry
        lax.fori_loop(0, n_steps, step, 0)
        for kv in range(N_KV_HEADS):
            fn(kv, _key_tile(ktc_ref.at[kv], 0, ktc_ref.shape[1]), vc_ref[kv])

    def q_rows(kv):
        return q_ref[kv * KV_GROUP:(kv + 1) * KV_GROUP].reshape(rows, LANES)

    mx_ref[...] = jnp.full(mx_ref.shape, NEG_INF, F32)

    def track_max(kv, kt, v):
        s = jnp.dot(q_rows(kv), kt, preferred_element_type=F32)
        mx = mx_ref[kv]
        for j in range(kt.shape[1] // LANES):
            mx = jnp.maximum(mx, s[:, j * LANES:(j + 1) * LANES])
        mx_ref[kv] = mx

    key_steps(track_max)

    for kv in range(N_KV_HEADS):
        qs = q_rows(kv).astype(F32)
        rest = jnp.max(mx_ref[kv], axis=-1, keepdims=True)
        for piece in range(N_SHIFT):
            part = rest.astype(BF16).astype(F32)
            qs = jnp.where(lane == HEAD_DIM + piece, -part, qs)
            rest = rest - part
        qs_ref[kv] = qs.astype(BF16)

    acc_ref[...] = jnp.zeros(acc_ref.shape, F32)

    def accumulate(kv, kt, v):
        for r in range(rows // rb):
            rsl = slice(r * rb, (r + 1) * rb)
            s = jnp.dot(qs_ref[kv, rsl, :], kt, preferred_element_type=F32)
            p = jnp.exp(s).astype(BF16)
            acc_ref[kv, rsl, :] += jnp.dot(p, v, preferred_element_type=F32)

    key_steps(accumulate)

    outs = []
    for hd in range(N_Q_HEADS):
        a = acc_ref[hd // KV_GROUP, (hd % KV_GROUP) * tq:(hd % KV_GROUP + 1) * tq, :]
        outs.append(a / a[:, HEAD_DIM:HEAD_DIM + 1])
    _store_heads(o_ref, 0, outs)


def _global_attn(q, kt, v, ktc, vc, *, tq, tk, rb):
    b, _, n, _ = q.shape
    lc = vc.shape[2]
    rows = KV_GROUP * tq
    batch4 = lambda bi, i: (bi, 0, 0, 0)
    batch5 = lambda bi, i: (bi, 0, 0, 0, 0)
    return pl.pallas_call(
        functools.partial(_global_attn_kernel, tk=tk, rb=rb),
        out_shape=jax.ShapeDtypeStruct((b, n, N_Q_HEADS * HEAD_DIM), BF16),
        grid=(b, n // tq),
        in_specs=[
            pl.BlockSpec((None, N_Q_HEADS, tq, LANES), lambda bi, i: (bi, 0, i, 0)),
            pl.BlockSpec((None, N_KV_HEADS, n // LANES, LANES, LANES), batch5),
            pl.BlockSpec((None, N_KV_HEADS, n, LANES), batch4),
            pl.BlockSpec((None, N_KV_HEADS, lc // LANES, LANES, LANES), batch5),
            pl.BlockSpec((None, N_KV_HEADS, lc, LANES), batch4),
        ],
        out_specs=pl.BlockSpec((None, tq, N_Q_HEADS * HEAD_DIM), lambda bi, i: (bi, i, 0)),
        scratch_shapes=[pltpu.VMEM((N_KV_HEADS, rows, LANES), BF16), pltpu.VMEM((N_KV_HEADS, rows, LANES), F32),
                        pltpu.VMEM((N_KV_HEADS, rows, LANES), F32)],
        compiler_params=_cparams("arbitrary", "arbitrary"),
        name="global_attn_exact_max",
    )(q, kt, v, ktc, vc)


def _attn_kernel(qg_ref, ktg_ref, vg_ref, ktgc_ref, vgc_ref, qw_ref, ktw_ref, vw_ref, ktwc_ref, vwc_ref, sink_ref,
                 og_ref, ow_ref, acc_ref, *, tk, rb):
    tq = qg_ref.shape[1]
    rows = KV_GROUP * tq
    sub = WINDOW
    n_sub = tq // sub
    kb_per_step = tk // LANES
    steps_per_sub = ktg_ref.shape[1] // kb_per_step // n_sub
    assert steps_per_sub * n_sub * kb_per_step == ktg_ref.shape[1] and tq % rb == 0

    def global_keys(kv, kt, v, assign):
        for r in range(rows // rb):
            hd, off = kv * KV_GROUP + (r * rb) // tq, (r * rb) % tq
            s = jnp.dot(qg_ref[hd, off:off + rb, :], kt, preferred_element_type=F32)
            pv = jnp.dot(jnp.exp(s).astype(BF16), v, preferred_element_type=F32)
            if assign:
                acc_ref[kv, r * rb:(r + 1) * rb, :] = pv
            else:
                acc_ref[kv, r * rb:(r + 1) * rb, :] += pv

    def global_step(kc):
        for kv in range(N_KV_HEADS):
            global_keys(kv, _key_tile(ktg_ref.at[kv], kc * kb_per_step, kb_per_step),
                        vg_ref[kv, pl.ds(pl.multiple_of(kc * tk, tk), tk), :], False)

    for kv in range(N_KV_HEADS):
        global_keys(kv, _key_tile(ktgc_ref.at[kv], 0, ktgc_ref.shape[1]), vgc_ref[kv], True)

    logits, attend = _window_units(qw_ref, ktw_ref, vw_ref, ktwc_ref, vwc_ref, sink_ref, pl.program_id(1) * n_sub)

    def body(j, carry):
        pending = logits(j, 0)
        outs = []
        for kv in range(N_KV_HEADS):
            for t in range(steps_per_sub // N_KV_HEADS):
                global_step(j * steps_per_sub + kv * (steps_per_sub // N_KV_HEADS) + t)
            ahead = logits(j, kv + 1) if kv + 1 < N_KV_HEADS else None
            outs += attend(j, kv, *pending)
            pending = ahead
        _store_heads(ow_ref, pl.multiple_of(j * sub, sub), outs)
        return carry

    assert steps_per_sub % N_KV_HEADS == 0
    lax.fori_loop(0, n_sub, body, 0)

    outs = []
    for hd in range(N_Q_HEADS):
        a = acc_ref[hd // KV_GROUP, (hd % KV_GROUP) * tq:(hd % KV_GROUP + 1) * tq, :]
        outs.append(a / a[:, HEAD_DIM:HEAD_DIM + 1])
    _store_heads(og_ref, 0, outs)


def _attn(qg, ktg, vg, ktgc, vgc, qw, ktw, vw, ktwc, vwc, sink, *, tq, tk, rb):
    b, _, n, _ = qg.shape
    lc = vgc.shape[2]
    batch4 = lambda bi, i: (bi, 0, 0, 0)
    batch5 = lambda bi, i: (bi, 0, 0, 0, 0)
    branch_specs = [
        pl.BlockSpec((None, N_Q_HEADS, tq, LANES), lambda bi, i: (bi, 0, i, 0)),
        _resident((None, N_KV_HEADS, n // LANES, LANES, LANES), batch5),
        _resident((None, N_KV_HEADS, n, LANES), batch4),
        _resident((None, N_KV_HEADS, lc // LANES, LANES, LANES), batch5),
        _resident((None, N_KV_HEADS, lc, LANES), batch4),
    ]
    oshape = jax.ShapeDtypeStruct((b, n, N_Q_HEADS * HEAD_DIM), BF16)
    ospec = pl.BlockSpec((None, tq, N_Q_HEADS * HEAD_DIM), lambda bi, i: (bi, i, 0))
    return pl.pallas_call(
        functools.partial(_attn_kernel, tk=tk, rb=rb),
        out_shape=(oshape, oshape),
        grid=(b, n // tq),
        in_specs=branch_specs + branch_specs + [pl.BlockSpec(sink.shape, lambda bi, i: (0, 0))],
        out_specs=(ospec, ospec),
        scratch_shapes=[pltpu.VMEM((N_KV_HEADS, KV_GROUP * tq, LANES), F32)],
        compiler_params=_cparams("arbitrary", "arbitrary"),
        name="attn",
    )(qg, ktg, vg, ktgc, vgc, qw, ktw, vw, ktwc, vwc, sink)


def _out_proj_kernel(x_ref, a_ref, ap_ref, an_ref, icnt_ref, ow_ref, og_ref, wpool_ref, pscale_ref, wo_ref, g_ref, mod_ref,
                     o_ref, ext_ref, s2_ref, s4_ref, s8_ref):
    i = pl.program_id(1)
    tm = x_ref.shape[0]
    ext_ref[0:POOL_HALO, :] = jnp.where(i > 0, ap_ref[...], 0.0)
    ext_ref[POOL_HALO:POOL_HALO + tm, :] = a_ref[...]
    ext_ref[POOL_HALO + tm:2 * POOL_HALO + tm, :] = jnp.where(i < pl.num_programs(1) - 1, an_ref[...], 0.0)

    pw = a_ref.shape[1]
    aw = ow_ref.shape[1]
    y = jnp.dot(ow_ref[...], wo_ref[pw:pw + aw, :], preferred_element_type=F32)
    y = y + jnp.dot(og_ref[...], wo_ref[pw + aw:pw + 2 * aw, :], preferred_element_type=F32)

    lo = lax.broadcasted_iota(jnp.int32, (tm, LANES), 1) < HEAD_DIM

    pad = POOL_HALO // 2
    ext_ref[2 * POOL_HALO + tm:2 * POOL_HALO + tm + pad, :] = jnp.zeros((pad, ext_ref.shape[1]), F32)
    r2, r4, r8 = tm + 2 * POOL_HALO, tm + 2 * POOL_HALO - pad, tm + POOL_HALO
    s2_ref[0:r2, :] = ext_ref[0:r2, :] + ext_ref[1:r2 + 1, :]
    s4_ref[0:r4, :] = s2_ref[0:r4, :] + s2_ref[2:r4 + 2, :]
    s8_ref[0:r8, :] = s4_ref[0:r8, LANES:2 * LANES] + s4_ref[4:r8 + 4, LANES:2 * LANES]

    def window_sum(w, c):
        first = POOL_HALO - w // 2
        cols = slice(c * LANES, (c + 1) * LANES)
        if w == 2:
            return s2_ref[first:first + tm, cols]
        if w == 4:
            return s4_ref[first:first + tm, cols]
        if w == 8:
            return s8_ref[first:first + tm, :]
        return s8_ref[first:first + tm, :] + s8_ref[first + 8:first + 8 + tm, :]

    feats = []
    for c in range(2):
        w_a, w_b = POOL_WINDOWS[2 * c], POOL_WINDOWS[2 * c + 1]
        x0 = ext_ref[POOL_HALO:POOL_HALO + tm, c * LANES:(c + 1) * LANES]
        mean = jnp.where(lo, window_sum(w_a, c), window_sum(w_b, c)) * icnt_ref[:, c * LANES:(c + 1) * LANES]
        feats.append(mean - x0)
    feats = jnp.concatenate(feats, axis=1).astype(BF16)
    y_pool = jnp.dot(feats, wpool_ref[...], preferred_element_type=F32) * pscale_ref[...]

    y = y + jnp.dot(y_pool.astype(BF16), wo_ref[0:pw, :], preferred_element_type=F32)
    o_ref[...] = x_ref[...] + mod_ref[2:3, :] * _rms(y, g_ref[...])


def _out_proj(x, a, inv_count, o_win, o_glb, w_pool_bd, pool_scale, w_out, g_post, mod, tm):
    b, n, d = x.shape
    pw = a.shape[2]
    aw = o_win.shape[2]
    hb = tm // POOL_HALO
    row = lambda bi, i: (0, 0)
    tile = lambda bi, i: (bi, i, 0)
    return pl.pallas_call(
        _out_proj_kernel,
        out_shape=jax.ShapeDtypeStruct((b, n, d), F32),
        grid=(b, n // tm),
        in_specs=[
            pl.BlockSpec((None, tm, d), tile),
            pl.BlockSpec((None, tm, pw), tile),
            pl.BlockSpec((None, POOL_HALO, pw), lambda bi, i: (bi, jnp.maximum(i * hb - 1, 0), 0)),
            pl.BlockSpec((None, POOL_HALO, pw), lambda bi, i: (bi, jnp.minimum((i + 1) * hb, n // POOL_HALO - 1), 0)),
            pl.BlockSpec((tm, pw), lambda bi, i: (i, 0)),
            pl.BlockSpec((None, tm, aw), tile),
            pl.BlockSpec((None, tm, aw), tile),
            _resident((pw, pw), row),
            pl.BlockSpec((1, pw), row),
            _resident((d, d), row),
            pl.BlockSpec((1, d), row),
            pl.BlockSpec((None, N_MOD, d), lambda bi, i: (bi, 0, 0)),
        ],
        out_specs=pl.BlockSpec((None, tm, d), tile),
        scratch_shapes=[pltpu.VMEM((tm + 2 * POOL_HALO + POOL_HALO // 2, pw), F32),
                        pltpu.VMEM((tm + 2 * POOL_HALO, pw), F32), pltpu.VMEM((tm + 2 * POOL_HALO, pw), F32),
                        pltpu.VMEM((tm + 2 * POOL_HALO, LANES), F32)],
        compiler_params=_cparams("arbitrary", "arbitrary"),
        name="out_proj",
    )(x, a, a, a, inv_count, o_win, o_glb, w_pool_bd, pool_scale, w_out, g_post, mod)


def _ffn_tile(x, mod_ref, gpre_ref, gpost_ref, wg_ref, wu_ref, wd_ref, chunk):
    h = _rms(x, gpre_ref[...])
    h = (h * (1.0 + mod_ref[4:5, :]) + mod_ref[3:4, :]).astype(BF16)
    f = None
    for c in range(wg_ref.shape[1] // chunk):
        cols = slice(c * chunk, (c + 1) * chunk)
        g = jnp.dot(h, wg_ref[:, cols], preferred_element_type=F32)
        u = jnp.dot(h, wu_ref[:, cols], preferred_element_type=F32)
        act = ((g * jax.nn.sigmoid(g)) * u).astype(BF16)
        part = jnp.dot(act, wd_ref[cols, :], preferred_element_type=F32)
        f = part if f is None else f + part
    return x + mod_ref[5:6, :] * _rms(f, gpost_ref[...])


def _ffn_kernel(x_ref, mod_ref, gpre_ref, gpost_ref, wg_ref, wu_ref, wd_ref, o_ref, *, chunk):
    o_ref[...] = _ffn_tile(x_ref[...], mod_ref, gpre_ref, gpost_ref, wg_ref, wu_ref, wd_ref, chunk)


def _ffn(x, mod, g_pre, g_post, w_gate, w_up, w_down, tm, chunk):
    b, n, d = x.shape
    dff = w_gate.shape[1]
    row = lambda bi, i: (0, 0)
    tile = lambda bi, i: (bi, i, 0)
    return pl.pallas_call(
        functools.partial(_ffn_kernel, chunk=chunk),
        out_shape=jax.ShapeDtypeStruct((b, n, d), F32),
        grid=(b, n // tm),
        in_specs=[
            pl.BlockSpec((None, tm, d), tile),
            pl.BlockSpec((None, N_MOD, d), lambda bi, i: (bi, 0, 0)),
            pl.BlockSpec((1, d), row),
            pl.BlockSpec((1, d), row),
            _resident((d, dff), row),
            _resident((d, dff), row),
            _resident((dff, d), row),
        ],
        out_specs=pl.BlockSpec((None, tm, d), tile),
        compiler_params=_cparams("arbitrary", "arbitrary"),
        name="ffn",
    )(x, mod, g_pre, g_post, w_gate, w_up, w_down)


def _ffn_in_proj_kernel(x_ref, modf_ref, gpre_ref, gpost_ref, wg_ref, wu_ref, wd_ref,
                        modi_ref, gin_ref, win_ref, cos_ref, sin_ref, gq_ref, gk_ref, qfill_ref,
                        o_ref, a_ref, qw_ref, ktw_ref, vw_ref, qg_ref, ktg_ref, vg_ref, u_ref, *, chunk, st):
    tm = x_ref.shape[0]

    @pl.when(pl.program_id(0) == 0)
    def _():
        u_ref[...] = jnp.zeros(u_ref.shape, F32)

    _in_proj_epilogue(lambda sub: u_ref[sub * st:(sub + 1) * st, :], tm, st, cos_ref, sin_ref, gq_ref, gk_ref, qfill_ref,
                      a_ref, qw_ref, ktw_ref, vw_ref, qg_ref, ktg_ref, vg_ref)
    y = _ffn_tile(x_ref[...], modf_ref, gpre_ref, gpost_ref, wg_ref, wu_ref, wd_ref, chunk)
    o_ref[...] = y
    u_ref[...] = _project(y, modi_ref, gin_ref, win_ref)


def _ffn_in_proj(x, mod_f, g_pre_f, g_post_f, w_gate, w_up, w_down, mod_i, g_pre_i, w_in, cos, sin, gq, gk, qfill, tm, chunk):
    b, n, d = x.shape
    dff = w_gate.shape[1]
    nw = w_in.shape[1]
    tpb = n // tm
    last = b * tpb - 1
    cur = lambda s: (jnp.minimum(s, last) // tpb, jnp.minimum(s, last) % tpb)
    prev = lambda s: (jnp.maximum(s - 1, 0) // tpb, jnp.maximum(s - 1, 0) % tpb)
    out_shape, out_specs = _in_proj_outputs(b, n, tm, prev)
    row = lambda s: (0, 0)
    outs = pl.pallas_call(
        functools.partial(_ffn_in_proj_kernel, chunk=chunk, st=min(tm, 256)),
        out_shape=(jax.ShapeDtypeStruct((b, n, d), F32),) + out_shape,
        grid=(b * tpb + 1,),
        in_specs=[
            pl.BlockSpec((None, tm, d), lambda s: (cur(s)[0], cur(s)[1], 0)),
            pl.BlockSpec((None, N_MOD, d), lambda s: (cur(s)[0], 0, 0)),
            pl.BlockSpec((1, d), row),
            pl.BlockSpec((1, d), row),
            _resident((d, dff), row),
            _resident((d, dff), row),
            _resident((dff, d), row),
            pl.BlockSpec((None, N_MOD, d), lambda s: (cur(s)[0], 0, 0)),
            pl.BlockSpec((1, d), row),
            _resident((d, nw), row),
            pl.BlockSpec((tm, LANES), lambda s: (prev(s)[1], 0)),
            pl.BlockSpec((tm, LANES), lambda s: (prev(s)[1], 0)),
            pl.BlockSpec((1, LANES), row),
            pl.BlockSpec((1, LANES), row),
            pl.BlockSpec((1, LANES), row),
        ],
        out_specs=(pl.BlockSpec((None, tm, d), lambda s: (cur(s)[0], cur(s)[1], 0)),) + out_specs,
        scratch_shapes=[pltpu.VMEM((tm, nw), F32)],
        compiler_params=_cparams("arbitrary"),
        name="ffn_in_proj",
    )(x, mod_f, g_pre_f, g_post_f, w_gate, w_up, w_down, mod_i, g_pre_i, w_in, cos, sin, gq, gk, qfill)
    return outs[0], outs[1:]


def _rope_tables(n):
    rows = n // GRID_W
    row = jnp.broadcast_to(jnp.arange(rows)[:, None], (rows, GRID_W)).reshape(-1).astype(F32)
    col = jnp.broadcast_to(jnp.arange(GRID_W)[None, :], (rows, GRID_W)).reshape(-1).astype(F32)
    freq = ROPE_THETA ** (-jnp.arange(ROPE_FREQS, dtype=F32) / ROPE_FREQS)
    ar, ac = row[:, None] * freq, col[:, None] * freq
    cos = jnp.concatenate([jnp.cos(ar), jnp.cos(ar), jnp.cos(ac), jnp.cos(ac)], axis=1)
    sin = jnp.concatenate([-jnp.sin(ar), jnp.sin(ar), -jnp.sin(ac), jnp.sin(ac)], axis=1)
    return jnp.tile(cos, (1, LANES // HEAD_DIM)), jnp.tile(sin, (1, LANES // HEAD_DIM))


def _shift_row(m):
    pieces, rest = [], m.astype(F32)
    for _ in range(N_SHIFT):
        part = rest.astype(BF16).astype(F32)
        pieces.append(-part)
        rest = rest - part
    return jnp.zeros((1, LANES), F32).at[0, HEAD_DIM:HEAD_DIM + N_SHIFT].set(jnp.stack(pieces))


def _pool_inv_counts(n):
    t = jnp.arange(n)
    cols = []
    for w in POOL_WINDOWS:
        hi = jnp.clip(t + (w - 1 - w // 2), 0, n - 1)
        lo = jnp.clip(t - w // 2, 0, n - 1)
        cols.append(jnp.broadcast_to((1.0 / (hi - lo + 1).astype(F32))[:, None], (n, HEAD_DIM)))
    return jnp.concatenate(cols, axis=1)


def _tile(n, pref):
    t = min(n, pref)
    assert n % t == 0, (n, t)
    return t


def kernel(x, c, ctx, c_ctx, w_mod, b_mod, g_pre_mix, g_post_mix, g_pre_ffn, g_post_ffn, w_in, w_pool, pool_scale, win_sink, g_qnorm, g_knorm, w_out, w_gate, w_up, w_down):
    b, n, d = x.shape
    lc = ctx.shape[1]
    depth = w_mod.shape[0]
    assert d == 1024 and w_in.shape[2] == 1536 and n % GRID_W == 0

    cos, sin = _rope_tables(n)
    cos_c, sin_c = jnp.ones((lc, LANES), F32), jnp.zeros((lc, LANES), F32)

    mod_rows = -(-(b + 1) // 8) * 8
    cc = jnp.concatenate([c, c_ctx[None, :], jnp.zeros((mod_rows - b - 1, d), F32)], axis=0)
    mod_all = _modulation(cc, w_mod, b_mod)

    tm = _tile(n, 512)
    tq_win = _tile(n, 1024)
    tq_glb = _tile(n, 512)
    tk_glb = n // (N_KV_HEADS * (tq_glb // WINDOW))
    assert tk_glb % LANES == 0
    rb_glb = 256
    ffn_chunk = 256

    def layer(l):
        bound = ATTN_SCALE * HEAD_DIM * jnp.max(jnp.abs(g_qnorm[l])) * jnp.max(jnp.abs(g_knorm[l]))
        return dict(
            m=mod_all[l, :b].reshape(b, N_MOD, d),
            mc=jnp.broadcast_to(mod_all[l, b].reshape(1, N_MOD, d), (b, N_MOD, d)),
            w_in=w_in[l].astype(BF16),
            gq=jnp.tile(g_qnorm[l], LANES // HEAD_DIM)[None, :],
            gk=jnp.tile(g_knorm[l], LANES // HEAD_DIM)[None, :],
            g_pre=g_pre_mix[l][None, :],
            bound=bound,
            qfill=jnp.where(bound <= SAFE_SHIFT, _shift_row(bound), 0.0),
        )

    no_fill = jnp.zeros((1, LANES), F32)
    icnt, icnt_c = _pool_inv_counts(n), _pool_inv_counts(lc)
    xc = ctx
    p = layer(0)
    proj = _in_proj(x, p["m"], p["g_pre"], p["w_in"], cos, sin, p["gq"], p["gk"], p["qfill"], tm)
    for l in range(depth):
        last = l == depth - 1
        m, mc, gq, gk = p["m"], p["mc"], p["gq"], p["gk"]
        w_out_l = w_out[l].astype(BF16)
        w_gate_l, w_up_l, w_down_l = w_gate[l].astype(BF16), w_up[l].astype(BF16), w_down[l].astype(BF16)
        w_pool_bd = jax.scipy.linalg.block_diag(*[w_pool[l, g] for g in range(len(POOL_WINDOWS))]).astype(BF16)
        pscale = pool_scale[l][None, :]
        sink = jnp.zeros((8, LANES), F32).at[:N_Q_HEADS].set(jnp.broadcast_to(win_sink[l][:, None], (N_Q_HEADS, LANES)))
        g_post = g_post_mix[l][None, :]
        g_pre_f, g_post_f = g_pre_ffn[l][None, :], g_post_ffn[l][None, :]

        a, qw, ktw, vw, qg, ktg, vg = proj
        ac, qwc, ktwc, vwc, qgc, ktgc, vgc = _in_proj(xc, mc, p["g_pre"], p["w_in"], cos_c, sin_c, gq, gk, no_fill, lc)

        def attn_bounded(*ops):
            o_g, o_w = _attn(*ops, tq=tq_glb, tk=tk_glb, rb=rb_glb)
            return o_w, o_g

        def attn_exact(qg, ktg, vg, ktgc, vgc, qw, ktw, vw, ktwc, vwc, sink):
            return (_window_attn(qw, ktw, vw, ktwc, vwc, sink, tq_win),
                    _global_attn(qg, ktg, vg, ktgc, vgc, tq=tq_glb, tk=tk_glb, rb=rb_glb))

        o_win, o_glb = lax.cond(p["bound"] <= SAFE_SHIFT, attn_bounded, attn_exact,
                                qg, ktg, vg, ktgc, vgc, qw, ktw, vw, ktwc, vwc, sink)
        x_new = _out_proj(x, a, icnt, o_win, o_glb, w_pool_bd, pscale, w_out_l, g_post, m, tm)

        if last:
            x = _ffn(x_new, m, g_pre_f, g_post_f, w_gate_l, w_up_l, w_down_l, tm, ffn_chunk)
        else:
            oc_win = _ctx_attn(qwc, ktwc, vwc, sink)
            oc_glb = _ctx_attn(qgc, ktgc, vgc, None)
            xc = _out_proj(xc, ac, icnt_c, oc_win, oc_glb, w_pool_bd, pscale, w_out_l, g_post, mc, lc)
            xc = _ffn(xc, mc, g_pre_f, g_post_f, w_gate_l, w_up_l, w_down_l, lc, ffn_chunk)
            p = layer(l + 1)
            x, proj = _ffn_in_proj(x_new, m, g_pre_f, g_post_f, w_gate_l, w_up_l, w_down_l,
                                   p["m"], p["g_pre"], p["w_in"], cos, sin, p["gq"], p["gk"], p["qfill"], tm, ffn_chunk)
    return x
```

```python
import functools

import jax
import jax.numpy as jnp
from jax import lax
from jax.experimental import pallas as pl
from jax.experimental.pallas import tpu as pltpu

F32 = jnp.float32
BF16 = jnp.bfloat16

HEAD_DIM = 64
GRID_W = 64
ROPE_FREQS = HEAD_DIM // 4
ROPE_THETA = 10000.0
NORM_EPS = 1e-6
NEG_INF = -1e30
ATTN_SCALE = HEAD_DIM ** -0.5
WINDOW = 128
POOL_WINDOWS = (2, 4, 8, 16)
POOL_HALO = 16
N_Q_HEADS = 6
N_KV_HEADS = 2
KV_GROUP = N_Q_HEADS // N_KV_HEADS
N_MOD = 6
POOL_WIDTH = len(POOL_WINDOWS) * HEAD_DIM
Q_WIDTH = N_Q_HEADS * HEAD_DIM
KV_WIDTH = N_KV_HEADS * HEAD_DIM
COL_QW = POOL_WIDTH
COL_KW = COL_QW + Q_WIDTH
COL_VW = COL_KW + KV_WIDTH
COL_QG = COL_VW + KV_WIDTH
COL_KG = COL_QG + Q_WIDTH
COL_VG = COL_KG + KV_WIDTH
IN_WIDTH = COL_VG + KV_WIDTH
N_SHIFT = 3
SAFE_SHIFT = 30.0

LANES = 128
VMEM_LIMIT = 56 * 1024 * 1024


def _cparams(*sem):
    return pltpu.CompilerParams(dimension_semantics=sem, vmem_limit_bytes=VMEM_LIMIT)


def _resident(block_shape, index_map):
    return pl.BlockSpec(block_shape, index_map, pipeline_mode=pl.Buffered(1))


def _rms(x, g):
    ms = jnp.mean(x * x, axis=-1, keepdims=True)
    return (x * lax.rsqrt(ms + NORM_EPS)) * g


def _mod_kernel(c_ref, w_ref, b_ref, o_ref):
    c = c_ref[...]
    s = c * jax.nn.sigmoid(c)
    o_ref[...] = jnp.dot(s.astype(BF16), w_ref[...].astype(BF16), preferred_element_type=F32) + b_ref[...]


def _modulation(cc, w_mod, b_mod):
    depth, d, dm = w_mod.shape
    rows = cc.shape[0]
    tn = dm // 4
    return pl.pallas_call(
        _mod_kernel,
        out_shape=jax.ShapeDtypeStruct((depth, rows, dm), F32),
        grid=(depth, dm // tn),
        in_specs=[
            pl.BlockSpec((rows, d), lambda l, j: (0, 0)),
            pl.BlockSpec((None, d, tn), lambda l, j: (l, 0, j)),
            pl.BlockSpec((None, 1, tn), lambda l, j: (l, 0, j)),
        ],
        out_specs=pl.BlockSpec((None, rows, tn), lambda l, j: (l, 0, j)),
        compiler_params=_cparams("arbitrary", "arbitrary"),
        name="modulation",
    )(cc, w_mod, b_mod.reshape(depth, 1, dm))


def _in_proj_epilogue(u_of, n_rows, st, cos_ref, sin_ref, gq_ref, gk_ref, qfill_ref,
                      a_ref, qw_ref, ktw_ref, vw_ref, qg_ref, ktg_ref, vg_ref):
    lane = lax.broadcasted_iota(jnp.int32, (st, LANES), 1)
    lo = lane < HEAD_DIM
    first = (lane & ROPE_FREQS) == 0
    shift_rows = (lax.broadcasted_iota(jnp.int32, (HEAD_DIM, LANES), 0) < N_SHIFT).astype(BF16)
    sum_lane = (lane == HEAD_DIM).astype(F32)
    gq, gk = gq_ref[...], gk_ref[...]
    qfill = qfill_ref[...]

    def head_norm(t, g):
        sq = t * t
        s_lo = jnp.sum(jnp.where(lo, sq, 0.0), axis=-1, keepdims=True)
        s_hi = jnp.sum(jnp.where(lo, 0.0, sq), axis=-1, keepdims=True)
        ms = jnp.where(lo, s_lo, s_hi) * (1.0 / HEAD_DIM)
        return (t * lax.rsqrt(ms + NORM_EPS)) * g

    for sub in range(n_rows // st):
        rs = slice(sub * st, (sub + 1) * st)
        u = u_of(sub)
        cos = cos_ref[rs, :]
        sin = sin_ref[rs, :]

        def rope(t):
            partner = jnp.where(first, pltpu.roll(t, LANES - ROPE_FREQS, 1), pltpu.roll(t, ROPE_FREQS, 1))
            return t * cos + partner * sin

        def store_branch(q_chunks, k, v, fill, q_ref, kt_ref, v_ref):
            for hd in range(N_Q_HEADS):
                t = q_chunks[hd // 2]
                if hd % 2:
                    t = pltpu.roll(t, HEAD_DIM, 1)
                q_ref[hd, rs, :] = jnp.where(lo, t, fill).astype(BF16)
            for r in range(st // LANES):
                kb = sub * (st // LANES) + r
                kt = jnp.transpose(k[r * LANES:(r + 1) * LANES, :]).astype(BF16)
                for kv in range(N_KV_HEADS):
                    kt_ref[kv, kb, 0:HEAD_DIM, :] = kt[kv * HEAD_DIM:(kv + 1) * HEAD_DIM, :]
                    kt_ref[kv, kb, HEAD_DIM:LANES, :] = shift_rows
            v_ref[0, rs, :] = jnp.where(lo, v, sum_lane).astype(BF16)
            v_ref[1, rs, :] = jnp.where(lo, pltpu.roll(v, HEAD_DIM, 1), sum_lane).astype(BF16)

        def cols(start, width):
            return u[:, start:start + width]

        a_ref[rs, :] = cols(0, POOL_WIDTH)
        store_branch([rope(cols(COL_QW + LANES * c, LANES)) * ATTN_SCALE for c in range(Q_WIDTH // LANES)],
                     rope(cols(COL_KW, KV_WIDTH)), cols(COL_VW, KV_WIDTH), 0.0, qw_ref, ktw_ref, vw_ref)
        store_branch([rope(head_norm(cols(COL_QG + LANES * c, LANES), gq)) * ATTN_SCALE for c in range(Q_WIDTH // LANES)],
                     rope(head_norm(cols(COL_KG, KV_WIDTH), gk)), cols(COL_VG, KV_WIDTH), qfill, qg_ref, ktg_ref, vg_ref)


def _project(x, mod_ref, g_ref, w_ref):
    h = _rms(x, g_ref[...])
    h = h * (1.0 + mod_ref[1:2, :]) + mod_ref[0:1, :]
    return jnp.dot(h.astype(BF16), w_ref[...], preferred_element_type=F32)


def _in_proj_kernel(x_ref, mod_ref, g_ref, w_ref, cos_ref, sin_ref, gq_ref, gk_ref, qfill_ref, *out_refs, st):
    tm = x_ref.shape[0]
    us = [_project(x_ref[sub * st:(sub + 1) * st, :], mod_ref, g_ref, w_ref) for sub in range(tm // st)]
    _in_proj_epilogue(lambda sub: us[sub], tm, st, cos_ref, sin_ref, gq_ref, gk_ref, qfill_ref, *out_refs)


def _in_proj_outputs(b, n, tm, tile_of):
    qshape = jax.ShapeDtypeStruct((b, N_Q_HEADS, n, LANES), BF16)
    ktshape = jax.ShapeDtypeStruct((b, N_KV_HEADS, n // LANES, LANES, LANES), BF16)
    vshape = jax.ShapeDtypeStruct((b, N_KV_HEADS, n, LANES), BF16)

    def spec(block, where):
        return pl.BlockSpec(block, lambda *g: where(*tile_of(*g)))

    aspec = spec((None, tm, POOL_WIDTH), lambda bi, i: (bi, i, 0))
    qspec = spec((None, N_Q_HEADS, tm, LANES), lambda bi, i: (bi, 0, i, 0))
    ktspec = spec((None, N_KV_HEADS, tm // LANES, LANES, LANES), lambda bi, i: (bi, 0, i, 0, 0))
    vspec = spec((None, N_KV_HEADS, tm, LANES), lambda bi, i: (bi, 0, i, 0))
    shapes = (jax.ShapeDtypeStruct((b, n, POOL_WIDTH), F32), qshape, ktshape, vshape, qshape, ktshape, vshape)
    return shapes, (aspec, qspec, ktspec, vspec, qspec, ktspec, vspec)


def _in_proj_carried_kernel(x_ref, mod_ref, g_ref, w_ref, cos_ref, sin_ref, gq_ref, gk_ref, qfill_ref, *refs, st):
    out_refs, u_ref = refs[:-1], refs[-1]

    @pl.when(pl.program_id(0) == 0)
    def _():
        u_ref[...] = jnp.zeros(u_ref.shape, F32)

    _in_proj_epilogue(lambda sub: u_ref[sub * st:(sub + 1) * st, :], x_ref.shape[0], st, cos_ref, sin_ref,
                      gq_ref, gk_ref, qfill_ref, *out_refs)
    u_ref[...] = _project(x_ref[...], mod_ref, g_ref, w_ref)


def _in_proj(x, mod, g_pre, w_in, cos, sin, gq, gk, qfill, tm, carried=False):
    b, n, d = x.shape
    nw = w_in.shape[1]
    tpb = n // tm
    last = b * tpb - 1
    if carried:
        grid = (b * tpb + 1,)
        cur = lambda s: (jnp.minimum(s, last) // tpb, jnp.minimum(s, last) % tpb)
        prev = lambda s: (jnp.maximum(s - 1, 0) // tpb, jnp.maximum(s - 1, 0) % tpb)
        kern, scratch, sem = _in_proj_carried_kernel, [pltpu.VMEM((tm, nw), F32)], ("arbitrary",)
    else:
        grid = (b, tpb)
        cur = prev = lambda bi, i: (bi, i)
        kern, scratch, sem = _in_proj_kernel, [], ("arbitrary", "arbitrary")
    out_shape, out_specs = _in_proj_outputs(b, n, tm, prev)
    row = lambda *g: (0, 0)
    return pl.pallas_call(
        functools.partial(kern, st=min(tm, 256)),
        out_shape=out_shape,
        grid=grid,
        in_specs=[
            pl.BlockSpec((None, tm, d), lambda *g: (cur(*g)[0], cur(*g)[1], 0)),
            pl.BlockSpec((None, N_MOD, d), lambda *g: (cur(*g)[0], 0, 0)),
            pl.BlockSpec((1, d), row),
            _resident((d, nw), row),
            pl.BlockSpec((tm, LANES), lambda *g: (prev(*g)[1], 0)),
            pl.BlockSpec((tm, LANES), lambda *g: (prev(*g)[1], 0)),
            pl.BlockSpec((1, LANES), row),
            pl.BlockSpec((1, LANES), row),
            pl.BlockSpec((1, LANES), row),
        ],
        out_specs=out_specs,
        scratch_shapes=scratch,
        compiler_params=_cparams(*sem),
        name="in_proj",
    )(x, mod, g_pre, w_in, cos, sin, gq, gk, qfill)


def _merge_heads(outs, lo):
    return [jnp.where(lo, outs[2 * c], pltpu.roll(outs[2 * c + 1], HEAD_DIM, 1)) for c in range(N_Q_HEADS // 2)]


def _key_tile(kt_ref, first_block, n_blocks):
    return jnp.concatenate([kt_ref[first_block + j] for j in range(n_blocks)], axis=1)


def _stacked_sink(sink_ref, kv, rows_per_head):
    return jnp.concatenate([jnp.broadcast_to(sink_ref[hd:hd + 1, 0:1], (rows_per_head, 1))
                            for hd in range(kv * KV_GROUP, (kv + 1) * KV_GROUP)], axis=0)


def _window_units(q_ref, kt_ref, v_ref, ktc_ref, vc_ref, sink_ref, first_blk):
    nkb = kt_ref.shape[1]
    sub = WINDOW
    rows = KV_GROUP * sub
    r_in = lax.broadcasted_iota(jnp.int32, (rows, LANES), 0) & (sub - 1)
    c_in = lax.broadcasted_iota(jnp.int32, (rows, LANES), 1)
    ctx = [(_key_tile(ktc_ref.at[kv], 0, ktc_ref.shape[1]), vc_ref[kv], _stacked_sink(sink_ref, kv, sub))
           for kv in range(N_KV_HEADS)]

    def blocks(j):
        blk = first_blk + j
        return blk, jnp.maximum(blk - 1, 0), jnp.minimum(blk + 1, nkb - 1)

    def logits(j, kv):
        blk, prev_blk, next_blk = blocks(j)
        q = q_ref[kv * KV_GROUP:(kv + 1) * KV_GROUP, pl.ds(pl.multiple_of(j * sub, sub), sub), :].reshape(rows, LANES)
        kt = jnp.concatenate([kt_ref[kv, prev_blk], kt_ref[kv, blk], kt_ref[kv, next_blk]], axis=1)
        s = jnp.dot(q, kt, preferred_element_type=F32)
        sc = jnp.dot(q, ctx[kv][0], preferred_element_type=F32)
        return s, sc

    def attend(j, kv, s, sc):
        blk, prev_blk, next_blk = blocks(j)
        _, vc, sk = ctx[kv]
        v = jnp.concatenate([v_ref[kv, pl.ds(pl.multiple_of(kb * sub, sub), sub), :]
                             for kb in (prev_blk, blk, next_blk)], axis=0)
        s_prev = jnp.where(jnp.logical_and(c_in >= r_in, blk > 0), s[:, 0:sub], NEG_INF)
        s_own = s[:, sub:2 * sub]
        s_next = jnp.where(jnp.logical_and(c_in <= r_in, blk < nkb - 1), s[:, 2 * sub:3 * sub], NEG_INF)
        mx = jnp.maximum(jnp.maximum(s_prev, s_own), s_next)
        for t in range(sc.shape[1] // LANES):
            mx = jnp.maximum(mx, sc[:, t * LANES:(t + 1) * LANES])
        m = jnp.maximum(jnp.max(mx, axis=-1, keepdims=True), sk)
        p = jnp.concatenate([jnp.exp(s_prev - m), jnp.exp(s_own - m), jnp.exp(s_next - m)], axis=1)
        pv = jnp.dot(p.astype(BF16), v, preferred_element_type=F32)
        pv = pv + jnp.dot(jnp.exp(sc - m).astype(BF16), vc, preferred_element_type=F32)
        l = pv[:, HEAD_DIM:HEAD_DIM + 1] + jnp.exp(sk - m)
        o = pv / l
        return [o[g * sub:(g + 1) * sub] for g in range(KV_GROUP)]

    return logits, attend


def _store_heads(o_ref, row0, outs):
    nrows = outs[0].shape[0]
    lo = lax.broadcasted_iota(jnp.int32, (nrows, LANES), 1) < HEAD_DIM
    for c, chunk in enumerate(_merge_heads(outs, lo)):
        o_ref[pl.ds(row0, nrows), c * LANES:(c + 1) * LANES] = chunk.astype(BF16)


def _window_attn_kernel(q_ref, kt_ref, v_ref, ktc_ref, vc_ref, sink_ref, o_ref):
    tq = q_ref.shape[1]
    sub = WINDOW
    logits, attend = _window_units(q_ref, kt_ref, v_ref, ktc_ref, vc_ref, sink_ref, pl.program_id(1) * (tq // sub))
    units = [(j, kv) for j in range(tq // sub) for kv in range(N_KV_HEADS)]
    pending = logits(*units[0])
    outs = []
    for idx, (j, kv) in enumerate(units):
        ahead = logits(*units[idx + 1]) if idx + 1 < len(units) else None
        outs += attend(j, kv, *pending)
        pending = ahead
        if kv == N_KV_HEADS - 1:
            _store_heads(o_ref, j * sub, outs)
            outs = []


def _window_attn(q, kt, v, ktc, vc, sink, tq):
    b, _, n, _ = q.shape
    lc = vc.shape[2]
    assert tq % WINDOW == 0 and WINDOW == LANES
    batch4 = lambda bi, i: (bi, 0, 0, 0)
    batch5 = lambda bi, i: (bi, 0, 0, 0, 0)
    return pl.pallas_call(
        _window_attn_kernel,
        out_shape=jax.ShapeDtypeStruct((b, n, N_Q_HEADS * HEAD_DIM), BF16),
        grid=(b, n // tq),
        in_specs=[
            pl.BlockSpec((None, N_Q_HEADS, tq, LANES), lambda bi, i: (bi, 0, i, 0)),
            pl.BlockSpec((None, N_KV_HEADS, n // LANES, LANES, LANES), batch5),
            pl.BlockSpec((None, N_KV_HEADS, n, LANES), batch4),
            pl.BlockSpec((None, N_KV_HEADS, lc // LANES, LANES, LANES), batch5),
            pl.BlockSpec((None, N_KV_HEADS, lc, LANES), batch4),
            pl.BlockSpec(sink.shape, lambda bi, i: (0, 0)),
        ],
        out_specs=pl.BlockSpec((None, tq, N_Q_HEADS * HEAD_DIM), lambda bi, i: (bi, i, 0)),
        compiler_params=_cparams("arbitrary", "arbitrary"),
        name="window_attn",
    )(q, kt, v, ktc, vc, sink)


def _ctx_attn_kernel(*refs, has_sink):
    if has_sink:
        q_ref, kt_ref, v_ref, sink_ref, o_ref = refs
    else:
        q_ref, kt_ref, v_ref, o_ref = refs
    tq = q_ref.shape[1]
    rows = KV_GROUP * tq
    outs = []
    for kv in range(N_KV_HEADS):
        q = q_ref[kv * KV_GROUP:(kv + 1) * KV_GROUP].reshape(rows, LANES)
        s = jnp.dot(q, _key_tile(kt_ref.at[kv], 0, kt_ref.shape[1]), preferred_element_type=F32)
        m = jnp.max(s, axis=-1, keepdims=True)
        if has_sink:
            sk = _stacked_sink(sink_ref, kv, tq)
            m = jnp.maximum(m, sk)
        pv = jnp.dot(jnp.exp(s - m).astype(BF16), v_ref[kv], preferred_element_type=F32)
        l = pv[:, HEAD_DIM:HEAD_DIM + 1]
        if has_sink:
            l = l + jnp.exp(sk - m)
        o = pv / l
        outs += [o[g * tq:(g + 1) * tq] for g in range(KV_GROUP)]
    _store_heads(o_ref, 0, outs)


def _ctx_attn(q, kt, v, sink):
    b, _, n, _ = q.shape
    args = [q, kt, v]
    in_specs = [pl.BlockSpec((None,) + a.shape[1:], lambda bi, nd=a.ndim: (bi,) + (0,) * (nd - 1)) for a in args]
    if sink is not None:
        args.append(sink)
        in_specs.append(pl.BlockSpec(sink.shape, lambda bi: (0, 0)))
    return pl.pallas_call(
        functools.partial(_ctx_attn_kernel, has_sink=sink is not None),
        out_shape=jax.ShapeDtypeStruct((b, n, N_Q_HEADS * HEAD_DIM), BF16),
        grid=(b,),
        in_specs=in_specs,
        out_specs=pl.BlockSpec((None, n, N_Q_HEADS * HEAD_DIM), lambda bi: (bi, 0, 0)),
        compiler_params=_cparams("arbitrary"),
        name="ctx_attn",
    )(*args)


def _global_attn_kernel(q_ref, kt_ref, v_ref, ktc_ref, vc_ref, o_ref, qs_ref, acc_ref, mx_ref, *, tk, rb):
    tq = q_ref.shape[1]
    rows = KV_GROUP * tq
    kb_per_step = tk // LANES
    n_steps = kt_ref.shape[1] // kb_per_step
    lane = lax.broadcasted_iota(jnp.int32, (rows, LANES), 1)

    def key_steps(fn):
        def step(kc, carry):
            for kv in range(N_KV_HEADS):
                fn(kv, _key_tile(kt_ref.at[kv], kc * kb_per_step, kb_per_step),
                   v_ref[kv, pl.ds(pl.multiple_of(kc * tk, tk), tk), :])
            return carry
        lax.fori_loop(0, n_steps, step, 0)
        for kv in range(N_KV_HEADS):
            fn(kv, _key_tile(ktc_ref.at[kv], 0, ktc_ref.shape[1]), vc_ref[kv])

    def q_rows(kv):
        return q_ref[kv * KV_GROUP:(kv + 1) * KV_GROUP].reshape(rows, LANES)

    mx_ref[...] = jnp.full(mx_ref.shape, NEG_INF, F32)

    def track_max(kv, kt, v):
        s = jnp.dot(q_rows(kv), kt, preferred_element_type=F32)
        mx = mx_ref[kv]
        for j in range(kt.shape[1] // LANES):
            mx = jnp.maximum(mx, s[:, j * LANES:(j + 1) * LANES])
        mx_ref[kv] = mx

    key_steps(track_max)

    for kv in range(N_KV_HEADS):
        qs = q_rows(kv).astype(F32)
        rest = jnp.max(mx_ref[kv], axis=-1, keepdims=True)
        for piece in range(N_SHIFT):
            part = rest.astype(BF16).astype(F32)
            qs = jnp.where(lane == HEAD_DIM + piece, -part, qs)
            rest = rest - part
        qs_ref[kv] = qs.astype(BF16)

    acc_ref[...] = jnp.zeros(acc_ref.shape, F32)

    def accumulate(kv, kt, v):
        for r in range(rows // rb):
            rsl = slice(r * rb, (r + 1) * rb)
            s = jnp.dot(qs_ref[kv, rsl, :], kt, preferred_element_type=F32)
            p = jnp.exp(s).astype(BF16)
            acc_ref[kv, rsl, :] += jnp.dot(p, v, preferred_element_type=F32)

    key_steps(accumulate)

    outs = []
    for hd in range(N_Q_HEADS):
        a = acc_ref[hd // KV_GROUP, (hd % KV_GROUP) * tq:(hd % KV_GROUP + 1) * tq, :]
        outs.append(a / a[:, HEAD_DIM:HEAD_DIM + 1])
    _store_heads(o_ref, 0, outs)


def _global_attn(q, kt, v, ktc, vc, *, tq, tk, rb):
    b, _, n, _ = q.shape
    lc = vc.shape[2]
    rows = KV_GROUP * tq
    batch4 = lambda bi, i: (bi, 0, 0, 0)
    batch5 = lambda bi, i: (bi, 0, 0, 0, 0)
    return pl.pallas_call(
        functools.partial(_global_attn_kernel, tk=tk, rb=rb),
        out_shape=jax.ShapeDtypeStruct((b, n, N_Q_HEADS * HEAD_DIM), BF16),
        grid=(b, n // tq),
        in_specs=[
            pl.BlockSpec((None, N_Q_HEADS, tq, LANES), lambda bi, i: (bi, 0, i, 0)),
            pl.BlockSpec((None, N_KV_HEADS, n // LANES, LANES, LANES), batch5),
            pl.BlockSpec((None, N_KV_HEADS, n, LANES), batch4),
            pl.BlockSpec((None, N_KV_HEADS, lc // LANES, LANES, LANES), batch5),
            pl.BlockSpec((None, N_KV_HEADS, lc, LANES), batch4),
        ],
        out_specs=pl.BlockSpec((None, tq, N_Q_HEADS * HEAD_DIM), lambda bi, i: (bi, i, 0)),
        scratch_shapes=[pltpu.VMEM((N_KV_HEADS, rows, LANES), BF16), pltpu.VMEM((N_KV_HEADS, rows, LANES), F32),
                        pltpu.VMEM((N_KV_HEADS, rows, LANES), F32)],
        compiler_params=_cparams("arbitrary", "arbitrary"),
        name="global_attn_exact_max",
    )(q, kt, v, ktc, vc)


def _attn_kernel(qg_ref, ktg_ref, vg_ref, ktgc_ref, vgc_ref, qw_ref, ktw_ref, vw_ref, ktwc_ref, vwc_ref, sink_ref,
                 og_ref, ow_ref, acc_ref, *, tk, rb):
    tq = qg_ref.shape[1]
    rows = KV_GROUP * tq
    sub = WINDOW
    n_sub = tq // sub
    kb_per_step = tk // LANES
    steps_per_sub = ktg_ref.shape[1] // kb_per_step // n_sub
    assert steps_per_sub * n_sub * kb_per_step == ktg_ref.shape[1] and tq % rb == 0

    def global_keys(kv, kt, v, assign):
        for r in range(rows // rb):
            hd, off = kv * KV_GROUP + (r * rb) // tq, (r * rb) % tq
            s = jnp.dot(qg_ref[hd, off:off + rb, :], kt, preferred_element_type=F32)
            pv = jnp.dot(jnp.exp(s).astype(BF16), v, preferred_element_type=F32)
            if assign:
                acc_ref[kv, r * rb:(r + 1) * rb, :] = pv
            else:
                acc_ref[kv, r * rb:(r + 1) * rb, :] += pv

    def global_step(kc):
        for kv in range(N_KV_HEADS):
            global_keys(kv, _key_tile(ktg_ref.at[kv], kc * kb_per_step, kb_per_step),
                        vg_ref[kv, pl.ds(pl.multiple_of(kc * tk, tk), tk), :], False)

    for kv in range(N_KV_HEADS):
        global_keys(kv, _key_tile(ktgc_ref.at[kv], 0, ktgc_ref.shape[1]), vgc_ref[kv], True)

    logits, attend = _window_units(qw_ref, ktw_ref, vw_ref, ktwc_ref, vwc_ref, sink_ref, pl.program_id(1) * n_sub)

    def body(j, carry):
        pending = logits(j, 0)
        outs = []
        for kv in range(N_KV_HEADS):
            for t in range(steps_per_sub // N_KV_HEADS):
                global_step(j * steps_per_sub + kv * (steps_per_sub // N_KV_HEADS) + t)
            ahead = logits(j, kv + 1) if kv + 1 < N_KV_HEADS else None
            outs += attend(j, kv, *pending)
            pending = ahead
        _store_heads(ow_ref, pl.multiple_of(j * sub, sub), outs)
        return carry

    assert steps_per_sub % N_KV_HEADS == 0
    lax.fori_loop(0, n_sub, body, 0)

    outs = []
    for hd in range(N_Q_HEADS):
        a = acc_ref[hd // KV_GROUP, (hd % KV_GROUP) * tq:(hd % KV_GROUP + 1) * tq, :]
        outs.append(a / a[:, HEAD_DIM:HEAD_DIM + 1])
    _store_heads(og_ref, 0, outs)


def _attn(qg, ktg, vg, ktgc, vgc, qw, ktw, vw, ktwc, vwc, sink, *, tq, tk, rb):
    b, _, n, _ = qg.shape
    lc = vgc.shape[2]
    batch4 = lambda bi, i: (bi, 0, 0, 0)
    batch5 = lambda bi, i: (bi, 0, 0, 0, 0)
    branch_specs = [
        pl.BlockSpec((None, N_Q_HEADS, tq, LANES), lambda bi, i: (bi, 0, i, 0)),
        _resident((None, N_KV_HEADS, n // LANES, LANES, LANES), batch5),
        _resident((None, N_KV_HEADS, n, LANES), batch4),
        _resident((None, N_KV_HEADS, lc // LANES, LANES, LANES), batch5),
        _resident((None, N_KV_HEADS, lc, LANES), batch4),
    ]
    oshape = jax.ShapeDtypeStruct((b, n, N_Q_HEADS * HEAD_DIM), BF16)
    ospec = pl.BlockSpec((None, tq, N_Q_HEADS * HEAD_DIM), lambda bi, i: (bi, i, 0))
    return pl.pallas_call(
        functools.partial(_attn_kernel, tk=tk, rb=rb),
        out_shape=(oshape, oshape),
        grid=(b, n // tq),
        in_specs=branch_specs + branch_specs + [pl.BlockSpec(sink.shape, lambda bi, i: (0, 0))],
        out_specs=(ospec, ospec),
        scratch_shapes=[pltpu.VMEM((N_KV_HEADS, KV_GROUP * tq, LANES), F32)],
        compiler_params=_cparams("arbitrary", "arbitrary"),
        name="attn",
    )(qg, ktg, vg, ktgc, vgc, qw, ktw, vw, ktwc, vwc, sink)


def _out_proj_tile(has_prev, has_next, x_ref, a_ref, ap_ref, an_ref, icnt_ref, ow_ref, og_ref, wpool_ref, pscale_ref,
                   wo_ref, g_ref, mod_ref, ext_ref, s2_ref, s4_ref, s8_ref):
    tm = x_ref.shape[0]
    ext_ref[0:POOL_HALO, :] = jnp.where(has_prev, ap_ref[...], 0.0)
    ext_ref[POOL_HALO:POOL_HALO + tm, :] = a_ref[...]
    ext_ref[POOL_HALO + tm:2 * POOL_HALO + tm, :] = jnp.where(has_next, an_ref[...], 0.0)

    pw = a_ref.shape[1]
    aw = ow_ref.shape[1]
    y = jnp.dot(ow_ref[...], wo_ref[pw:pw + aw, :], preferred_element_type=F32)
    y = y + jnp.dot(og_ref[...], wo_ref[pw + aw:pw + 2 * aw, :], preferred_element_type=F32)

    lo = lax.broadcasted_iota(jnp.int32, (tm, LANES), 1) < HEAD_DIM

    pad = POOL_HALO // 2
    ext_ref[2 * POOL_HALO + tm:2 * POOL_HALO + tm + pad, :] = jnp.zeros((pad, ext_ref.shape[1]), F32)
    r2, r4, r8 = tm + 2 * POOL_HALO, tm + 2 * POOL_HALO - pad, tm + POOL_HALO
    s2_ref[0:r2, :] = ext_ref[0:r2, :] + ext_ref[1:r2 + 1, :]
    s4_ref[0:r4, :] = s2_ref[0:r4, :] + s2_ref[2:r4 + 2, :]
    s8_ref[0:r8, :] = s4_ref[0:r8, LANES:2 * LANES] + s4_ref[4:r8 + 4, LANES:2 * LANES]

    def window_sum(w, c):
        first = POOL_HALO - w // 2
        cols = slice(c * LANES, (c + 1) * LANES)
        if w == 2:
            return s2_ref[first:first + tm, cols]
        if w == 4:
            return s4_ref[first:first + tm, cols]
        if w == 8:
            return s8_ref[first:first + tm, :]
        return s8_ref[first:first + tm, :] + s8_ref[first + 8:first + 8 + tm, :]

    feats = []
    for c in range(2):
        w_a, w_b = POOL_WINDOWS[2 * c], POOL_WINDOWS[2 * c + 1]
        x0 = ext_ref[POOL_HALO:POOL_HALO + tm, c * LANES:(c + 1) * LANES]
        mean = jnp.where(lo, window_sum(w_a, c), window_sum(w_b, c)) * icnt_ref[:, c * LANES:(c + 1) * LANES]
        feats.append(mean - x0)
    feats = jnp.concatenate(feats, axis=1).astype(BF16)
    y_pool = jnp.dot(feats, wpool_ref[...], preferred_element_type=F32) * pscale_ref[...]

    y = y + jnp.dot(y_pool.astype(BF16), wo_ref[0:pw, :], preferred_element_type=F32)
    return x_ref[...] + mod_ref[2:3, :] * _rms(y, g_ref[...])


N_OUT_PROJ_IN = 12


def _out_proj_kernel(*refs):
    ins, o_ref, scratch = refs[:N_OUT_PROJ_IN], refs[N_OUT_PROJ_IN], refs[N_OUT_PROJ_IN + 1:]
    i = pl.program_id(1)
    o_ref[...] = _out_proj_tile(i > 0, i < pl.num_programs(1) - 1, *ins, *scratch)


def _out_proj(x, a, inv_count, o_win, o_glb, w_pool_bd, pool_scale, w_out, g_post, mod, tm):
    b, n, d = x.shape
    pw = a.shape[2]
    aw = o_win.shape[2]
    hb = tm // POOL_HALO
    row = lambda bi, i: (0, 0)
    tile = lambda bi, i: (bi, i, 0)
    return pl.pallas_call(
        _out_proj_kernel,
        out_shape=jax.ShapeDtypeStruct((b, n, d), F32),
        grid=(b, n // tm),
        in_specs=[
            pl.BlockSpec((None, tm, d), tile),
            pl.BlockSpec((None, tm, pw), tile),
            pl.BlockSpec((None, POOL_HALO, pw), lambda bi, i: (bi, jnp.maximum(i * hb - 1, 0), 0)),
            pl.BlockSpec((None, POOL_HALO, pw), lambda bi, i: (bi, jnp.minimum((i + 1) * hb, n // POOL_HALO - 1), 0)),
            pl.BlockSpec((tm, pw), lambda bi, i: (i, 0)),
            pl.BlockSpec((None, tm, aw), tile),
            pl.BlockSpec((None, tm, aw), tile),
            _resident((pw, pw), row),
            pl.BlockSpec((1, pw), row),
            _resident((d, d), row),
            pl.BlockSpec((1, d), row),
            pl.BlockSpec((None, N_MOD, d), lambda bi, i: (bi, 0, 0)),
        ],
        out_specs=pl.BlockSpec((None, tm, d), tile),
        scratch_shapes=[pltpu.VMEM((tm + 2 * POOL_HALO + POOL_HALO // 2, pw), F32),
                        pltpu.VMEM((tm + 2 * POOL_HALO, pw), F32), pltpu.VMEM((tm + 2 * POOL_HALO, pw), F32),
                        pltpu.VMEM((tm + 2 * POOL_HALO, LANES), F32)],
        compiler_params=_cparams("arbitrary", "arbitrary"),
        name="out_proj",
    )(x, a, a, a, inv_count, o_win, o_glb, w_pool_bd, pool_scale, w_out, g_post, mod)


def _ffn_tile(x, mod_ref, gpre_ref, gpost_ref, wg_ref, wu_ref, wd_ref, chunk):
    h = _rms(x, gpre_ref[...])
    h = (h * (1.0 + mod_ref[4:5, :]) + mod_ref[3:4, :]).astype(BF16)
    f = None
    for c in range(wg_ref.shape[1] // chunk):
        cols = slice(c * chunk, (c + 1) * chunk)
        g = jnp.dot(h, wg_ref[:, cols], preferred_element_type=F32)
        u = jnp.dot(h, wu_ref[:, cols], preferred_element_type=F32)
        act = ((g * jax.nn.sigmoid(g)) * u).astype(BF16)
        part = jnp.dot(act, wd_ref[cols, :], preferred_element_type=F32)
        f = part if f is None else f + part
    return x + mod_ref[5:6, :] * _rms(f, gpost_ref[...])


def _ffn_kernel(x_ref, mod_ref, gpre_ref, gpost_ref, wg_ref, wu_ref, wd_ref, o_ref, *, chunk):
    o_ref[...] = _ffn_tile(x_ref[...], mod_ref, gpre_ref, gpost_ref, wg_ref, wu_ref, wd_ref, chunk)


def _ffn(x, mod, g_pre, g_post, w_gate, w_up, w_down, tm, chunk):
    b, n, d = x.shape
    dff = w_gate.shape[1]
    row = lambda bi, i: (0, 0)
    tile = lambda bi, i: (bi, i, 0)
    return pl.pallas_call(
        functools.partial(_ffn_kernel, chunk=chunk),
        out_shape=jax.ShapeDtypeStruct((b, n, d), F32),
        grid=(b, n // tm),
        in_specs=[
            pl.BlockSpec((None, tm, d), tile),
            pl.BlockSpec((None, N_MOD, d), lambda bi, i: (bi, 0, 0)),
            pl.BlockSpec((1, d), row),
            pl.BlockSpec((1, d), row),
            _resident((d, dff), row),
            _resident((d, dff), row),
            _resident((dff, d), row),
        ],
        out_specs=pl.BlockSpec((None, tm, d), tile),
        compiler_params=_cparams("arbitrary", "arbitrary"),
        name="ffn",
    )(x, mod, g_pre, g_post, w_gate, w_up, w_down)


N_FFN_IN = 5
N_PROJ_IN = 8
N_PROJ_OUT = 7


def _tail_kernel(*refs, tpb, chunk, st, with_proj):
    refs = list(refs)
    op_in, refs = refs[:N_OUT_PROJ_IN], refs[N_OUT_PROJ_IN:]
    ffn_in, refs = refs[:N_FFN_IN], refs[N_FFN_IN:]
    if with_proj:
        (modi_ref, gin_ref, win_ref, cos_ref, sin_ref, gq_ref, gk_ref, qfill_ref), refs = refs[:N_PROJ_IN], refs[N_PROJ_IN:]
    o_ref, refs = refs[0], refs[1:]
    if with_proj:
        proj_out, refs = refs[:N_PROJ_OUT], refs[N_PROJ_OUT:]
        u_ref, refs = refs[0], refs[1:]
    pool_scratch = refs
    tm = o_ref.shape[0]
    step = pl.program_id(0)
    i = lax.rem(jnp.minimum(step, pl.num_programs(0) - (2 if with_proj else 1)), tpb)

    if with_proj:
        @pl.when(step == 0)
        def _():
            u_ref[...] = jnp.zeros(u_ref.shape, F32)

        _in_proj_epilogue(lambda sub: u_ref[sub * st:(sub + 1) * st, :], tm, st, cos_ref, sin_ref, gq_ref, gk_ref,
                          qfill_ref, *proj_out)

    x_mid = _out_proj_tile(i > 0, i < tpb - 1, *op_in, *pool_scratch)
    y = _ffn_tile(x_mid, op_in[-1], *ffn_in, chunk)
    o_ref[...] = y
    if with_proj:
        u_ref[...] = _project(y, modi_ref, gin_ref, win_ref)


def _tail(x, a, inv_count, o_win, o_glb, w_pool_bd, pool_scale, w_out, g_post, mod, g_pre_f, g_post_f, w_gate, w_up, w_down,
          proj, tm, chunk):
    b, n, d = x.shape
    pw, aw, dff = a.shape[2], o_win.shape[2], w_gate.shape[1]
    hb = tm // POOL_HALO
    tpb = n // tm
    last = b * tpb - 1
    with_proj = proj is not None
    cur = lambda s: (jnp.minimum(s, last) // tpb, jnp.minimum(s, last) % tpb)
    prev = lambda s: (jnp.maximum(s - 1, 0) // tpb, jnp.maximum(s - 1, 0) % tpb)
    row = lambda s: (0, 0)
    tile = lambda s: (cur(s)[0], cur(s)[1], 0)
    in_specs = [
        pl.BlockSpec((None, tm, d), tile),
        pl.BlockSpec((None, tm, pw), tile),
        pl.BlockSpec((None, POOL_HALO, pw), lambda s: (cur(s)[0], jnp.maximum(cur(s)[1] * hb - 1, 0), 0)),
        pl.BlockSpec((None, POOL_HALO, pw), lambda s: (cur(s)[0], jnp.minimum((cur(s)[1] + 1) * hb, n // POOL_HALO - 1), 0)),
        pl.BlockSpec((tm, pw), lambda s: (cur(s)[1], 0)),
        pl.BlockSpec((None, tm, aw), tile),
        pl.BlockSpec((None, tm, aw), tile),
        _resident((pw, pw), row),
        pl.BlockSpec((1, pw), row),
        _resident((d, d), row),
        pl.BlockSpec((1, d), row),
        pl.BlockSpec((None, N_MOD, d), lambda s: (cur(s)[0], 0, 0)),
        pl.BlockSpec((1, d), row),
        pl.BlockSpec((1, d), row),
        _resident((d, dff), row),
        _resident((d, dff), row),
        _resident((dff, d), row),
    ]
    args = [x, a, a, a, inv_count, o_win, o_glb, w_pool_bd, pool_scale, w_out, g_post, mod,
            g_pre_f, g_post_f, w_gate, w_up, w_down]
    out_shape = (jax.ShapeDtypeStruct((b, n, d), F32),)
    out_specs = (pl.BlockSpec((None, tm, d), tile),)
    scratch = [pltpu.VMEM((tm + 2 * POOL_HALO + POOL_HALO // 2, pw), F32),
               pltpu.VMEM((tm + 2 * POOL_HALO, pw), F32), pltpu.VMEM((tm + 2 * POOL_HALO, pw), F32),
               pltpu.VMEM((tm + 2 * POOL_HALO, LANES), F32)]
    if with_proj:
        mod_i, g_pre_i, w_in, cos, sin, gq, gk, qfill = proj
        nw = w_in.shape[1]
        in_specs += [
            pl.BlockSpec((None, N_MOD, d), lambda s: (cur(s)[0], 0, 0)),
            pl.BlockSpec((1, d), row),
            _resident((d, nw), row),
            pl.BlockSpec((tm, LANES), lambda s: (prev(s)[1], 0)),
            pl.BlockSpec((tm, LANES), lambda s: (prev(s)[1], 0)),
            pl.BlockSpec((1, LANES), row),
            pl.BlockSpec((1, LANES), row),
            pl.BlockSpec((1, LANES), row),
        ]
        args += [mod_i, g_pre_i, w_in, cos, sin, gq, gk, qfill]
        proj_shape, proj_specs = _in_proj_outputs(b, n, tm, prev)
        out_shape += proj_shape
        out_specs += proj_specs
        scratch = [pltpu.VMEM((tm, nw), F32)] + scratch
    outs = pl.pallas_call(
        functools.partial(_tail_kernel, tpb=tpb, chunk=chunk, st=min(tm, 256), with_proj=with_proj),
        out_shape=out_shape,
        grid=(b * tpb + (1 if with_proj else 0),),
        in_specs=in_specs,
        out_specs=out_specs,
        scratch_shapes=scratch,
        compiler_params=_cparams("arbitrary"),
        name="tail_proj" if with_proj else "tail",
    )(*args)
    return outs[0], outs[1:]


def _rope_tables(n):
    rows = n // GRID_W
    row = jnp.broadcast_to(jnp.arange(rows)[:, None], (rows, GRID_W)).reshape(-1).astype(F32)
    col = jnp.broadcast_to(jnp.arange(GRID_W)[None, :], (rows, GRID_W)).reshape(-1).astype(F32)
    freq = ROPE_THETA ** (-jnp.arange(ROPE_FREQS, dtype=F32) / ROPE_FREQS)
    ar, ac = row[:, None] * freq, col[:, None] * freq
    cos = jnp.concatenate([jnp.cos(ar), jnp.cos(ar), jnp.cos(ac), jnp.cos(ac)], axis=1)
    sin = jnp.concatenate([-jnp.sin(ar), jnp.sin(ar), -jnp.sin(ac), jnp.sin(ac)], axis=1)
    return jnp.tile(cos, (1, LANES // HEAD_DIM)), jnp.tile(sin, (1, LANES // HEAD_DIM))


def _shift_row(m):
    pieces, rest = [], m.astype(F32)
    for _ in range(N_SHIFT):
        part = rest.astype(BF16).astype(F32)
        pieces.append(-part)
        rest = rest - part
    return jnp.zeros((1, LANES), F32).at[0, HEAD_DIM:HEAD_DIM + N_SHIFT].set(jnp.stack(pieces))


def _pool_inv_counts(n):
    t = jnp.arange(n)
    cols = []
    for w in POOL_WINDOWS:
        hi = jnp.clip(t + (w - 1 - w // 2), 0, n - 1)
        lo = jnp.clip(t - w // 2, 0, n - 1)
        cols.append(jnp.broadcast_to((1.0 / (hi - lo + 1).astype(F32))[:, None], (n, HEAD_DIM)))
    return jnp.concatenate(cols, axis=1)


def _tile(n, pref):
    t = min(n, pref)
    assert n % t == 0, (n, t)
    return t


def kernel(x, c, ctx, c_ctx, w_mod, b_mod, g_pre_mix, g_post_mix, g_pre_ffn, g_post_ffn, w_in, w_pool, pool_scale, win_sink, g_qnorm, g_knorm, w_out, w_gate, w_up, w_down):
    b, n, d = x.shape
    lc = ctx.shape[1]
    depth = w_mod.shape[0]
    assert w_in.shape[2] == IN_WIDTH and KV_WIDTH == LANES and n % GRID_W == 0

    cos, sin = _rope_tables(n)
    cos_c, sin_c = jnp.ones((lc, LANES), F32), jnp.zeros((lc, LANES), F32)

    mod_rows = -(-(b + 1) // 8) * 8
    cc = jnp.concatenate([c, c_ctx[None, :], jnp.zeros((mod_rows - b - 1, d), F32)], axis=0)
    mod_all = _modulation(cc, w_mod, b_mod)

    tm = _tile(n, 512)
    tq_win = _tile(n, 1024)
    tq_glb = _tile(n, 512)
    tk_glb = n // (N_KV_HEADS * (tq_glb // WINDOW))
    assert tk_glb % LANES == 0
    rb_glb = 512
    ffn_chunk = 256

    def layer(l):
        bound = ATTN_SCALE * HEAD_DIM * jnp.max(jnp.abs(g_qnorm[l])) * jnp.max(jnp.abs(g_knorm[l]))
        return dict(
            m=mod_all[l, :b].reshape(b, N_MOD, d),
            mc=jnp.broadcast_to(mod_all[l, b].reshape(1, N_MOD, d), (b, N_MOD, d)),
            w_in=w_in[l].astype(BF16),
            gq=jnp.tile(g_qnorm[l], LANES // HEAD_DIM)[None, :],
            gk=jnp.tile(g_knorm[l], LANES // HEAD_DIM)[None, :],
            g_pre=g_pre_mix[l][None, :],
            bound=bound,
            qfill=jnp.where(bound <= SAFE_SHIFT, _shift_row(bound), 0.0),
        )

    no_fill = jnp.zeros((1, LANES), F32)
    icnt, icnt_c = _pool_inv_counts(n), _pool_inv_counts(lc)
    xc = ctx
    p = layer(0)
    proj = _in_proj(x, p["m"], p["g_pre"], p["w_in"], cos, sin, p["gq"], p["gk"], p["qfill"], tm, carried=True)
    for l in range(depth):
        last = l == depth - 1
        m, mc, gq, gk = p["m"], p["mc"], p["gq"], p["gk"]
        w_out_l = w_out[l].astype(BF16)
        w_gate_l, w_up_l, w_down_l = w_gate[l].astype(BF16), w_up[l].astype(BF16), w_down[l].astype(BF16)
        w_pool_bd = jax.scipy.linalg.block_diag(*[w_pool[l, g] for g in range(len(POOL_WINDOWS))]).astype(BF16)
        pscale = pool_scale[l][None, :]
        sink = jnp.zeros((8, LANES), F32).at[:N_Q_HEADS].set(jnp.broadcast_to(win_sink[l][:, None], (N_Q_HEADS, LANES)))
        g_post = g_post_mix[l][None, :]
        g_pre_f, g_post_f = g_pre_ffn[l][None, :], g_post_ffn[l][None, :]

        a, qw, ktw, vw, qg, ktg, vg = proj
        ac, qwc, ktwc, vwc, qgc, ktgc, vgc = _in_proj(xc, mc, p["g_pre"], p["w_in"], cos_c, sin_c, gq, gk, no_fill, lc)

        def attn_bounded(*ops):
            o_g, o_w = _attn(*ops, tq=tq_glb, tk=tk_glb, rb=rb_glb)
            return o_w, o_g

        def attn_exact(qg, ktg, vg, ktgc, vgc, qw, ktw, vw, ktwc, vwc, sink):
            return (_window_attn(qw, ktw, vw, ktwc, vwc, sink, tq_win),
                    _global_attn(qg, ktg, vg, ktgc, vgc, tq=tq_glb, tk=tk_glb, rb=rb_glb))

        o_win, o_glb = lax.cond(p["bound"] <= SAFE_SHIFT, attn_bounded, attn_exact,
                                qg, ktg, vg, ktgc, vgc, qw, ktw, vw, ktwc, vwc, sink)
        tail_args = (x, a, icnt, o_win, o_glb, w_pool_bd, pscale, w_out_l, g_post, m, g_pre_f, g_post_f,
                     w_gate_l, w_up_l, w_down_l)
        if last:
            x, _ = _tail(*tail_args, None, tm, ffn_chunk)
        else:
            oc_win = _ctx_attn(qwc, ktwc, vwc, sink)
            oc_glb = _ctx_attn(qgc, ktgc, vgc, None)
            xc = _out_proj(xc, ac, icnt_c, oc_win, oc_glb, w_pool_bd, pscale, w_out_l, g_post, mc, lc)
            xc = _ffn(xc, mc, g_pre_f, g_post_f, w_gate_l, w_up_l, w_down_l, lc, ffn_chunk)
            p = layer(l + 1)
            x, proj = _tail(*tail_args, (p["m"], p["g_pre"], p["w_in"], cos, sin, p["gq"], p["gk"], p["qfill"]),
                            tm, ffn_chunk)
    return x
```

```python
import functools

import jax
import jax.numpy as jnp
from jax import lax
from jax.experimental import pallas as pl
from jax.experimental.pallas import tpu as pltpu

F32 = jnp.float32
BF16 = jnp.bfloat16

HEAD_DIM = 64
GRID_W = 64
ROPE_FREQS = HEAD_DIM // 4
ROPE_THETA = 10000.0
NORM_EPS = 1e-6
NEG_INF = -1e30
ATTN_SCALE = HEAD_DIM ** -0.5
WINDOW = 128
POOL_WINDOWS = (2, 4, 8, 16)
POOL_HALO = 16
N_Q_HEADS = 6
N_KV_HEADS = 2
KV_GROUP = N_Q_HEADS // N_KV_HEADS
N_MOD = 6
POOL_WIDTH = len(POOL_WINDOWS) * HEAD_DIM
Q_WIDTH = N_Q_HEADS * HEAD_DIM
KV_WIDTH = N_KV_HEADS * HEAD_DIM
COL_QW = POOL_WIDTH
COL_KW = COL_QW + Q_WIDTH
COL_VW = COL_KW + KV_WIDTH
COL_QG = COL_VW + KV_WIDTH
COL_KG = COL_QG + Q_WIDTH
COL_VG = COL_KG + KV_WIDTH
IN_WIDTH = COL_VG + KV_WIDTH
N_SHIFT = 3
SAFE_SHIFT = 30.0

LANES = 128
VMEM_LIMIT = 56 * 1024 * 1024


def _cparams(*sem):
    return pltpu.CompilerParams(dimension_semantics=sem, vmem_limit_bytes=VMEM_LIMIT)


def _resident(block_shape, index_map):
    return pl.BlockSpec(block_shape, index_map, pipeline_mode=pl.Buffered(1))


def _rms(x, g):
    ms = jnp.mean(x * x, axis=-1, keepdims=True)
    return (x * lax.rsqrt(ms + NORM_EPS)) * g


def _mod_kernel(c_ref, w_ref, b_ref, o_ref):
    c = c_ref[...]
    s = c * jax.nn.sigmoid(c)
    o_ref[...] = jnp.dot(s.astype(BF16), w_ref[...].astype(BF16), preferred_element_type=F32) + b_ref[...]


def _modulation(cc, w_mod, b_mod):
    depth, d, dm = w_mod.shape
    rows = cc.shape[0]
    tn = dm // 4
    return pl.pallas_call(
        _mod_kernel,
        out_shape=jax.ShapeDtypeStruct((depth, rows, dm), F32),
        grid=(depth, dm // tn),
        in_specs=[
            pl.BlockSpec((rows, d), lambda l, j: (0, 0)),
            pl.BlockSpec((None, d, tn), lambda l, j: (l, 0, j)),
            pl.BlockSpec((None, 1, tn), lambda l, j: (l, 0, j)),
        ],
        out_specs=pl.BlockSpec((None, rows, tn), lambda l, j: (l, 0, j)),
        compiler_params=_cparams("arbitrary", "arbitrary"),
        name="modulation",
    )(cc, w_mod, b_mod.reshape(depth, 1, dm))


def _in_proj_epilogue(u_of, n_rows, st, cos_ref, sin_ref, gq_ref, gk_ref, qfill_ref,
                      a_ref, qw_ref, ktw_ref, vw_ref, qg_ref, ktg_ref, vg_ref):
    lane = lax.broadcasted_iota(jnp.int32, (st, LANES), 1)
    lo = lane < HEAD_DIM
    first = (lane & ROPE_FREQS) == 0
    shift_rows = (lax.broadcasted_iota(jnp.int32, (HEAD_DIM, LANES), 0) < N_SHIFT).astype(BF16)
    sum_lane = (lane == HEAD_DIM).astype(F32)
    gq, gk = gq_ref[...], gk_ref[...]
    qfill = qfill_ref[...]

    def head_norm(t, g):
        sq = t * t
        s_lo = jnp.sum(jnp.where(lo, sq, 0.0), axis=-1, keepdims=True)
        s_hi = jnp.sum(jnp.where(lo, 0.0, sq), axis=-1, keepdims=True)
        ms = jnp.where(lo, s_lo, s_hi) * (1.0 / HEAD_DIM)
        return (t * lax.rsqrt(ms + NORM_EPS)) * g

    for sub in range(n_rows // st):
        rs = slice(sub * st, (sub + 1) * st)
        u = u_of(sub)
        cos = cos_ref[rs, :]
        sin = sin_ref[rs, :]

        def rope(t):
            partner = jnp.where(first, pltpu.roll(t, LANES - ROPE_FREQS, 1), pltpu.roll(t, ROPE_FREQS, 1))
            return t * cos + partner * sin

        def store_branch(q_chunks, k, v, fill, q_ref, kt_ref, v_ref):
            for hd in range(N_Q_HEADS):
                t = q_chunks[hd // 2]
                if hd % 2:
                    t = pltpu.roll(t, HEAD_DIM, 1)
                q_ref[hd, rs, :] = jnp.where(lo, t, fill).astype(BF16)
            for r in range(st // LANES):
                kb = sub * (st // LANES) + r
                kt = jnp.transpose(k[r * LANES:(r + 1) * LANES, :]).astype(BF16)
                for kv in range(N_KV_HEADS):
                    kt_ref[kv, kb, 0:HEAD_DIM, :] = kt[kv * HEAD_DIM:(kv + 1) * HEAD_DIM, :]
                    kt_ref[kv, kb, HEAD_DIM:LANES, :] = shift_rows
            v_ref[0, rs, :] = jnp.where(lo, v, sum_lane).astype(BF16)
            v_ref[1, rs, :] = jnp.where(lo, pltpu.roll(v, HEAD_DIM, 1), sum_lane).astype(BF16)

        def cols(start, width):
            return u[:, start:start + width]

        a_ref[rs, :] = cols(0, POOL_WIDTH)
        store_branch([rope(cols(COL_QW + LANES * c, LANES)) * ATTN_SCALE for c in range(Q_WIDTH // LANES)],
                     rope(cols(COL_KW, KV_WIDTH)), cols(COL_VW, KV_WIDTH), 0.0, qw_ref, ktw_ref, vw_ref)
        store_branch([rope(head_norm(cols(COL_QG + LANES * c, LANES), gq)) * ATTN_SCALE for c in range(Q_WIDTH // LANES)],
                     rope(head_norm(cols(COL_KG, KV_WIDTH), gk)), cols(COL_VG, KV_WIDTH), qfill, qg_ref, ktg_ref, vg_ref)


def _project(x, mod_ref, g_ref, w_ref):
    h = _rms(x, g_ref[...])
    h = h * (1.0 + mod_ref[1:2, :]) + mod_ref[0:1, :]
    return jnp.dot(h.astype(BF16), w_ref[...], preferred_element_type=F32)


def _in_proj_kernel(x_ref, mod_ref, g_ref, w_ref, cos_ref, sin_ref, gq_ref, gk_ref, qfill_ref, *out_refs, st):
    tm = x_ref.shape[0]
    us = [_project(x_ref[sub * st:(sub + 1) * st, :], mod_ref, g_ref, w_ref) for sub in range(tm // st)]
    _in_proj_epilogue(lambda sub: us[sub], tm, st, cos_ref, sin_ref, gq_ref, gk_ref, qfill_ref, *out_refs)


def _in_proj_outputs(b, n, tm, tile_of):
    qshape = jax.ShapeDtypeStruct((b, N_Q_HEADS, n, LANES), BF16)
    ktshape = jax.ShapeDtypeStruct((b, N_KV_HEADS, n // LANES, LANES, LANES), BF16)
    vshape = jax.ShapeDtypeStruct((b, N_KV_HEADS, n, LANES), BF16)

    def spec(block, where):
        return pl.BlockSpec(block, lambda *g: where(*tile_of(*g)))

    aspec = spec((None, tm, POOL_WIDTH), lambda bi, i: (bi, i, 0))
    qspec = spec((None, N_Q_HEADS, tm, LANES), lambda bi, i: (bi, 0, i, 0))
    ktspec = spec((None, N_KV_HEADS, tm // LANES, LANES, LANES), lambda bi, i: (bi, 0, i, 0, 0))
    vspec = spec((None, N_KV_HEADS, tm, LANES), lambda bi, i: (bi, 0, i, 0))
    shapes = (jax.ShapeDtypeStruct((b, n, POOL_WIDTH), F32), qshape, ktshape, vshape, qshape, ktshape, vshape)
    return shapes, (aspec, qspec, ktspec, vspec, qspec, ktspec, vspec)


def _in_proj_carried_kernel(x_ref, mod_ref, g_ref, w_ref, cos_ref, sin_ref, gq_ref, gk_ref, qfill_ref, *refs, st):
    out_refs, u_ref = refs[:-1], refs[-1]

    @pl.when(pl.program_id(0) == 0)
    def _():
        u_ref[...] = jnp.zeros(u_ref.shape, F32)

    _in_proj_epilogue(lambda sub: u_ref[sub * st:(sub + 1) * st, :], x_ref.shape[0], st, cos_ref, sin_ref,
                      gq_ref, gk_ref, qfill_ref, *out_refs)
    u_ref[...] = _project(x_ref[...], mod_ref, g_ref, w_ref)


def _in_proj(x, mod, g_pre, w_in, cos, sin, gq, gk, qfill, tm, carried=False):
    b, n, d = x.shape
    nw = w_in.shape[1]
    tpb = n // tm
    last = b * tpb - 1
    if carried:
        grid = (b * tpb + 1,)
        cur = lambda s: (jnp.minimum(s, last) // tpb, jnp.minimum(s, last) % tpb)
        prev = lambda s: (jnp.maximum(s - 1, 0) // tpb, jnp.maximum(s - 1, 0) % tpb)
        kern, scratch, sem = _in_proj_carried_kernel, [pltpu.VMEM((tm, nw), F32)], ("arbitrary",)
    else:
        grid = (b, tpb)
        cur = prev = lambda bi, i: (bi, i)
        kern, scratch, sem = _in_proj_kernel, [], ("arbitrary", "arbitrary")
    out_shape, out_specs = _in_proj_outputs(b, n, tm, prev)
    row = lambda *g: (0, 0)
    return pl.pallas_call(
        functools.partial(kern, st=min(tm, 256)),
        out_shape=out_shape,
        grid=grid,
        in_specs=[
            pl.BlockSpec((None, tm, d), lambda *g: (cur(*g)[0], cur(*g)[1], 0)),
            pl.BlockSpec((None, N_MOD, d), lambda *g: (cur(*g)[0], 0, 0)),
            pl.BlockSpec((1, d), row),
            _resident((d, nw), row),
            pl.BlockSpec((tm, LANES), lambda *g: (prev(*g)[1], 0)),
            pl.BlockSpec((tm, LANES), lambda *g: (prev(*g)[1], 0)),
            pl.BlockSpec((1, LANES), row),
            pl.BlockSpec((1, LANES), row),
            pl.BlockSpec((1, LANES), row),
        ],
        out_specs=out_specs,
        scratch_shapes=scratch,
        compiler_params=_cparams(*sem),
        name="in_proj",
    )(x, mod, g_pre, w_in, cos, sin, gq, gk, qfill)


def _merge_heads(outs, lo):
    return [jnp.where(lo, outs[2 * c], pltpu.roll(outs[2 * c + 1], HEAD_DIM, 1)) for c in range(N_Q_HEADS // 2)]


def _key_tile(kt_ref, first_block, n_blocks):
    return jnp.concatenate([kt_ref[first_block + j] for j in range(n_blocks)], axis=1)


def _stacked_sink(sink_ref, kv, rows_per_head):
    return jnp.concatenate([jnp.broadcast_to(sink_ref[hd:hd + 1, 0:1], (rows_per_head, 1))
                            for hd in range(kv * KV_GROUP, (kv + 1) * KV_GROUP)], axis=0)


def _window_units(q_ref, kt_ref, v_ref, ktc_ref, vc_ref, sink_ref, first_blk):
    nkb = kt_ref.shape[1]
    sub = WINDOW
    rows = KV_GROUP * sub
    r_in = lax.broadcasted_iota(jnp.int32, (rows, LANES), 0) & (sub - 1)
    c_in = lax.broadcasted_iota(jnp.int32, (rows, LANES), 1)
    ctx = [(_key_tile(ktc_ref.at[kv], 0, ktc_ref.shape[1]), vc_ref[kv], _stacked_sink(sink_ref, kv, sub))
           for kv in range(N_KV_HEADS)]

    def blocks(j):
        blk = first_blk + j
        return blk, jnp.maximum(blk - 1, 0), jnp.minimum(blk + 1, nkb - 1)

    def logits(j, kv):
        blk, prev_blk, next_blk = blocks(j)
        q = q_ref[kv * KV_GROUP:(kv + 1) * KV_GROUP, pl.ds(pl.multiple_of(j * sub, sub), sub), :].reshape(rows, LANES)
        kt = jnp.concatenate([kt_ref[kv, prev_blk], kt_ref[kv, blk], kt_ref[kv, next_blk]], axis=1)
        s = jnp.dot(q, kt, preferred_element_type=F32)
        sc = jnp.dot(q, ctx[kv][0], preferred_element_type=F32)
        return s, sc

    def attend(j, kv, s, sc):
        blk, prev_blk, next_blk = blocks(j)
        _, vc, sk = ctx[kv]
        v = jnp.concatenate([v_ref[kv, pl.ds(pl.multiple_of(kb * sub, sub), sub), :]
                             for kb in (prev_blk, blk, next_blk)], axis=0)
        s_prev = jnp.where(jnp.logical_and(c_in >= r_in, blk > 0), s[:, 0:sub], NEG_INF)
        s_own = s[:, sub:2 * sub]
        s_next = jnp.where(jnp.logical_and(c_in <= r_in, blk < nkb - 1), s[:, 2 * sub:3 * sub], NEG_INF)
        mx = jnp.maximum(jnp.maximum(s_prev, s_own), s_next)
        for t in range(sc.shape[1] // LANES):
            mx = jnp.maximum(mx, sc[:, t * LANES:(t + 1) * LANES])
        m = jnp.maximum(jnp.max(mx, axis=-1, keepdims=True), sk)
        p = jnp.concatenate([jnp.exp(s_prev - m), jnp.exp(s_own - m), jnp.exp(s_next - m)], axis=1)
        pv = jnp.dot(p.astype(BF16), v, preferred_element_type=F32)
        pv = pv + jnp.dot(jnp.exp(sc - m).astype(BF16), vc, preferred_element_type=F32)
        l = pv[:, HEAD_DIM:HEAD_DIM + 1] + jnp.exp(sk - m)
        o = pv / l
        return [o[g * sub:(g + 1) * sub] for g in range(KV_GROUP)]

    return logits, attend


def _store_heads(o_ref, row0, outs):
    nrows = outs[0].shape[0]
    lo = lax.broadcasted_iota(jnp.int32, (nrows, LANES), 1) < HEAD_DIM
    for c, chunk in enumerate(_merge_heads(outs, lo)):
        o_ref[pl.ds(row0, nrows), c * LANES:(c + 1) * LANES] = chunk.astype(BF16)


def _window_attn_kernel(q_ref, kt_ref, v_ref, ktc_ref, vc_ref, sink_ref, o_ref):
    tq = q_ref.shape[1]
    sub = WINDOW
    logits, attend = _window_units(q_ref, kt_ref, v_ref, ktc_ref, vc_ref, sink_ref, pl.program_id(1) * (tq // sub))
    units = [(j, kv) for j in range(tq // sub) for kv in range(N_KV_HEADS)]
    pending = logits(*units[0])
    outs = []
    for idx, (j, kv) in enumerate(units):
        ahead = logits(*units[idx + 1]) if idx + 1 < len(units) else None
        outs += attend(j, kv, *pending)
        pending = ahead
        if kv == N_KV_HEADS - 1:
            _store_heads(o_ref, j * sub, outs)
            outs = []


def _window_attn(q, kt, v, ktc, vc, sink, tq):
    b, _, n, _ = q.shape
    lc = vc.shape[2]
    assert tq % WINDOW == 0 and WINDOW == LANES
    batch4 = lambda bi, i: (bi, 0, 0, 0)
    batch5 = lambda bi, i: (bi, 0, 0, 0, 0)
    return pl.pallas_call(
        _window_attn_kernel,
        out_shape=jax.ShapeDtypeStruct((b, n, N_Q_HEADS * HEAD_DIM), BF16),
        grid=(b, n // tq),
        in_specs=[
            pl.BlockSpec((None, N_Q_HEADS, tq, LANES), lambda bi, i: (bi, 0, i, 0)),
            pl.BlockSpec((None, N_KV_HEADS, n // LANES, LANES, LANES), batch5),
            pl.BlockSpec((None, N_KV_HEADS, n, LANES), batch4),
            pl.BlockSpec((None, N_KV_HEADS, lc // LANES, LANES, LANES), batch5),
            pl.BlockSpec((None, N_KV_HEADS, lc, LANES), batch4),
            pl.BlockSpec(sink.shape, lambda bi, i: (0, 0)),
        ],
        out_specs=pl.BlockSpec((None, tq, N_Q_HEADS * HEAD_DIM), lambda bi, i: (bi, i, 0)),
        compiler_params=_cparams("arbitrary", "arbitrary"),
        name="window_attn",
    )(q, kt, v, ktc, vc, sink)


def _ctx_attn_kernel(*refs, has_sink):
    if has_sink:
        q_ref, kt_ref, v_ref, sink_ref, o_ref = refs
    else:
        q_ref, kt_ref, v_ref, o_ref = refs
    tq = q_ref.shape[1]
    rows = KV_GROUP * tq
    outs = []
    for kv in range(N_KV_HEADS):
        q = q_ref[kv * KV_GROUP:(kv + 1) * KV_GROUP].reshape(rows, LANES)
        s = jnp.dot(q, _key_tile(kt_ref.at[kv], 0, kt_ref.shape[1]), preferred_element_type=F32)
        m = jnp.max(s, axis=-1, keepdims=True)
        if has_sink:
            sk = _stacked_sink(sink_ref, kv, tq)
            m = jnp.maximum(m, sk)
        pv = jnp.dot(jnp.exp(s - m).astype(BF16), v_ref[kv], preferred_element_type=F32)
        l = pv[:, HEAD_DIM:HEAD_DIM + 1]
        if has_sink:
            l = l + jnp.exp(sk - m)
        o = pv / l
        outs += [o[g * tq:(g + 1) * tq] for g in range(KV_GROUP)]
    _store_heads(o_ref, 0, outs)


def _ctx_attn(q, kt, v, sink):
    b, _, n, _ = q.shape
    args = [q, kt, v]
    in_specs = [pl.BlockSpec((None,) + a.shape[1:], lambda bi, nd=a.ndim: (bi,) + (0,) * (nd - 1)) for a in args]
    if sink is not None:
        args.append(sink)
        in_specs.append(pl.BlockSpec(sink.shape, lambda bi: (0, 0)))
    return pl.pallas_call(
        functools.partial(_ctx_attn_kernel, has_sink=sink is not None),
        out_shape=jax.ShapeDtypeStruct((b, n, N_Q_HEADS * HEAD_DIM), BF16),
        grid=(b,),
        in_specs=in_specs,
        out_specs=pl.BlockSpec((None, n, N_Q_HEADS * HEAD_DIM), lambda bi: (bi, 0, 0)),
        compiler_params=_cparams("arbitrary"),
        name="ctx_attn",
    )(*args)


def _global_attn_kernel(q_ref, kt_ref, v_ref, ktc_ref, vc_ref, o_ref, qs_ref, acc_ref, mx_ref, *, tk, rb):
    tq = q_ref.shape[1]
    rows = KV_GROUP * tq
    kb_per_step = tk // LANES
    n_steps = kt_ref.shape[1] // kb_per_step
    lane = lax.broadcasted_iota(jnp.int32, (rows, LANES), 1)

    def key_steps(fn):
        def step(kc, carry):
            for kv in range(N_KV_HEADS):
                fn(kv, _key_tile(kt_ref.at[kv], kc * kb_per_step, kb_per_step),
                   v_ref[kv, pl.ds(pl.multiple_of(kc * tk, tk), tk), :])
            return carry
        lax.fori_loop(0, n_steps, step, 0)
        for kv in range(N_KV_HEADS):
            fn(kv, _key_tile(ktc_ref.at[kv], 0, ktc_ref.shape[1]), vc_ref[kv])

    def q_rows(kv):
        return q_ref[kv * KV_GROUP:(kv + 1) * KV_GROUP].reshape(rows, LANES)

    mx_ref[...] = jnp.full(mx_ref.shape, NEG_INF, F32)

    def track_max(kv, kt, v):
        s = jnp.dot(q_rows(kv), kt, preferred_element_type=F32)
        mx = mx_ref[kv]
        for j in range(kt.shape[1] // LANES):
            mx = jnp.maximum(mx, s[:, j * LANES:(j + 1) * LANES])
        mx_ref[kv] = mx

    key_steps(track_max)

    for kv in range(N_KV_HEADS):
        qs = q_rows(kv).astype(F32)
        rest = jnp.max(mx_ref[kv], axis=-1, keepdims=True)
        for piece in range(N_SHIFT):
            part = rest.astype(BF16).astype(F32)
            qs = jnp.where(lane == HEAD_DIM + piece, -part, qs)
            rest = rest - part
        qs_ref[kv] = qs.astype(BF16)

    acc_ref[...] = jnp.zeros(acc_ref.shape, F32)

    def accumulate(kv, kt, v):
        for r in range(rows // rb):
            rsl = slice(r * rb, (r + 1) * rb)
            s = jnp.dot(qs_ref[kv, rsl, :], kt, preferred_element_type=F32)
            p = jnp.exp(s).astype(BF16)
            acc_ref[kv, rsl, :] += jnp.dot(p, v, preferred_element_type=F32)

    key_steps(accumulate)

    outs = []
    for hd in range(N_Q_HEADS):
        a = acc_ref[hd // KV_GROUP, (hd % KV_GROUP) * tq:(hd % KV_GROUP + 1) * tq, :]
        outs.append(a / a[:, HEAD_DIM:HEAD_DIM + 1])
    _store_heads(o_ref, 0, outs)


def _global_attn(q, kt, v, ktc, vc, *, tq, tk, rb):
    b, _, n, _ = q.shape
    lc = vc.shape[2]
    rows = KV_GROUP * tq
    batch4 = lambda bi, i: (bi, 0, 0, 0)
    batch5 = lambda bi, i: (bi, 0, 0, 0, 0)
    return pl.pallas_call(
        functools.partial(_global_attn_kernel, tk=tk, rb=rb),
        out_shape=jax.ShapeDtypeStruct((b, n, N_Q_HEADS * HEAD_DIM), BF16),
        grid=(b, n // tq),
        in_specs=[
            pl.BlockSpec((None, N_Q_HEADS, tq, LANES), lambda bi, i: (bi, 0, i, 0)),
            pl.BlockSpec((None, N_KV_HEADS, n // LANES, LANES, LANES), batch5),
            pl.BlockSpec((None, N_KV_HEADS, n, LANES), batch4),
            pl.BlockSpec((None, N_KV_HEADS, lc // LANES, LANES, LANES), batch5),
            pl.BlockSpec((None, N_KV_HEADS, lc, LANES), batch4),
        ],
        out_specs=pl.BlockSpec((None, tq, N_Q_HEADS * HEAD_DIM), lambda bi, i: (bi, i, 0)),
        scratch_shapes=[pltpu.VMEM((N_KV_HEADS, rows, LANES), BF16), pltpu.VMEM((N_KV_HEADS, rows, LANES), F32),
                        pltpu.VMEM((N_KV_HEADS, rows, LANES), F32)],
        compiler_params=_cparams("arbitrary", "arbitrary"),
        name="global_attn_exact_max",
    )(q, kt, v, ktc, vc)


def _attn_kernel(qg_ref, ktg_ref, vg_ref, ktgc_ref, vgc_ref, qw_ref, ktw_ref, vw_ref, ktwc_ref, vwc_ref, sink_ref,
                 og_ref, ow_ref, acc_ref, *, tk, rb):
    tq = qg_ref.shape[1]
    rows = KV_GROUP * tq
    sub = WINDOW
    n_sub = tq // sub
    kb_per_step = tk // LANES
    steps_per_sub = ktg_ref.shape[1] // kb_per_step // n_sub
    assert steps_per_sub * n_sub * kb_per_step == ktg_ref.shape[1] and tq % rb == 0

    def global_keys(kv, kt, v, assign):
        for r in range(rows // rb):
            hd, off = kv * KV_GROUP + (r * rb) // tq, (r * rb) % tq
            s = jnp.dot(qg_ref[hd, off:off + rb, :], kt, preferred_element_type=F32)
            pv = jnp.dot(jnp.exp(s).astype(BF16), v, preferred_element_type=F32)
            if assign:
                acc_ref[kv, r * rb:(r + 1) * rb, :] = pv
            else:
                acc_ref[kv, r * rb:(r + 1) * rb, :] += pv

    def global_step(kc):
        for kv in range(N_KV_HEADS):
            global_keys(kv, _key_tile(ktg_ref.at[kv], kc * kb_per_step, kb_per_step),
                        vg_ref[kv, pl.ds(pl.multiple_of(kc * tk, tk), tk), :], False)

    for kv in range(N_KV_HEADS):
        global_keys(kv, _key_tile(ktgc_ref.at[kv], 0, ktgc_ref.shape[1]), vgc_ref[kv], True)

    logits, attend = _window_units(qw_ref, ktw_ref, vw_ref, ktwc_ref, vwc_ref, sink_ref, pl.program_id(1) * n_sub)

    def body(j, carry):
        pending = logits(j, 0)
        outs = []
        for kv in range(N_KV_HEADS):
            for t in range(steps_per_sub // N_KV_HEADS):
                global_step(j * steps_per_sub + kv * (steps_per_sub // N_KV_HEADS) + t)
            ahead = logits(j, kv + 1) if kv + 1 < N_KV_HEADS else None
            outs += attend(j, kv, *pending)
            pending = ahead
        _store_heads(ow_ref, pl.multiple_of(j * sub, sub), outs)
        return carry

    assert steps_per_sub % N_KV_HEADS == 0
    lax.fori_loop(0, n_sub, body, 0, unroll=True)

    outs = []
    for hd in range(N_Q_HEADS):
        a = acc_ref[hd // KV_GROUP, (hd % KV_GROUP) * tq:(hd % KV_GROUP + 1) * tq, :]
        outs.append(a / a[:, HEAD_DIM:HEAD_DIM + 1])
    _store_heads(og_ref, 0, outs)


def _attn(qg, ktg, vg, ktgc, vgc, qw, ktw, vw, ktwc, vwc, sink, *, tq, tk, rb):
    b, _, n, _ = qg.shape
    lc = vgc.shape[2]
    batch4 = lambda bi, i: (bi, 0, 0, 0)
    batch5 = lambda bi, i: (bi, 0, 0, 0, 0)
    branch_specs = [
        pl.BlockSpec((None, N_Q_HEADS, tq, LANES), lambda bi, i: (bi, 0, i, 0)),
        _resident((None, N_KV_HEADS, n // LANES, LANES, LANES), batch5),
        _resident((None, N_KV_HEADS, n, LANES), batch4),
        _resident((None, N_KV_HEADS, lc // LANES, LANES, LANES), batch5),
        _resident((None, N_KV_HEADS, lc, LANES), batch4),
    ]
    oshape = jax.ShapeDtypeStruct((b, n, N_Q_HEADS * HEAD_DIM), BF16)
    ospec = pl.BlockSpec((None, tq, N_Q_HEADS * HEAD_DIM), lambda bi, i: (bi, i, 0))
    return pl.pallas_call(
        functools.partial(_attn_kernel, tk=tk, rb=rb),
        out_shape=(oshape, oshape),
        grid=(b, n // tq),
        in_specs=branch_specs + branch_specs + [pl.BlockSpec(sink.shape, lambda bi, i: (0, 0))],
        out_specs=(ospec, ospec),
        scratch_shapes=[pltpu.VMEM((N_KV_HEADS, KV_GROUP * tq, LANES), F32)],
        compiler_params=_cparams("arbitrary", "arbitrary"),
        name="attn",
    )(qg, ktg, vg, ktgc, vgc, qw, ktw, vw, ktwc, vwc, sink)


def _out_proj_tile(has_prev, has_next, x_ref, a_ref, ap_ref, an_ref, icnt_ref, ow_ref, og_ref, wpool_ref, pscale_ref,
                   wo_ref, g_ref, mod_ref, ext_ref, s2_ref, s4_ref, s8_ref):
    tm = x_ref.shape[0]
    ext_ref[0:POOL_HALO, :] = jnp.where(has_prev, ap_ref[...], 0.0)
    ext_ref[POOL_HALO:POOL_HALO + tm, :] = a_ref[...]
    ext_ref[POOL_HALO + tm:2 * POOL_HALO + tm, :] = jnp.where(has_next, an_ref[...], 0.0)

    pw = a_ref.shape[1]
    aw = ow_ref.shape[1]
    y = jnp.dot(ow_ref[...], wo_ref[pw:pw + aw, :], preferred_element_type=F32)
    y = y + jnp.dot(og_ref[...], wo_ref[pw + aw:pw + 2 * aw, :], preferred_element_type=F32)

    lo = lax.broadcasted_iota(jnp.int32, (tm, LANES), 1) < HEAD_DIM

    pad = POOL_HALO // 2
    ext_ref[2 * POOL_HALO + tm:2 * POOL_HALO + tm + pad, :] = jnp.zeros((pad, ext_ref.shape[1]), F32)
    r2, r4, r8 = tm + 2 * POOL_HALO, tm + 2 * POOL_HALO - pad, tm + POOL_HALO
    s2_ref[0:r2, :] = ext_ref[0:r2, :] + ext_ref[1:r2 + 1, :]
    s4_ref[0:r4, :] = s2_ref[0:r4, :] + s2_ref[2:r4 + 2, :]
    s8_ref[0:r8, :] = s4_ref[0:r8, LANES:2 * LANES] + s4_ref[4:r8 + 4, LANES:2 * LANES]

    def window_sum(w, c):
        first = POOL_HALO - w // 2
        cols = slice(c * LANES, (c + 1) * LANES)
        if w == 2:
            return s2_ref[first:first + tm, cols]
        if w == 4:
            return s4_ref[first:first + tm, cols]
        if w == 8:
            return s8_ref[first:first + tm, :]
        return s8_ref[first:first + tm, :] + s8_ref[first + 8:first + 8 + tm, :]

    feats = []
    for c in range(2):
        w_a, w_b = POOL_WINDOWS[2 * c], POOL_WINDOWS[2 * c + 1]
        x0 = ext_ref[POOL_HALO:POOL_HALO + tm, c * LANES:(c + 1) * LANES]
        mean = jnp.where(lo, window_sum(w_a, c), window_sum(w_b, c)) * icnt_ref[:, c * LANES:(c + 1) * LANES]
        feats.append(mean - x0)
    feats = jnp.concatenate(feats, axis=1).astype(BF16)
    y_pool = jnp.dot(feats, wpool_ref[...], preferred_element_type=F32) * pscale_ref[...]

    y = y + jnp.dot(y_pool.astype(BF16), wo_ref[0:pw, :], preferred_element_type=F32)
    return x_ref[...] + mod_ref[2:3, :] * _rms(y, g_ref[...])


N_OUT_PROJ_IN = 12


def _out_proj_kernel(*refs):
    ins, o_ref, scratch = refs[:N_OUT_PROJ_IN], refs[N_OUT_PROJ_IN], refs[N_OUT_PROJ_IN + 1:]
    i = pl.program_id(1)
    o_ref[...] = _out_proj_tile(i > 0, i < pl.num_programs(1) - 1, *ins, *scratch)


def _out_proj(x, a, inv_count, o_win, o_glb, w_pool_bd, pool_scale, w_out, g_post, mod, tm):
    b, n, d = x.shape
    pw = a.shape[2]
    aw = o_win.shape[2]
    hb = tm // POOL_HALO
    row = lambda bi, i: (0, 0)
    tile = lambda bi, i: (bi, i, 0)
    return pl.pallas_call(
        _out_proj_kernel,
        out_shape=jax.ShapeDtypeStruct((b, n, d), F32),
        grid=(b, n // tm),
        in_specs=[
            pl.BlockSpec((None, tm, d), tile),
            pl.BlockSpec((None, tm, pw), tile),
            pl.BlockSpec((None, POOL_HALO, pw), lambda bi, i: (bi, jnp.maximum(i * hb - 1, 0), 0)),
            pl.BlockSpec((None, POOL_HALO, pw), lambda bi, i: (bi, jnp.minimum((i + 1) * hb, n // POOL_HALO - 1), 0)),
            pl.BlockSpec((tm, pw), lambda bi, i: (i, 0)),
            pl.BlockSpec((None, tm, aw), tile),
            pl.BlockSpec((None, tm, aw), tile),
            _resident((pw, pw), row),
            pl.BlockSpec((1, pw), row),
            _resident((d, d), row),
            pl.BlockSpec((1, d), row),
            pl.BlockSpec((None, N_MOD, d), lambda bi, i: (bi, 0, 0)),
        ],
        out_specs=pl.BlockSpec((None, tm, d), tile),
        scratch_shapes=[pltpu.VMEM((tm + 2 * POOL_HALO + POOL_HALO // 2, pw), F32),
                        pltpu.VMEM((tm + 2 * POOL_HALO, pw), F32), pltpu.VMEM((tm + 2 * POOL_HALO, pw), F32),
                        pltpu.VMEM((tm + 2 * POOL_HALO, LANES), F32)],
        compiler_params=_cparams("arbitrary", "arbitrary"),
        name="out_proj",
    )(x, a, a, a, inv_count, o_win, o_glb, w_pool_bd, pool_scale, w_out, g_post, mod)


def _ffn_tile(x, mod_ref, gpre_ref, gpost_ref, wg_ref, wu_ref, wd_ref, chunk):
    h = _rms(x, gpre_ref[...])
    h = (h * (1.0 + mod_ref[4:5, :]) + mod_ref[3:4, :]).astype(BF16)
    f = None
    for c in range(wg_ref.shape[1] // chunk):
        cols = slice(c * chunk, (c + 1) * chunk)
        g = jnp.dot(h, wg_ref[:, cols], preferred_element_type=F32)
        u = jnp.dot(h, wu_ref[:, cols], preferred_element_type=F32)
        act = ((g * jax.nn.sigmoid(g)) * u).astype(BF16)
        part = jnp.dot(act, wd_ref[cols, :], preferred_element_type=F32)
        f = part if f is None else f + part
    return x + mod_ref[5:6, :] * _rms(f, gpost_ref[...])


def _ffn_kernel(x_ref, mod_ref, gpre_ref, gpost_ref, wg_ref, wu_ref, wd_ref, o_ref, *, chunk):
    o_ref[...] = _ffn_tile(x_ref[...], mod_ref, gpre_ref, gpost_ref, wg_ref, wu_ref, wd_ref, chunk)


def _ffn(x, mod, g_pre, g_post, w_gate, w_up, w_down, tm, chunk):
    b, n, d = x.shape
    dff = w_gate.shape[1]
    row = lambda bi, i: (0, 0)
    tile = lambda bi, i: (bi, i, 0)
    return pl.pallas_call(
        functools.partial(_ffn_kernel, chunk=chunk),
        out_shape=jax.ShapeDtypeStruct((b, n, d), F32),
        grid=(b, n // tm),
        in_specs=[
            pl.BlockSpec((None, tm, d), tile),
            pl.BlockSpec((None, N_MOD, d), lambda bi, i: (bi, 0, 0)),
            pl.BlockSpec((1, d), row),
            pl.BlockSpec((1, d), row),
            _resident((d, dff), row),
            _resident((d, dff), row),
            _resident((dff, d), row),
        ],
        out_specs=pl.BlockSpec((None, tm, d), tile),
        compiler_params=_cparams("arbitrary", "arbitrary"),
        name="ffn",
    )(x, mod, g_pre, g_post, w_gate, w_up, w_down)


N_FFN_IN = 5
N_PROJ_IN = 8
N_PROJ_OUT = 7


def _tail_kernel(*refs, tpb, chunk, st, with_proj):
    refs = list(refs)
    op_in, refs = refs[:N_OUT_PROJ_IN], refs[N_OUT_PROJ_IN:]
    ffn_in, refs = refs[:N_FFN_IN], refs[N_FFN_IN:]
    if with_proj:
        (modi_ref, gin_ref, win_ref, cos_ref, sin_ref, gq_ref, gk_ref, qfill_ref), refs = refs[:N_PROJ_IN], refs[N_PROJ_IN:]
    o_ref, refs = refs[0], refs[1:]
    if with_proj:
        proj_out, refs = refs[:N_PROJ_OUT], refs[N_PROJ_OUT:]
        u_ref, refs = refs[0], refs[1:]
    pool_scratch = refs
    tm = o_ref.shape[0]
    step = pl.program_id(0)
    i = lax.rem(jnp.minimum(step, pl.num_programs(0) - (2 if with_proj else 1)), tpb)

    if with_proj:
        @pl.when(step == 0)
        def _():
            u_ref[...] = jnp.zeros(u_ref.shape, F32)

        _in_proj_epilogue(lambda sub: u_ref[sub * st:(sub + 1) * st, :], tm, st, cos_ref, sin_ref, gq_ref, gk_ref,
                          qfill_ref, *proj_out)

    x_mid = _out_proj_tile(i > 0, i < tpb - 1, *op_in, *pool_scratch)
    y = _ffn_tile(x_mid, op_in[-1], *ffn_in, chunk)
    o_ref[...] = y
    if with_proj:
        u_ref[...] = _project(y, modi_ref, gin_ref, win_ref)


def _tail(x, a, inv_count, o_win, o_glb, w_pool_bd, pool_scale, w_out, g_post, mod, g_pre_f, g_post_f, w_gate, w_up, w_down,
          proj, tm, chunk):
    b, n, d = x.shape
    pw, aw, dff = a.shape[2], o_win.shape[2], w_gate.shape[1]
    hb = tm // POOL_HALO
    tpb = n // tm
    last = b * tpb - 1
    with_proj = proj is not None
    cur = lambda s: (jnp.minimum(s, last) // tpb, jnp.minimum(s, last) % tpb)
    prev = lambda s: (jnp.maximum(s - 1, 0) // tpb, jnp.maximum(s - 1, 0) % tpb)
    row = lambda s: (0, 0)
    tile = lambda s: (cur(s)[0], cur(s)[1], 0)
    in_specs = [
        pl.BlockSpec((None, tm, d), tile),
        pl.BlockSpec((None, tm, pw), tile),
        pl.BlockSpec((None, POOL_HALO, pw), lambda s: (cur(s)[0], jnp.maximum(cur(s)[1] * hb - 1, 0), 0)),
        pl.BlockSpec((None, POOL_HALO, pw), lambda s: (cur(s)[0], jnp.minimum((cur(s)[1] + 1) * hb, n // POOL_HALO - 1), 0)),
        pl.BlockSpec((tm, pw), lambda s: (cur(s)[1], 0)),
        pl.BlockSpec((None, tm, aw), tile),
        pl.BlockSpec((None, tm, aw), tile),
        _resident((pw, pw), row),
        pl.BlockSpec((1, pw), row),
        _resident((d, d), row),
        pl.BlockSpec((1, d), row),
        pl.BlockSpec((None, N_MOD, d), lambda s: (cur(s)[0], 0, 0)),
        pl.BlockSpec((1, d), row),
        pl.BlockSpec((1, d), row),
        _resident((d, dff), row),
        _resident((d, dff), row),
        _resident((dff, d), row),
    ]
    args = [x, a, a, a, inv_count, o_win, o_glb, w_pool_bd, pool_scale, w_out, g_post, mod,
            g_pre_f, g_post_f, w_gate, w_up, w_down]
    out_shape = (jax.ShapeDtypeStruct((b, n, d), F32),)
    out_specs = (pl.BlockSpec((None, tm, d), tile),)
    scratch = [pltpu.VMEM((tm + 2 * POOL_HALO + POOL_HALO // 2, pw), F32),
               pltpu.VMEM((tm + 2 * POOL_HALO, pw), F32), pltpu.VMEM((tm + 2 * POOL_HALO, pw), F32),
               pltpu.VMEM((tm + 2 * POOL_HALO, LANES), F32)]
    if with_proj:
        mod_i, g_pre_i, w_in, cos, sin, gq, gk, qfill = proj
        nw = w_in.shape[1]
        in_specs += [
            pl.BlockSpec((None, N_MOD, d), lambda s: (cur(s)[0], 0, 0)),
            pl.BlockSpec((1, d), row),
            _resident((d, nw), row),
            pl.BlockSpec((tm, LANES), lambda s: (prev(s)[1], 0)),
            pl.BlockSpec((tm, LANES), lambda s: (prev(s)[1], 0)),
            pl.BlockSpec((1, LANES), row),
            pl.BlockSpec((1, LANES), row),
            pl.BlockSpec((1, LANES), row),
        ]
        args += [mod_i, g_pre_i, w_in, cos, sin, gq, gk, qfill]
        proj_shape, proj_specs = _in_proj_outputs(b, n, tm, prev)
        out_shape += proj_shape
        out_specs += proj_specs
        scratch = [pltpu.VMEM((tm, nw), F32)] + scratch
    outs = pl.pallas_call(
        functools.partial(_tail_kernel, tpb=tpb, chunk=chunk, st=min(tm, 256), with_proj=with_proj),
        out_shape=out_shape,
        grid=(b * tpb + (1 if with_proj else 0),),
        in_specs=in_specs,
        out_specs=out_specs,
        scratch_shapes=scratch,
        compiler_params=_cparams("arbitrary"),
        name="tail_proj" if with_proj else "tail",
    )(*args)
    return outs[0], outs[1:]


def _rope_tables(n):
    rows = n // GRID_W
    row = jnp.broadcast_to(jnp.arange(rows)[:, None], (rows, GRID_W)).reshape(-1).astype(F32)
    col = jnp.broadcast_to(jnp.arange(GRID_W)[None, :], (rows, GRID_W)).reshape(-1).astype(F32)
    freq = ROPE_THETA ** (-jnp.arange(ROPE_FREQS, dtype=F32) / ROPE_FREQS)
    ar, ac = row[:, None] * freq, col[:, None] * freq
    cos = jnp.concatenate([jnp.cos(ar), jnp.cos(ar), jnp.cos(ac), jnp.cos(ac)], axis=1)
    sin = jnp.concatenate([-jnp.sin(ar), jnp.sin(ar), -jnp.sin(ac), jnp.sin(ac)], axis=1)
    return jnp.tile(cos, (1, LANES // HEAD_DIM)), jnp.tile(sin, (1, LANES // HEAD_DIM))


def _shift_row(m):
    pieces, rest = [], m.astype(F32)
    for _ in range(N_SHIFT):
        part = rest.astype(BF16).astype(F32)
        pieces.append(-part)
        rest = rest - part
    return jnp.zeros((1, LANES), F32).at[0, HEAD_DIM:HEAD_DIM + N_SHIFT].set(jnp.stack(pieces))


def _pool_inv_counts(n):
    t = jnp.arange(n)
    cols = []
    for w in POOL_WINDOWS:
        hi = jnp.clip(t + (w - 1 - w // 2), 0, n - 1)
        lo = jnp.clip(t - w // 2, 0, n - 1)
        cols.append(jnp.broadcast_to((1.0 / (hi - lo + 1).astype(F32))[:, None], (n, HEAD_DIM)))
    return jnp.concatenate(cols, axis=1)


def _tile(n, pref):
    t = min(n, pref)
    assert n % t == 0, (n, t)
    return t


def kernel(x, c, ctx, c_ctx, w_mod, b_mod, g_pre_mix, g_post_mix, g_pre_ffn, g_post_ffn, w_in, w_pool, pool_scale, win_sink, g_qnorm, g_knorm, w_out, w_gate, w_up, w_down):
    b, n, d = x.shape
    lc = ctx.shape[1]
    depth = w_mod.shape[0]
    assert w_in.shape[2] == IN_WIDTH and KV_WIDTH == LANES and n % GRID_W == 0

    cos, sin = _rope_tables(n)
    cos_c, sin_c = jnp.ones((lc, LANES), F32), jnp.zeros((lc, LANES), F32)

    mod_rows = -(-(b + 1) // 8) * 8
    cc = jnp.concatenate([c, c_ctx[None, :], jnp.zeros((mod_rows - b - 1, d), F32)], axis=0)
    mod_all = _modulation(cc, w_mod, b_mod)

    tm = _tile(n, 512)
    tq_win = _tile(n, 1024)
    tq_glb = _tile(n, 512)
    tk_glb = n // (N_KV_HEADS * (tq_glb // WINDOW))
    assert tk_glb % LANES == 0
    rb_glb = 512
    ffn_chunk = 256

    def layer(l):
        bound = ATTN_SCALE * HEAD_DIM * jnp.max(jnp.abs(g_qnorm[l])) * jnp.max(jnp.abs(g_knorm[l]))
        return dict(
            m=mod_all[l, :b].reshape(b, N_MOD, d),
            mc=jnp.broadcast_to(mod_all[l, b].reshape(1, N_MOD, d), (b, N_MOD, d)),
            w_in=w_in[l].astype(BF16),
            gq=jnp.tile(g_qnorm[l], LANES // HEAD_DIM)[None, :],
            gk=jnp.tile(g_knorm[l], LANES // HEAD_DIM)[None, :],
            g_pre=g_pre_mix[l][None, :],
            bound=bound,
            qfill=jnp.where(bound <= SAFE_SHIFT, _shift_row(bound), 0.0),
        )

    no_fill = jnp.zeros((1, LANES), F32)
    icnt, icnt_c = _pool_inv_counts(n), _pool_inv_counts(lc)
    xc = ctx
    p = layer(0)
    proj = _in_proj(x, p["m"], p["g_pre"], p["w_in"], cos, sin, p["gq"], p["gk"], p["qfill"], tm, carried=True)
    for l in range(depth):
        last = l == depth - 1
        m, mc, gq, gk = p["m"], p["mc"], p["gq"], p["gk"]
        w_out_l = w_out[l].astype(BF16)
        w_gate_l, w_up_l, w_down_l = w_gate[l].astype(BF16), w_up[l].astype(BF16), w_down[l].astype(BF16)
        w_pool_bd = jax.scipy.linalg.block_diag(*[w_pool[l, g] for g in range(len(POOL_WINDOWS))]).astype(BF16)
        pscale = pool_scale[l][None, :]
        sink = jnp.zeros((8, LANES), F32).at[:N_Q_HEADS].set(jnp.broadcast_to(win_sink[l][:, None], (N_Q_HEADS, LANES)))
        g_post = g_post_mix[l][None, :]
        g_pre_f, g_post_f = g_pre_ffn[l][None, :], g_post_ffn[l][None, :]

        a, qw, ktw, vw, qg, ktg, vg = proj
        ac, qwc, ktwc, vwc, qgc, ktgc, vgc = _in_proj(xc, mc, p["g_pre"], p["w_in"], cos_c, sin_c, gq, gk, no_fill, lc)

        def attn_bounded(*ops):
            o_g, o_w = _attn(*ops, tq=tq_glb, tk=tk_glb, rb=rb_glb)
            return o_w, o_g

        def attn_exact(qg, ktg, vg, ktgc, vgc, qw, ktw, vw, ktwc, vwc, sink):
            return (_window_attn(qw, ktw, vw, ktwc, vwc, sink, tq_win),
                    _global_attn(qg, ktg, vg, ktgc, vgc, tq=tq_glb, tk=tk_glb, rb=rb_glb))

        o_win, o_glb = lax.cond(p["bound"] <= SAFE_SHIFT, attn_bounded, attn_exact,
                                qg, ktg, vg, ktgc, vgc, qw, ktw, vw, ktwc, vwc, sink)
        tail_args = (x, a, icnt, o_win, o_glb, w_pool_bd, pscale, w_out_l, g_post, m, g_pre_f, g_post_f,
                     w_gate_l, w_up_l, w_down_l)
        if last:
            x, _ = _tail(*tail_args, None, tm, ffn_chunk)
        else:
            oc_win = _ctx_attn(qwc, ktwc, vwc, sink)
            oc_glb = _ctx_attn(qgc, ktgc, vgc, None)
            xc = _out_proj(xc, ac, icnt_c, oc_win, oc_glb, w_pool_bd, pscale, w_out_l, g_post, mc, lc)
            xc = _ffn(xc, mc, g_pre_f, g_post_f, w_gate_l, w_up_l, w_down_l, lc, ffn_chunk)
            p = layer(l + 1)
            x, proj = _tail(*tail_args, (p["m"], p["g_pre"], p["w_in"], cos, sin, p["gq"], p["gk"], p["qfill"]),
                            tm, ffn_chunk)
    return x
```

```python
import functools

import jax
import jax.numpy as jnp
from jax import lax
from jax.experimental import pallas as pl
from jax.experimental.pallas import tpu as pltpu

F32 = jnp.float32
BF16 = jnp.bfloat16

HEAD_DIM = 64
GRID_W = 64
ROPE_FREQS = HEAD_DIM // 4
ROPE_THETA = 10000.0
NORM_EPS = 1e-6
NEG_INF = -1e30
ATTN_SCALE = HEAD_DIM ** -0.5
WINDOW = 128
POOL_WINDOWS = (2, 4, 8, 16)
POOL_HALO = 16
N_Q_HEADS = 6
N_KV_HEADS = 2
KV_GROUP = N_Q_HEADS // N_KV_HEADS
N_MOD = 6
POOL_WIDTH = len(POOL_WINDOWS) * HEAD_DIM
Q_WIDTH = N_Q_HEADS * HEAD_DIM
KV_WIDTH = N_KV_HEADS * HEAD_DIM
COL_QW = POOL_WIDTH
COL_KW = COL_QW + Q_WIDTH
COL_VW = COL_KW + KV_WIDTH
COL_QG = COL_VW + KV_WIDTH
COL_KG = COL_QG + Q_WIDTH
COL_VG = COL_KG + KV_WIDTH
IN_WIDTH = COL_VG + KV_WIDTH
N_SHIFT = 3
SAFE_SHIFT = 30.0

LANES = 128
VMEM_LIMIT = 56 * 1024 * 1024


def _cparams(*sem):
    return pltpu.CompilerParams(dimension_semantics=sem, vmem_limit_bytes=VMEM_LIMIT)


def _resident(block_shape, index_map):
    return pl.BlockSpec(block_shape, index_map, pipeline_mode=pl.Buffered(1))


def _layer_spec(wl):
    w, l = wl
    return pl.BlockSpec((None,) + w.shape[1:], lambda *g: (l,) + (0,) * (w.ndim - 1), pipeline_mode=pl.Buffered(1))


def _rms(x, g):
    ms = jnp.mean(x * x, axis=-1, keepdims=True)
    return (x * lax.rsqrt(ms + NORM_EPS)) * g


def _mod_kernel(c_ref, w_ref, b_ref, o_ref):
    c = c_ref[...]
    s = c * jax.nn.sigmoid(c)
    o_ref[...] = jnp.dot(s.astype(BF16), w_ref[...].astype(BF16), preferred_element_type=F32) + b_ref[...]


def _modulation(cc, w_mod, b_mod):
    depth, d, dm = w_mod.shape
    rows = cc.shape[0]
    tn = dm // 4
    return pl.pallas_call(
        _mod_kernel,
        out_shape=jax.ShapeDtypeStruct((depth, rows, dm), F32),
        grid=(depth, dm // tn),
        in_specs=[
            pl.BlockSpec((rows, d), lambda l, j: (0, 0)),
            pl.BlockSpec((None, d, tn), lambda l, j: (l, 0, j)),
            pl.BlockSpec((None, 1, tn), lambda l, j: (l, 0, j)),
        ],
        out_specs=pl.BlockSpec((None, rows, tn), lambda l, j: (l, 0, j)),
        compiler_params=_cparams("arbitrary", "arbitrary"),
        name="modulation",
    )(cc, w_mod, b_mod.reshape(depth, 1, dm))


def _in_proj_epilogue(u_of, n_rows, st, cos_ref, sin_ref, gq_ref, gk_ref, qfill_ref,
                      a_ref, qw_ref, ktw_ref, vw_ref, qg_ref, ktg_ref, vg_ref):
    lane = lax.broadcasted_iota(jnp.int32, (st, LANES), 1)
    lo = lane < HEAD_DIM
    first = (lane & ROPE_FREQS) == 0
    shift_rows = (lax.broadcasted_iota(jnp.int32, (HEAD_DIM, LANES), 0) < N_SHIFT).astype(BF16)
    sum_lane = (lane == HEAD_DIM).astype(F32)
    gq, gk = gq_ref[...], gk_ref[...]
    qfill = qfill_ref[...]

    def head_norm(t, g):
        sq = t * t
        s_lo = jnp.sum(jnp.where(lo, sq, 0.0), axis=-1, keepdims=True)
        s_hi = jnp.sum(jnp.where(lo, 0.0, sq), axis=-1, keepdims=True)
        ms = jnp.where(lo, s_lo, s_hi) * (1.0 / HEAD_DIM)
        return (t * lax.rsqrt(ms + NORM_EPS)) * g

    for sub in range(n_rows // st):
        rs = slice(sub * st, (sub + 1) * st)
        u = u_of(sub)
        cos = cos_ref[rs, :]
        sin = sin_ref[rs, :]

        def rope(t):
            partner = jnp.where(first, pltpu.roll(t, LANES - ROPE_FREQS, 1), pltpu.roll(t, ROPE_FREQS, 1))
            return t * cos + partner * sin

        def store_branch(q_chunks, k, v, fill, q_ref, kt_ref, v_ref):
            for hd in range(N_Q_HEADS):
                t = q_chunks[hd // 2]
                if hd % 2:
                    t = pltpu.roll(t, HEAD_DIM, 1)
                q_ref[hd, rs, :] = jnp.where(lo, t, fill).astype(BF16)
            for r in range(st // LANES):
                kb = sub * (st // LANES) + r
                kt = jnp.transpose(k[r * LANES:(r + 1) * LANES, :]).astype(BF16)
                for kv in range(N_KV_HEADS):
                    kt_ref[kv, kb, 0:HEAD_DIM, :] = kt[kv * HEAD_DIM:(kv + 1) * HEAD_DIM, :]
                    kt_ref[kv, kb, HEAD_DIM:LANES, :] = shift_rows
            v_ref[0, rs, :] = jnp.where(lo, v, sum_lane).astype(BF16)
            v_ref[1, rs, :] = jnp.where(lo, pltpu.roll(v, HEAD_DIM, 1), sum_lane).astype(BF16)

        def cols(start, width):
            return u[:, start:start + width]

        a_ref[rs, :] = cols(0, POOL_WIDTH)
        store_branch([rope(cols(COL_QW + LANES * c, LANES)) * ATTN_SCALE for c in range(Q_WIDTH // LANES)],
                     rope(cols(COL_KW, KV_WIDTH)), cols(COL_VW, KV_WIDTH), 0.0, qw_ref, ktw_ref, vw_ref)
        store_branch([rope(head_norm(cols(COL_QG + LANES * c, LANES), gq)) * ATTN_SCALE for c in range(Q_WIDTH // LANES)],
                     rope(head_norm(cols(COL_KG, KV_WIDTH), gk)), cols(COL_VG, KV_WIDTH), qfill, qg_ref, ktg_ref, vg_ref)


def _project(x, mod_ref, g_ref, w_ref):
    h = _rms(x, g_ref[...])
    h = h * (1.0 + mod_ref[1:2, :]) + mod_ref[0:1, :]
    return jnp.dot(h.astype(BF16), w_ref[...], preferred_element_type=F32)


def _in_proj_kernel(x_ref, mod_ref, g_ref, w_ref, cos_ref, sin_ref, gq_ref, gk_ref, qfill_ref, *out_refs, st):
    tm = x_ref.shape[0]
    us = [_project(x_ref[sub * st:(sub + 1) * st, :], mod_ref, g_ref, w_ref) for sub in range(tm // st)]
    _in_proj_epilogue(lambda sub: us[sub], tm, st, cos_ref, sin_ref, gq_ref, gk_ref, qfill_ref, *out_refs)


def _in_proj_outputs(b, n, tm, tile_of):
    qshape = jax.ShapeDtypeStruct((b, N_Q_HEADS, n, LANES), BF16)
    ktshape = jax.ShapeDtypeStruct((b, N_KV_HEADS, n // LANES, LANES, LANES), BF16)
    vshape = jax.ShapeDtypeStruct((b, N_KV_HEADS, n, LANES), BF16)

    def spec(block, where):
        return pl.BlockSpec(block, lambda *g: where(*tile_of(*g)))

    aspec = spec((None, tm, POOL_WIDTH), lambda bi, i: (bi, i, 0))
    qspec = spec((None, N_Q_HEADS, tm, LANES), lambda bi, i: (bi, 0, i, 0))
    ktspec = spec((None, N_KV_HEADS, tm // LANES, LANES, LANES), lambda bi, i: (bi, 0, i, 0, 0))
    vspec = spec((None, N_KV_HEADS, tm, LANES), lambda bi, i: (bi, 0, i, 0))
    shapes = (jax.ShapeDtypeStruct((b, n, POOL_WIDTH), F32), qshape, ktshape, vshape, qshape, ktshape, vshape)
    return shapes, (aspec, qspec, ktspec, vspec, qspec, ktspec, vspec)


def _in_proj_carried_kernel(x_ref, mod_ref, g_ref, w_ref, cos_ref, sin_ref, gq_ref, gk_ref, qfill_ref, *refs, st):
    out_refs, u_ref = refs[:-1], refs[-1]

    @pl.when(pl.program_id(0) == 0)
    def _():
        u_ref[...] = jnp.zeros(u_ref.shape, F32)

    _in_proj_epilogue(lambda sub: u_ref[sub * st:(sub + 1) * st, :], x_ref.shape[0], st, cos_ref, sin_ref,
                      gq_ref, gk_ref, qfill_ref, *out_refs)
    u_ref[...] = _project(x_ref[...], mod_ref, g_ref, w_ref)


def _in_proj(x, mod, g_pre, w_in, cos, sin, gq, gk, qfill, tm, carried=False):
    b, n, d = x.shape
    nw = w_in[0].shape[2]
    tpb = n // tm
    last = b * tpb - 1
    if carried:
        grid = (b * tpb + 1,)
        cur = lambda s: (jnp.minimum(s, last) // tpb, jnp.minimum(s, last) % tpb)
        prev = lambda s: (jnp.maximum(s - 1, 0) // tpb, jnp.maximum(s - 1, 0) % tpb)
        kern, scratch, sem = _in_proj_carried_kernel, [pltpu.VMEM((tm, nw), F32)], ("arbitrary",)
    else:
        grid = (b, tpb)
        cur = prev = lambda bi, i: (bi, i)
        kern, scratch, sem = _in_proj_kernel, [], ("arbitrary", "arbitrary")
    out_shape, out_specs = _in_proj_outputs(b, n, tm, prev)
    row = lambda *g: (0, 0)
    return pl.pallas_call(
        functools.partial(kern, st=min(tm, 256)),
        out_shape=out_shape,
        grid=grid,
        in_specs=[
            pl.BlockSpec((None, tm, d), lambda *g: (cur(*g)[0], cur(*g)[1], 0)),
            pl.BlockSpec((None, N_MOD, d), lambda *g: (cur(*g)[0], 0, 0)),
            pl.BlockSpec((1, d), row),
            _layer_spec(w_in),
            pl.BlockSpec((tm, LANES), lambda *g: (prev(*g)[1], 0)),
            pl.BlockSpec((tm, LANES), lambda *g: (prev(*g)[1], 0)),
            pl.BlockSpec((1, LANES), row),
            pl.BlockSpec((1, LANES), row),
            pl.BlockSpec((1, LANES), row),
        ],
        out_specs=out_specs,
        scratch_shapes=scratch,
        compiler_params=_cparams(*sem),
        name="in_proj",
    )(x, mod, g_pre, w_in[0], cos, sin, gq, gk, qfill)


def _merge_heads(outs, lo):
    return [jnp.where(lo, outs[2 * c], pltpu.roll(outs[2 * c + 1], HEAD_DIM, 1)) for c in range(N_Q_HEADS // 2)]


def _key_tile(kt_ref, first_block, n_blocks):
    return jnp.concatenate([kt_ref[first_block + j] for j in range(n_blocks)], axis=1)


def _stacked_sink(sink_ref, kv, rows_per_head):
    return jnp.concatenate([jnp.broadcast_to(sink_ref[hd:hd + 1, 0:1], (rows_per_head, 1))
                            for hd in range(kv * KV_GROUP, (kv + 1) * KV_GROUP)], axis=0)


def _window_units(q_ref, kt_ref, v_ref, ktc_ref, vc_ref, sink_ref, first_blk):
    nkb = kt_ref.shape[1]
    sub = WINDOW
    rows = KV_GROUP * sub
    r_in = lax.broadcasted_iota(jnp.int32, (rows, LANES), 0) & (sub - 1)
    c_in = lax.broadcasted_iota(jnp.int32, (rows, LANES), 1)
    ctx = [(_key_tile(ktc_ref.at[kv], 0, ktc_ref.shape[1]), vc_ref[kv], _stacked_sink(sink_ref, kv, sub))
           for kv in range(N_KV_HEADS)]

    def blocks(j):
        blk = first_blk + j
        return blk, jnp.maximum(blk - 1, 0), jnp.minimum(blk + 1, nkb - 1)

    def logits(j, kv):
        blk, prev_blk, next_blk = blocks(j)
        q = q_ref[kv * KV_GROUP:(kv + 1) * KV_GROUP, pl.ds(pl.multiple_of(j * sub, sub), sub), :].reshape(rows, LANES)
        kt = jnp.concatenate([kt_ref[kv, prev_blk], kt_ref[kv, blk], kt_ref[kv, next_blk]], axis=1)
        s = jnp.dot(q, kt, preferred_element_type=F32)
        sc = jnp.dot(q, ctx[kv][0], preferred_element_type=F32)
        return s, sc

    def attend(j, kv, s, sc):
        blk, prev_blk, next_blk = blocks(j)
        _, vc, sk = ctx[kv]
        v = jnp.concatenate([v_ref[kv, pl.ds(pl.multiple_of(kb * sub, sub), sub), :]
                             for kb in (prev_blk, blk, next_blk)], axis=0)
        s_prev = jnp.where(jnp.logical_and(c_in >= r_in, blk > 0), s[:, 0:sub], NEG_INF)
        s_own = s[:, sub:2 * sub]
        s_next = jnp.where(jnp.logical_and(c_in <= r_in, blk < nkb - 1), s[:, 2 * sub:3 * sub], NEG_INF)
        mx = jnp.maximum(jnp.maximum(s_prev, s_own), s_next)
        for t in range(sc.shape[1] // LANES):
            mx = jnp.maximum(mx, sc[:, t * LANES:(t + 1) * LANES])
        m = jnp.maximum(jnp.max(mx, axis=-1, keepdims=True), sk)
        p = jnp.concatenate([jnp.exp(s_prev - m), jnp.exp(s_own - m), jnp.exp(s_next - m)], axis=1)
        pv = jnp.dot(p.astype(BF16), v, preferred_element_type=F32)
        pv = pv + jnp.dot(jnp.exp(sc - m).astype(BF16), vc, preferred_element_type=F32)
        l = pv[:, HEAD_DIM:HEAD_DIM + 1] + jnp.exp(sk - m)
        o = pv / l
        return [o[g * sub:(g + 1) * sub] for g in range(KV_GROUP)]

    return logits, attend


def _store_heads(o_ref, row0, outs):
    nrows = outs[0].shape[0]
    lo = lax.broadcasted_iota(jnp.int32, (nrows, LANES), 1) < HEAD_DIM
    for c, chunk in enumerate(_merge_heads(outs, lo)):
        o_ref[pl.ds(row0, nrows), c * LANES:(c + 1) * LANES] = chunk.astype(BF16)


def _window_attn_kernel(q_ref, kt_ref, v_ref, ktc_ref, vc_ref, sink_ref, o_ref):
    tq = q_ref.shape[1]
    sub = WINDOW
    logits, attend = _window_units(q_ref, kt_ref, v_ref, ktc_ref, vc_ref, sink_ref, pl.program_id(1) * (tq // sub))
    units = [(j, kv) for j in range(tq // sub) for kv in range(N_KV_HEADS)]
    pending = logits(*units[0])
    outs = []
    for idx, (j, kv) in enumerate(units):
        ahead = logits(*units[idx + 1]) if idx + 1 < len(units) else None
        outs += attend(j, kv, *pending)
        pending = ahead
        if kv == N_KV_HEADS - 1:
            _store_heads(o_ref, j * sub, outs)
            outs = []


def _window_attn(q, kt, v, ktc, vc, sink, tq):
    b, _, n, _ = q.shape
    lc = vc.shape[2]
    assert tq % WINDOW == 0 and WINDOW == LANES
    batch4 = lambda bi, i: (bi, 0, 0, 0)
    batch5 = lambda bi, i: (bi, 0, 0, 0, 0)
    return pl.pallas_call(
        _window_attn_kernel,
        out_shape=jax.ShapeDtypeStruct((b, n, N_Q_HEADS * HEAD_DIM), BF16),
        grid=(b, n // tq),
        in_specs=[
            pl.BlockSpec((None, N_Q_HEADS, tq, LANES), lambda bi, i: (bi, 0, i, 0)),
            pl.BlockSpec((None, N_KV_HEADS, n // LANES, LANES, LANES), batch5),
            pl.BlockSpec((None, N_KV_HEADS, n, LANES), batch4),
            pl.BlockSpec((None, N_KV_HEADS, lc // LANES, LANES, LANES), batch5),
            pl.BlockSpec((None, N_KV_HEADS, lc, LANES), batch4),
            pl.BlockSpec(sink.shape, lambda bi, i: (0, 0)),
        ],
        out_specs=pl.BlockSpec((None, tq, N_Q_HEADS * HEAD_DIM), lambda bi, i: (bi, i, 0)),
        compiler_params=_cparams("arbitrary", "arbitrary"),
        name="window_attn",
    )(q, kt, v, ktc, vc, sink)


def _ctx_attn_kernel(*refs, has_sink):
    if has_sink:
        q_ref, kt_ref, v_ref, sink_ref, o_ref = refs
    else:
        q_ref, kt_ref, v_ref, o_ref = refs
    tq = q_ref.shape[1]
    rows = KV_GROUP * tq
    outs = []
    for kv in range(N_KV_HEADS):
        q = q_ref[kv * KV_GROUP:(kv + 1) * KV_GROUP].reshape(rows, LANES)
        s = jnp.dot(q, _key_tile(kt_ref.at[kv], 0, kt_ref.shape[1]), preferred_element_type=F32)
        m = jnp.max(s, axis=-1, keepdims=True)
        if has_sink:
            sk = _stacked_sink(sink_ref, kv, tq)
            m = jnp.maximum(m, sk)
        pv = jnp.dot(jnp.exp(s - m).astype(BF16), v_ref[kv], preferred_element_type=F32)
        l = pv[:, HEAD_DIM:HEAD_DIM + 1]
        if has_sink:
            l = l + jnp.exp(sk - m)
        o = pv / l
        outs += [o[g * tq:(g + 1) * tq] for g in range(KV_GROUP)]
    _store_heads(o_ref, 0, outs)


def _ctx_attn(q, kt, v, sink):
    b, _, n, _ = q.shape
    args = [q, kt, v]
    in_specs = [pl.BlockSpec((None,) + a.shape[1:], lambda bi, nd=a.ndim: (bi,) + (0,) * (nd - 1)) for a in args]
    if sink is not None:
        args.append(sink)
        in_specs.append(pl.BlockSpec(sink.shape, lambda bi: (0, 0)))
    return pl.pallas_call(
        functools.partial(_ctx_attn_kernel, has_sink=sink is not None),
        out_shape=jax.ShapeDtypeStruct((b, n, N_Q_HEADS * HEAD_DIM), BF16),
        grid=(b,),
        in_specs=in_specs,
        out_specs=pl.BlockSpec((None, n, N_Q_HEADS * HEAD_DIM), lambda bi: (bi, 0, 0)),
        compiler_params=_cparams("arbitrary"),
        name="ctx_attn",
    )(*args)


def _global_attn_kernel(q_ref, kt_ref, v_ref, ktc_ref, vc_ref, o_ref, qs_ref, acc_ref, mx_ref, *, tk, rb):
    tq = q_ref.shape[1]
    rows = KV_GROUP * tq
    kb_per_step = tk // LANES
    n_steps = kt_ref.shape[1] // kb_per_step
    lane = lax.broadcasted_iota(jnp.int32, (rows, LANES), 1)

    def key_steps(fn):
        def step(kc, carry):
            for kv in range(N_KV_HEADS):
                fn(kv, _key_tile(kt_ref.at[kv], kc * kb_per_step, kb_per_step),
                   v_ref[kv, pl.ds(pl.multiple_of(kc * tk, tk), tk), :])
            return carry
        lax.fori_loop(0, n_steps, step, 0)
        for kv in range(N_KV_HEADS):
            fn(kv, _key_tile(ktc_ref.at[kv], 0, ktc_ref.shape[1]), vc_ref[kv])

    def q_rows(kv):
        return q_ref[kv * KV_GROUP:(kv + 1) * KV_GROUP].reshape(rows, LANES)

    mx_ref[...] = jnp.full(mx_ref.shape, NEG_INF, F32)

    def track_max(kv, kt, v):
        s = jnp.dot(q_rows(kv), kt, preferred_element_type=F32)
        mx = mx_ref[kv]
        for j in range(kt.shape[1] // LANES):
            mx = jnp.maximum(mx, s[:, j * LANES:(j + 1) * LANES])
        mx_ref[kv] = mx

    key_steps(track_max)

    for kv in range(N_KV_HEADS):
        qs = q_rows(kv).astype(F32)
        rest = jnp.max(mx_ref[kv], axis=-1, keepdims=True)
        for piece in range(N_SHIFT):
            part = rest.astype(BF16).astype(F32)
            qs = jnp.where(lane == HEAD_DIM + piece, -part, qs)
            rest = rest - part
        qs_ref[kv] = qs.astype(BF16)

    acc_ref[...] = jnp.zeros(acc_ref.shape, F32)

    def accumulate(kv, kt, v):
        for r in range(rows // rb):
            rsl = slice(r * rb, (r + 1) * rb)
            s = jnp.dot(qs_ref[kv, rsl, :], kt, preferred_element_type=F32)
            p = jnp.exp(s).astype(BF16)
            acc_ref[kv, rsl, :] += jnp.dot(p, v, preferred_element_type=F32)

    key_steps(accumulate)

    outs = []
    for hd in range(N_Q_HEADS):
        a = acc_ref[hd // KV_GROUP, (hd % KV_GROUP) * tq:(hd % KV_GROUP + 1) * tq, :]
        outs.append(a / a[:, HEAD_DIM:HEAD_DIM + 1])
    _store_heads(o_ref, 0, outs)


def _global_attn(q, kt, v, ktc, vc, *, tq, tk, rb):
    b, _, n, _ = q.shape
    lc = vc.shape[2]
    rows = KV_GROUP * tq
    batch4 = lambda bi, i: (bi, 0, 0, 0)
    batch5 = lambda bi, i: (bi, 0, 0, 0, 0)
    return pl.pallas_call(
        functools.partial(_global_attn_kernel, tk=tk, rb=rb),
        out_shape=jax.ShapeDtypeStruct((b, n, N_Q_HEADS * HEAD_DIM), BF16),
        grid=(b, n // tq),
        in_specs=[
            pl.BlockSpec((None, N_Q_HEADS, tq, LANES), lambda bi, i: (bi, 0, i, 0)),
            pl.BlockSpec((None, N_KV_HEADS, n // LANES, LANES, LANES), batch5),
            pl.BlockSpec((None, N_KV_HEADS, n, LANES), batch4),
            pl.BlockSpec((None, N_KV_HEADS, lc // LANES, LANES, LANES), batch5),
            pl.BlockSpec((None, N_KV_HEADS, lc, LANES), batch4),
        ],
        out_specs=pl.BlockSpec((None, tq, N_Q_HEADS * HEAD_DIM), lambda bi, i: (bi, i, 0)),
        scratch_shapes=[pltpu.VMEM((N_KV_HEADS, rows, LANES), BF16), pltpu.VMEM((N_KV_HEADS, rows, LANES), F32),
                        pltpu.VMEM((N_KV_HEADS, rows, LANES), F32)],
        compiler_params=_cparams("arbitrary", "arbitrary"),
        name="global_attn_exact_max",
    )(q, kt, v, ktc, vc)


def _attn_kernel(qg_ref, ktg_ref, vg_ref, ktgc_ref, vgc_ref, qw_ref, ktw_ref, vw_ref, ktwc_ref, vwc_ref, sink_ref,
                 og_ref, ow_ref, acc_ref, *, tk, rb):
    tq = qg_ref.shape[1]
    rows = KV_GROUP * tq
    sub = WINDOW
    n_sub = tq // sub
    kb_per_step = tk // LANES
    steps_per_sub = ktg_ref.shape[1] // kb_per_step // n_sub
    assert steps_per_sub * n_sub * kb_per_step == ktg_ref.shape[1] and tq % rb == 0

    def global_keys(kv, kt, v, assign):
        for r in range(rows // rb):
            hd, off = kv * KV_GROUP + (r * rb) // tq, (r * rb) % tq
            s = jnp.dot(qg_ref[hd, off:off + rb, :], kt, preferred_element_type=F32)
            pv = jnp.dot(jnp.exp(s).astype(BF16), v, preferred_element_type=F32)
            if assign:
                acc_ref[kv, r * rb:(r + 1) * rb, :] = pv
            else:
                acc_ref[kv, r * rb:(r + 1) * rb, :] += pv

    def global_step(kc):
        for kv in range(N_KV_HEADS):
            global_keys(kv, _key_tile(ktg_ref.at[kv], kc * kb_per_step, kb_per_step),
                        vg_ref[kv, pl.ds(pl.multiple_of(kc * tk, tk), tk), :], False)

    for kv in range(N_KV_HEADS):
        global_keys(kv, _key_tile(ktgc_ref.at[kv], 0, ktgc_ref.shape[1]), vgc_ref[kv], True)

    logits, attend = _window_units(qw_ref, ktw_ref, vw_ref, ktwc_ref, vwc_ref, sink_ref, pl.program_id(1) * n_sub)

    def body(j, carry):
        pending = logits(j, 0)
        outs = []
        for kv in range(N_KV_HEADS):
            for t in range(steps_per_sub // N_KV_HEADS):
                global_step(j * steps_per_sub + kv * (steps_per_sub // N_KV_HEADS) + t)
            ahead = logits(j, kv + 1) if kv + 1 < N_KV_HEADS else None
            outs += attend(j, kv, *pending)
            pending = ahead
        _store_heads(ow_ref, pl.multiple_of(j * sub, sub), outs)
        return carry

    assert steps_per_sub % N_KV_HEADS == 0
    lax.fori_loop(0, n_sub, body, 0, unroll=True)

    outs = []
    for hd in range(N_Q_HEADS):
        a = acc_ref[hd // KV_GROUP, (hd % KV_GROUP) * tq:(hd % KV_GROUP + 1) * tq, :]
        outs.append(a / a[:, HEAD_DIM:HEAD_DIM + 1])
    _store_heads(og_ref, 0, outs)


def _attn(qg, ktg, vg, ktgc, vgc, qw, ktw, vw, ktwc, vwc, sink, *, tq, tk, rb):
    b, _, n, _ = qg.shape
    lc = vgc.shape[2]
    batch4 = lambda bi, i: (bi, 0, 0, 0)
    batch5 = lambda bi, i: (bi, 0, 0, 0, 0)
    branch_specs = [
        pl.BlockSpec((None, N_Q_HEADS, tq, LANES), lambda bi, i: (bi, 0, i, 0)),
        _resident((None, N_KV_HEADS, n // LANES, LANES, LANES), batch5),
        _resident((None, N_KV_HEADS, n, LANES), batch4),
        _resident((None, N_KV_HEADS, lc // LANES, LANES, LANES), batch5),
        _resident((None, N_KV_HEADS, lc, LANES), batch4),
    ]
    oshape = jax.ShapeDtypeStruct((b, n, N_Q_HEADS * HEAD_DIM), BF16)
    ospec = pl.BlockSpec((None, tq, N_Q_HEADS * HEAD_DIM), lambda bi, i: (bi, i, 0))
    return pl.pallas_call(
        functools.partial(_attn_kernel, tk=tk, rb=rb),
        out_shape=(oshape, oshape),
        grid=(b, n // tq),
        in_specs=branch_specs + branch_specs + [pl.BlockSpec(sink.shape, lambda bi, i: (0, 0))],
        out_specs=(ospec, ospec),
        scratch_shapes=[pltpu.VMEM((N_KV_HEADS, KV_GROUP * tq, LANES), F32)],
        compiler_params=_cparams("arbitrary", "arbitrary"),
        name="attn",
    )(qg, ktg, vg, ktgc, vgc, qw, ktw, vw, ktwc, vwc, sink)


def _out_proj_tile(has_prev, has_next, x_ref, a_ref, ap_ref, an_ref, icnt_ref, ow_ref, og_ref, wpool_ref, pscale_ref,
                   wo_ref, g_ref, mod_ref, ext_ref, s2_ref, s4_ref, s8_ref):
    tm = x_ref.shape[0]
    ext_ref[0:POOL_HALO, :] = jnp.where(has_prev, ap_ref[...], 0.0)
    ext_ref[POOL_HALO:POOL_HALO + tm, :] = a_ref[...]
    ext_ref[POOL_HALO + tm:2 * POOL_HALO + tm, :] = jnp.where(has_next, an_ref[...], 0.0)

    pw = a_ref.shape[1]
    aw = ow_ref.shape[1]
    y = jnp.dot(ow_ref[...], wo_ref[pw:pw + aw, :], preferred_element_type=F32)
    y = y + jnp.dot(og_ref[...], wo_ref[pw + aw:pw + 2 * aw, :], preferred_element_type=F32)

    lo = lax.broadcasted_iota(jnp.int32, (tm, LANES), 1) < HEAD_DIM

    pad = POOL_HALO // 2
    ext_ref[2 * POOL_HALO + tm:2 * POOL_HALO + tm + pad, :] = jnp.zeros((pad, ext_ref.shape[1]), F32)
    r2, r4, r8 = tm + 2 * POOL_HALO, tm + 2 * POOL_HALO - pad, tm + POOL_HALO
    s2_ref[0:r2, :] = ext_ref[0:r2, :] + ext_ref[1:r2 + 1, :]
    s4_ref[0:r4, :] = s2_ref[0:r4, :] + s2_ref[2:r4 + 2, :]
    s8_ref[0:r8, :] = s4_ref[0:r8, LANES:2 * LANES] + s4_ref[4:r8 + 4, LANES:2 * LANES]

    def window_sum(w, c):
        first = POOL_HALO - w // 2
        cols = slice(c * LANES, (c + 1) * LANES)
        if w == 2:
            return s2_ref[first:first + tm, cols]
        if w == 4:
            return s4_ref[first:first + tm, cols]
        if w == 8:
            return s8_ref[first:first + tm, :]
        return s8_ref[first:first + tm, :] + s8_ref[first + 8:first + 8 + tm, :]

    feats = []
    for c in range(2):
        w_a, w_b = POOL_WINDOWS[2 * c], POOL_WINDOWS[2 * c + 1]
        x0 = ext_ref[POOL_HALO:POOL_HALO + tm, c * LANES:(c + 1) * LANES]
        mean = jnp.where(lo, window_sum(w_a, c), window_sum(w_b, c)) * icnt_ref[:, c * LANES:(c + 1) * LANES]
        feats.append(mean - x0)
    feats = jnp.concatenate(feats, axis=1).astype(BF16)
    y_pool = jnp.dot(feats, wpool_ref[...], preferred_element_type=F32) * pscale_ref[...]

    y = y + jnp.dot(y_pool.astype(BF16), wo_ref[0:pw, :], preferred_element_type=F32)
    return x_ref[...] + mod_ref[2:3, :] * _rms(y, g_ref[...])


N_OUT_PROJ_IN = 12


def _out_proj_kernel(*refs):
    ins, o_ref, scratch = refs[:N_OUT_PROJ_IN], refs[N_OUT_PROJ_IN], refs[N_OUT_PROJ_IN + 1:]
    i = pl.program_id(1)
    o_ref[...] = _out_proj_tile(i > 0, i < pl.num_programs(1) - 1, *ins, *scratch)


def _out_proj(x, a, inv_count, o_win, o_glb, w_pool_bd, pool_scale, w_out, g_post, mod, tm):
    b, n, d = x.shape
    pw = a.shape[2]
    aw = o_win.shape[2]
    hb = tm // POOL_HALO
    row = lambda bi, i: (0, 0)
    tile = lambda bi, i: (bi, i, 0)
    return pl.pallas_call(
        _out_proj_kernel,
        out_shape=jax.ShapeDtypeStruct((b, n, d), F32),
        grid=(b, n // tm),
        in_specs=[
            pl.BlockSpec((None, tm, d), tile),
            pl.BlockSpec((None, tm, pw), tile),
            pl.BlockSpec((None, POOL_HALO, pw), lambda bi, i: (bi, jnp.maximum(i * hb - 1, 0), 0)),
            pl.BlockSpec((None, POOL_HALO, pw), lambda bi, i: (bi, jnp.minimum((i + 1) * hb, n // POOL_HALO - 1), 0)),
            pl.BlockSpec((tm, pw), lambda bi, i: (i, 0)),
            pl.BlockSpec((None, tm, aw), tile),
            pl.BlockSpec((None, tm, aw), tile),
            _resident((pw, pw), row),
            pl.BlockSpec((1, pw), row),
            _layer_spec(w_out),
            pl.BlockSpec((1, d), row),
            pl.BlockSpec((None, N_MOD, d), lambda bi, i: (bi, 0, 0)),
        ],
        out_specs=pl.BlockSpec((None, tm, d), tile),
        scratch_shapes=[pltpu.VMEM((tm + 2 * POOL_HALO + POOL_HALO // 2, pw), F32),
                        pltpu.VMEM((tm + 2 * POOL_HALO, pw), F32), pltpu.VMEM((tm + 2 * POOL_HALO, pw), F32),
                        pltpu.VMEM((tm + 2 * POOL_HALO, LANES), F32)],
        compiler_params=_cparams("arbitrary", "arbitrary"),
        name="out_proj",
    )(x, a, a, a, inv_count, o_win, o_glb, w_pool_bd, pool_scale, w_out[0], g_post, mod)


def _ffn_tile(x, mod_ref, gpre_ref, gpost_ref, wg_ref, wu_ref, wd_ref, chunk):
    h = _rms(x, gpre_ref[...])
    h = (h * (1.0 + mod_ref[4:5, :]) + mod_ref[3:4, :]).astype(BF16)
    f = None
    for c in range(wg_ref.shape[1] // chunk):
        cols = slice(c * chunk, (c + 1) * chunk)
        g = jnp.dot(h, wg_ref[:, cols], preferred_element_type=F32)
        u = jnp.dot(h, wu_ref[:, cols], preferred_element_type=F32)
        act = ((g * jax.nn.sigmoid(g)) * u).astype(BF16)
        part = jnp.dot(act, wd_ref[cols, :], preferred_element_type=F32)
        f = part if f is None else f + part
    return x + mod_ref[5:6, :] * _rms(f, gpost_ref[...])


def _ffn_kernel(x_ref, mod_ref, gpre_ref, gpost_ref, wg_ref, wu_ref, wd_ref, o_ref, *, chunk):
    o_ref[...] = _ffn_tile(x_ref[...], mod_ref, gpre_ref, gpost_ref, wg_ref, wu_ref, wd_ref, chunk)


def _ffn(x, mod, g_pre, g_post, w_gate, w_up, w_down, tm, chunk):
    b, n, d = x.shape
    row = lambda bi, i: (0, 0)
    tile = lambda bi, i: (bi, i, 0)
    return pl.pallas_call(
        functools.partial(_ffn_kernel, chunk=chunk),
        out_shape=jax.ShapeDtypeStruct((b, n, d), F32),
        grid=(b, n // tm),
        in_specs=[
            pl.BlockSpec((None, tm, d), tile),
            pl.BlockSpec((None, N_MOD, d), lambda bi, i: (bi, 0, 0)),
            pl.BlockSpec((1, d), row),
            pl.BlockSpec((1, d), row),
            _layer_spec(w_gate),
            _layer_spec(w_up),
            _layer_spec(w_down),
        ],
        out_specs=pl.BlockSpec((None, tm, d), tile),
        compiler_params=_cparams("arbitrary", "arbitrary"),
        name="ffn",
    )(x, mod, g_pre, g_post, w_gate[0], w_up[0], w_down[0])


N_FFN_IN = 5
N_PROJ_IN = 8
N_PROJ_OUT = 7


def _tail_kernel(*refs, tpb, chunk, st, with_proj):
    refs = list(refs)
    op_in, refs = refs[:N_OUT_PROJ_IN], refs[N_OUT_PROJ_IN:]
    ffn_in, refs = refs[:N_FFN_IN], refs[N_FFN_IN:]
    if with_proj:
        (modi_ref, gin_ref, win_ref, cos_ref, sin_ref, gq_ref, gk_ref, qfill_ref), refs = refs[:N_PROJ_IN], refs[N_PROJ_IN:]
    o_ref, refs = refs[0], refs[1:]
    if with_proj:
        proj_out, refs = refs[:N_PROJ_OUT], refs[N_PROJ_OUT:]
        u_ref, refs = refs[0], refs[1:]
    pool_scratch = refs
    tm = o_ref.shape[0]
    step = pl.program_id(0)
    i = lax.rem(jnp.minimum(step, pl.num_programs(0) - (2 if with_proj else 1)), tpb)

    if with_proj:
        @pl.when(step == 0)
        def _():
            u_ref[...] = jnp.zeros(u_ref.shape, F32)

        _in_proj_epilogue(lambda sub: u_ref[sub * st:(sub + 1) * st, :], tm, st, cos_ref, sin_ref, gq_ref, gk_ref,
                          qfill_ref, *proj_out)

    x_mid = _out_proj_tile(i > 0, i < tpb - 1, *op_in, *pool_scratch)
    y = _ffn_tile(x_mid, op_in[-1], *ffn_in, chunk)
    o_ref[...] = y
    if with_proj:
        u_ref[...] = _project(y, modi_ref, gin_ref, win_ref)


def _tail(x, a, inv_count, o_win, o_glb, w_pool_bd, pool_scale, w_out, g_post, mod, g_pre_f, g_post_f, w_gate, w_up, w_down,
          proj, tm, chunk):
    b, n, d = x.shape
    pw, aw = a.shape[2], o_win.shape[2]
    hb = tm // POOL_HALO
    tpb = n // tm
    last = b * tpb - 1
    with_proj = proj is not None
    cur = lambda s: (jnp.minimum(s, last) // tpb, jnp.minimum(s, last) % tpb)
    prev = lambda s: (jnp.maximum(s - 1, 0) // tpb, jnp.maximum(s - 1, 0) % tpb)
    row = lambda s: (0, 0)
    tile = lambda s: (cur(s)[0], cur(s)[1], 0)
    in_specs = [
        pl.BlockSpec((None, tm, d), tile),
        pl.BlockSpec((None, tm, pw), tile),
        pl.BlockSpec((None, POOL_HALO, pw), lambda s: (cur(s)[0], jnp.maximum(cur(s)[1] * hb - 1, 0), 0)),
        pl.BlockSpec((None, POOL_HALO, pw), lambda s: (cur(s)[0], jnp.minimum((cur(s)[1] + 1) * hb, n // POOL_HALO - 1), 0)),
        pl.BlockSpec((tm, pw), lambda s: (cur(s)[1], 0)),
        pl.BlockSpec((None, tm, aw), tile),
        pl.BlockSpec((None, tm, aw), tile),
        _resident((pw, pw), row),
        pl.BlockSpec((1, pw), row),
        _layer_spec(w_out),
        pl.BlockSpec((1, d), row),
        pl.BlockSpec((None, N_MOD, d), lambda s: (cur(s)[0], 0, 0)),
        pl.BlockSpec((1, d), row),
        pl.BlockSpec((1, d), row),
        _layer_spec(w_gate),
        _layer_spec(w_up),
        _layer_spec(w_down),
    ]
    args = [x, a, a, a, inv_count, o_win, o_glb, w_pool_bd, pool_scale, w_out[0], g_post, mod,
            g_pre_f, g_post_f, w_gate[0], w_up[0], w_down[0]]
    out_shape = (jax.ShapeDtypeStruct((b, n, d), F32),)
    out_specs = (pl.BlockSpec((None, tm, d), tile),)
    scratch = [pltpu.VMEM((tm + 2 * POOL_HALO + POOL_HALO // 2, pw), F32),
               pltpu.VMEM((tm + 2 * POOL_HALO, pw), F32), pltpu.VMEM((tm + 2 * POOL_HALO, pw), F32),
               pltpu.VMEM((tm + 2 * POOL_HALO, LANES), F32)]
    if with_proj:
        mod_i, g_pre_i, w_in, cos, sin, gq, gk, qfill = proj
        nw = w_in[0].shape[2]
        in_specs += [
            pl.BlockSpec((None, N_MOD, d), lambda s: (cur(s)[0], 0, 0)),
            pl.BlockSpec((1, d), row),
            _layer_spec(w_in),
            pl.BlockSpec((tm, LANES), lambda s: (prev(s)[1], 0)),
            pl.BlockSpec((tm, LANES), lambda s: (prev(s)[1], 0)),
            pl.BlockSpec((1, LANES), row),
            pl.BlockSpec((1, LANES), row),
            pl.BlockSpec((1, LANES), row),
        ]
        args += [mod_i, g_pre_i, w_in[0], cos, sin, gq, gk, qfill]
        proj_shape, proj_specs = _in_proj_outputs(b, n, tm, prev)
        out_shape += proj_shape
        out_specs += proj_specs
        scratch = [pltpu.VMEM((tm, nw), F32)] + scratch
    outs = pl.pallas_call(
        functools.partial(_tail_kernel, tpb=tpb, chunk=chunk, st=min(tm, 256), with_proj=with_proj),
        out_shape=out_shape,
        grid=(b * tpb + (1 if with_proj else 0),),
        in_specs=in_specs,
        out_specs=out_specs,
        scratch_shapes=scratch,
        compiler_params=_cparams("arbitrary"),
        name="tail_proj" if with_proj else "tail",
    )(*args)
    return outs[0], outs[1:]


def _rope_tables(n):
    rows = n // GRID_W
    row = jnp.broadcast_to(jnp.arange(rows)[:, None], (rows, GRID_W)).reshape(-1).astype(F32)
    col = jnp.broadcast_to(jnp.arange(GRID_W)[None, :], (rows, GRID_W)).reshape(-1).astype(F32)
    freq = ROPE_THETA ** (-jnp.arange(ROPE_FREQS, dtype=F32) / ROPE_FREQS)
    ar, ac = row[:, None] * freq, col[:, None] * freq
    cos = jnp.concatenate([jnp.cos(ar), jnp.cos(ar), jnp.cos(ac), jnp.cos(ac)], axis=1)
    sin = jnp.concatenate([-jnp.sin(ar), jnp.sin(ar), -jnp.sin(ac), jnp.sin(ac)], axis=1)
    return jnp.tile(cos, (1, LANES // HEAD_DIM)), jnp.tile(sin, (1, LANES // HEAD_DIM))


def _shift_row(m):
    pieces, rest = [], m.astype(F32)
    for _ in range(N_SHIFT):
        part = rest.astype(BF16).astype(F32)
        pieces.append(-part)
        rest = rest - part
    return jnp.zeros((1, LANES), F32).at[0, HEAD_DIM:HEAD_DIM + N_SHIFT].set(jnp.stack(pieces))


def _pool_inv_counts(n):
    t = jnp.arange(n)
    cols = []
    for w in POOL_WINDOWS:
        hi = jnp.clip(t + (w - 1 - w // 2), 0, n - 1)
        lo = jnp.clip(t - w // 2, 0, n - 1)
        cols.append(jnp.broadcast_to((1.0 / (hi - lo + 1).astype(F32))[:, None], (n, HEAD_DIM)))
    return jnp.concatenate(cols, axis=1)


def _tile(n, pref):
    t = min(n, pref)
    assert n % t == 0, (n, t)
    return t


def kernel(x, c, ctx, c_ctx, w_mod, b_mod, g_pre_mix, g_post_mix, g_pre_ffn, g_post_ffn, w_in, w_pool, pool_scale, win_sink, g_qnorm, g_knorm, w_out, w_gate, w_up, w_down):
    b, n, d = x.shape
    lc = ctx.shape[1]
    depth = w_mod.shape[0]
    assert w_in.shape[2] == IN_WIDTH and KV_WIDTH == LANES and n % GRID_W == 0

    cos, sin = _rope_tables(n)
    cos_c, sin_c = jnp.ones((lc, LANES), F32), jnp.zeros((lc, LANES), F32)

    mod_rows = -(-(b + 1) // 8) * 8
    cc = jnp.concatenate([c, c_ctx[None, :], jnp.zeros((mod_rows - b - 1, d), F32)], axis=0)
    mod_all = _modulation(cc, w_mod, b_mod)

    tm = _tile(n, 512)
    tq_win = _tile(n, 1024)
    tq_glb = _tile(n, 512)
    tk_glb = n // (N_KV_HEADS * (tq_glb // WINDOW))
    assert tk_glb % LANES == 0
    rb_glb = 512
    ffn_chunk = 256

    def layer(l):
        bound = ATTN_SCALE * HEAD_DIM * jnp.max(jnp.abs(g_qnorm[l])) * jnp.max(jnp.abs(g_knorm[l]))
        return dict(
            m=mod_all[l, :b].reshape(b, N_MOD, d),
            mc=jnp.broadcast_to(mod_all[l, b].reshape(1, N_MOD, d), (b, N_MOD, d)),
            w_in=(w_in_b, l),
            gq=jnp.tile(g_qnorm[l], LANES // HEAD_DIM)[None, :],
            gk=jnp.tile(g_knorm[l], LANES // HEAD_DIM)[None, :],
            g_pre=g_pre_mix[l][None, :],
            bound=bound,
            qfill=jnp.where(bound <= SAFE_SHIFT, _shift_row(bound), 0.0),
        )

    w_in_b, w_out_b, w_gate_b, w_up_b, w_down_b = (w.astype(BF16) for w in (w_in, w_out, w_gate, w_up, w_down))
    no_fill = jnp.zeros((1, LANES), F32)
    icnt, icnt_c = _pool_inv_counts(n), _pool_inv_counts(lc)
    xc = ctx
    p = layer(0)
    proj = _in_proj(x, p["m"], p["g_pre"], p["w_in"], cos, sin, p["gq"], p["gk"], p["qfill"], tm, carried=True)
    for l in range(depth):
        last = l == depth - 1
        m, mc, gq, gk = p["m"], p["mc"], p["gq"], p["gk"]
        w_out_l, w_gate_l, w_up_l, w_down_l = (w_out_b, l), (w_gate_b, l), (w_up_b, l), (w_down_b, l)
        w_pool_bd = jax.scipy.linalg.block_diag(*[w_pool[l, g] for g in range(len(POOL_WINDOWS))]).astype(BF16)
        pscale = pool_scale[l][None, :]
        sink = jnp.zeros((8, LANES), F32).at[:N_Q_HEADS].set(jnp.broadcast_to(win_sink[l][:, None], (N_Q_HEADS, LANES)))
        g_post = g_post_mix[l][None, :]
        g_pre_f, g_post_f = g_pre_ffn[l][None, :], g_post_ffn[l][None, :]

        a, qw, ktw, vw, qg, ktg, vg = proj
        ac, qwc, ktwc, vwc, qgc, ktgc, vgc = _in_proj(xc, mc, p["g_pre"], p["w_in"], cos_c, sin_c, gq, gk, no_fill, lc)

        def attn_bounded(*ops):
            o_g, o_w = _attn(*ops, tq=tq_glb, tk=tk_glb, rb=rb_glb)
            return o_w, o_g

        def attn_exact(qg, ktg, vg, ktgc, vgc, qw, ktw, vw, ktwc, vwc, sink):
            return (_window_attn(qw, ktw, vw, ktwc, vwc, sink, tq_win),
                    _global_attn(qg, ktg, vg, ktgc, vgc, tq=tq_glb, tk=tk_glb, rb=rb_glb))

        o_win, o_glb = lax.cond(p["bound"] <= SAFE_SHIFT, attn_bounded, attn_exact,
                                qg, ktg, vg, ktgc, vgc, qw, ktw, vw, ktwc, vwc, sink)
        tail_args = (x, a, icnt, o_win, o_glb, w_pool_bd, pscale, w_out_l, g_post, m, g_pre_f, g_post_f,
                     w_gate_l, w_up_l, w_down_l)
        if last:
            x, _ = _tail(*tail_args, None, tm, ffn_chunk)
        else:
            oc_win = _ctx_attn(qwc, ktwc, vwc, sink)
            oc_glb = _ctx_attn(qgc, ktgc, vgc, None)
            xc = _out_proj(xc, ac, icnt_c, oc_win, oc_glb, w_pool_bd, pscale, w_out_l, g_post, mc, lc)
            xc = _ffn(xc, mc, g_pre_f, g_post_f, w_gate_l, w_up_l, w_down_l, lc, ffn_chunk)
            p = layer(l + 1)
            x, proj = _tail(*tail_args, (p["m"], p["g_pre"], p["w_in"], cos, sin, p["gq"], p["gk"], p["qfill"]),
                            tm, ffn_chunk)
    return x
```

```python
import functools

import jax
import jax.numpy as jnp
from jax import lax
from jax.experimental import pallas as pl
from jax.experimental.pallas import tpu as pltpu

F32 = jnp.float32
BF16 = jnp.bfloat16

HEAD_DIM = 64
GRID_W = 64
ROPE_FREQS = HEAD_DIM // 4
ROPE_THETA = 10000.0
NORM_EPS = 1e-6
NEG_INF = -1e30
ATTN_SCALE = HEAD_DIM ** -0.5
WINDOW = 128
POOL_WINDOWS = (2, 4, 8, 16)
POOL_HALO = 16
N_Q_HEADS = 6
N_KV_HEADS = 2
KV_GROUP = N_Q_HEADS // N_KV_HEADS
N_MOD = 6
POOL_WIDTH = len(POOL_WINDOWS) * HEAD_DIM
Q_WIDTH = N_Q_HEADS * HEAD_DIM
KV_WIDTH = N_KV_HEADS * HEAD_DIM
COL_QW = POOL_WIDTH
COL_KW = COL_QW + Q_WIDTH
COL_VW = COL_KW + KV_WIDTH
COL_QG = COL_VW + KV_WIDTH
COL_KG = COL_QG + Q_WIDTH
COL_VG = COL_KG + KV_WIDTH
IN_WIDTH = COL_VG + KV_WIDTH
N_SHIFT = 3
SAFE_SHIFT = 30.0

LANES = 128
VMEM_LIMIT = 56 * 1024 * 1024


def _cparams(*sem):
    return pltpu.CompilerParams(dimension_semantics=sem, vmem_limit_bytes=VMEM_LIMIT)


def _resident(block_shape, index_map):
    return pl.BlockSpec(block_shape, index_map, pipeline_mode=pl.Buffered(1))


def _layer_spec(wl):
    w, l = wl
    return pl.BlockSpec((None,) + w.shape[1:], lambda *g: (l,) + (0,) * (w.ndim - 1), pipeline_mode=pl.Buffered(1))


def _rms(x, g):
    ms = jnp.mean(x * x, axis=-1, keepdims=True)
    return (x * lax.rsqrt(ms + NORM_EPS)) * g


def _mod_kernel(c_ref, w_ref, b_ref, o_ref):
    c = c_ref[...]
    s = c * jax.nn.sigmoid(c)
    o_ref[...] = jnp.dot(s.astype(BF16), w_ref[...].astype(BF16), preferred_element_type=F32) + b_ref[...]


def _modulation(cc, w_mod, b_mod):
    depth, d, dm = w_mod.shape
    rows = cc.shape[0]
    tn = dm // 4
    return pl.pallas_call(
        _mod_kernel,
        out_shape=jax.ShapeDtypeStruct((depth, rows, dm), F32),
        grid=(depth, dm // tn),
        in_specs=[
            pl.BlockSpec((rows, d), lambda l, j: (0, 0)),
            pl.BlockSpec((None, d, tn), lambda l, j: (l, 0, j)),
            pl.BlockSpec((None, 1, tn), lambda l, j: (l, 0, j)),
        ],
        out_specs=pl.BlockSpec((None, rows, tn), lambda l, j: (l, 0, j)),
        compiler_params=_cparams("arbitrary", "arbitrary"),
        name="modulation",
    )(cc, w_mod, b_mod.reshape(depth, 1, dm))


def _in_proj_epilogue(u_of, n_rows, st, cos_ref, sin_ref, gq_ref, gk_ref, qfill_ref,
                      a_ref, qw_ref, ktw_ref, vw_ref, qg_ref, ktg_ref, vg_ref):
    lane = lax.broadcasted_iota(jnp.int32, (st, LANES), 1)
    lo = lane < HEAD_DIM
    first = (lane & ROPE_FREQS) == 0
    shift_rows = (lax.broadcasted_iota(jnp.int32, (HEAD_DIM, LANES), 0) < N_SHIFT).astype(BF16)
    sum_lane = (lane == HEAD_DIM).astype(F32)
    gq, gk = gq_ref[...], gk_ref[...]
    qfill = qfill_ref[...]

    def head_norm(t, g):
        sq = t * t
        s_lo = jnp.sum(jnp.where(lo, sq, 0.0), axis=-1, keepdims=True)
        s_hi = jnp.sum(jnp.where(lo, 0.0, sq), axis=-1, keepdims=True)
        ms = jnp.where(lo, s_lo, s_hi) * (1.0 / HEAD_DIM)
        return (t * lax.rsqrt(ms + NORM_EPS)) * g

    for sub in range(n_rows // st):
        rs = slice(sub * st, (sub + 1) * st)
        u = u_of(sub)
        cos = cos_ref[rs, :]
        sin = sin_ref[rs, :]

        def rope(t):
            partner = jnp.where(first, pltpu.roll(t, LANES - ROPE_FREQS, 1), pltpu.roll(t, ROPE_FREQS, 1))
            return t * cos + partner * sin

        def store_branch(q_chunks, k, v, fill, q_ref, kt_ref, v_ref):
            for hd in range(N_Q_HEADS):
                t = q_chunks[hd // 2]
                if hd % 2:
                    t = pltpu.roll(t, HEAD_DIM, 1)
                q_ref[hd, rs, :] = jnp.where(lo, t, fill).astype(BF16)
            for r in range(st // LANES):
                kb = sub * (st // LANES) + r
                kt = jnp.transpose(k[r * LANES:(r + 1) * LANES, :]).astype(BF16)
                for kv in range(N_KV_HEADS):
                    kt_ref[kv, kb, 0:HEAD_DIM, :] = kt[kv * HEAD_DIM:(kv + 1) * HEAD_DIM, :]
                    kt_ref[kv, kb, HEAD_DIM:LANES, :] = shift_rows
            v_ref[0, rs, :] = jnp.where(lo, v, sum_lane).astype(BF16)
            v_ref[1, rs, :] = jnp.where(lo, pltpu.roll(v, HEAD_DIM, 1), sum_lane).astype(BF16)

        def cols(start, width):
            return u[:, start:start + width]

        a_ref[rs, :] = cols(0, POOL_WIDTH)
        store_branch([rope(cols(COL_QW + LANES * c, LANES)) * ATTN_SCALE for c in range(Q_WIDTH // LANES)],
                     rope(cols(COL_KW, KV_WIDTH)), cols(COL_VW, KV_WIDTH), 0.0, qw_ref, ktw_ref, vw_ref)
        store_branch([rope(head_norm(cols(COL_QG + LANES * c, LANES), gq)) * ATTN_SCALE for c in range(Q_WIDTH // LANES)],
                     rope(head_norm(cols(COL_KG, KV_WIDTH), gk)), cols(COL_VG, KV_WIDTH), qfill, qg_ref, ktg_ref, vg_ref)


def _project(x, mod_ref, g_ref, w_ref):
    h = _rms(x, g_ref[...])
    h = h * (1.0 + mod_ref[1:2, :]) + mod_ref[0:1, :]
    return jnp.dot(h.astype(BF16), w_ref[...], preferred_element_type=F32)


def _in_proj_kernel(x_ref, mod_ref, g_ref, w_ref, cos_ref, sin_ref, gq_ref, gk_ref, qfill_ref, *out_refs, st):
    tm = x_ref.shape[0]
    us = [_project(x_ref[sub * st:(sub + 1) * st, :], mod_ref, g_ref, w_ref) for sub in range(tm // st)]
    _in_proj_epilogue(lambda sub: us[sub], tm, st, cos_ref, sin_ref, gq_ref, gk_ref, qfill_ref, *out_refs)


def _in_proj_outputs(b, n, tm, tile_of):
    qshape = jax.ShapeDtypeStruct((b, N_Q_HEADS, n, LANES), BF16)
    ktshape = jax.ShapeDtypeStruct((b, N_KV_HEADS, n // LANES, LANES, LANES), BF16)
    vshape = jax.ShapeDtypeStruct((b, N_KV_HEADS, n, LANES), BF16)

    def spec(block, where):
        return pl.BlockSpec(block, lambda *g: where(*tile_of(*g)))

    aspec = spec((None, tm, POOL_WIDTH), lambda bi, i: (bi, i, 0))
    qspec = spec((None, N_Q_HEADS, tm, LANES), lambda bi, i: (bi, 0, i, 0))
    ktspec = spec((None, N_KV_HEADS, tm // LANES, LANES, LANES), lambda bi, i: (bi, 0, i, 0, 0))
    vspec = spec((None, N_KV_HEADS, tm, LANES), lambda bi, i: (bi, 0, i, 0))
    shapes = (jax.ShapeDtypeStruct((b, n, POOL_WIDTH), F32), qshape, ktshape, vshape, qshape, ktshape, vshape)
    return shapes, (aspec, qspec, ktspec, vspec, qspec, ktspec, vspec)


def _in_proj_carried_kernel(x_ref, mod_ref, g_ref, w_ref, cos_ref, sin_ref, gq_ref, gk_ref, qfill_ref, *refs, st):
    out_refs, u_ref = refs[:-1], refs[-1]

    @pl.when(pl.program_id(0) == 0)
    def _():
        u_ref[...] = jnp.zeros(u_ref.shape, F32)

    _in_proj_epilogue(lambda sub: u_ref[sub * st:(sub + 1) * st, :], x_ref.shape[0], st, cos_ref, sin_ref,
                      gq_ref, gk_ref, qfill_ref, *out_refs)
    u_ref[...] = _project(x_ref[...], mod_ref, g_ref, w_ref)


def _in_proj(x, mod, g_pre, w_in, cos, sin, gq, gk, qfill, tm, carried=False):
    b, n, d = x.shape
    nw = w_in[0].shape[2]
    tpb = n // tm
    last = b * tpb - 1
    if carried:
        grid = (b * tpb + 1,)
        cur = lambda s: (jnp.minimum(s, last) // tpb, jnp.minimum(s, last) % tpb)
        prev = lambda s: (jnp.maximum(s - 1, 0) // tpb, jnp.maximum(s - 1, 0) % tpb)
        kern, scratch, sem = _in_proj_carried_kernel, [pltpu.VMEM((tm, nw), F32)], ("arbitrary",)
    else:
        grid = (b, tpb)
        cur = prev = lambda bi, i: (bi, i)
        kern, scratch, sem = _in_proj_kernel, [], ("arbitrary", "arbitrary")
    out_shape, out_specs = _in_proj_outputs(b, n, tm, prev)
    row = lambda *g: (0, 0)
    return pl.pallas_call(
        functools.partial(kern, st=min(tm, 256)),
        out_shape=out_shape,
        grid=grid,
        in_specs=[
            pl.BlockSpec((None, tm, d), lambda *g: (cur(*g)[0], cur(*g)[1], 0)),
            pl.BlockSpec((None, N_MOD, d), lambda *g: (cur(*g)[0], 0, 0)),
            pl.BlockSpec((1, d), row),
            _layer_spec(w_in),
            pl.BlockSpec((tm, LANES), lambda *g: (prev(*g)[1], 0)),
            pl.BlockSpec((tm, LANES), lambda *g: (prev(*g)[1], 0)),
            pl.BlockSpec((1, LANES), row),
            pl.BlockSpec((1, LANES), row),
            pl.BlockSpec((1, LANES), row),
        ],
        out_specs=out_specs,
        scratch_shapes=scratch,
        compiler_params=_cparams(*sem),
        name="in_proj",
    )(x, mod, g_pre, w_in[0], cos, sin, gq, gk, qfill)


def _merge_heads(outs, lo):
    return [jnp.where(lo, outs[2 * c], pltpu.roll(outs[2 * c + 1], HEAD_DIM, 1)) for c in range(N_Q_HEADS // 2)]


def _key_tile(kt_ref, first_block, n_blocks):
    return jnp.concatenate([kt_ref[first_block + j] for j in range(n_blocks)], axis=1)


def _stacked_sink(sink_ref, kv, rows_per_head):
    return jnp.concatenate([jnp.broadcast_to(sink_ref[hd:hd + 1, 0:1], (rows_per_head, 1))
                            for hd in range(kv * KV_GROUP, (kv + 1) * KV_GROUP)], axis=0)


def _window_units(q_ref, kt_ref, v_ref, ktc_ref, vc_ref, sink_ref, first_blk):
    nkb = kt_ref.shape[1]
    sub = WINDOW
    rows = KV_GROUP * sub
    r_in = lax.broadcasted_iota(jnp.int32, (rows, LANES), 0) & (sub - 1)
    c_in = lax.broadcasted_iota(jnp.int32, (rows, LANES), 1)
    ctx = [(_key_tile(ktc_ref.at[kv], 0, ktc_ref.shape[1]), vc_ref[kv], _stacked_sink(sink_ref, kv, sub))
           for kv in range(N_KV_HEADS)]

    def blocks(j):
        blk = first_blk + j
        return blk, jnp.maximum(blk - 1, 0), jnp.minimum(blk + 1, nkb - 1)

    def logits(j, kv):
        blk, prev_blk, next_blk = blocks(j)
        q = q_ref[kv * KV_GROUP:(kv + 1) * KV_GROUP, pl.ds(pl.multiple_of(j * sub, sub), sub), :].reshape(rows, LANES)
        kt = jnp.concatenate([kt_ref[kv, prev_blk], kt_ref[kv, blk], kt_ref[kv, next_blk]], axis=1)
        s = jnp.dot(q, kt, preferred_element_type=F32)
        sc = jnp.dot(q, ctx[kv][0], preferred_element_type=F32)
        return s, sc

    def attend(j, kv, s, sc):
        blk, prev_blk, next_blk = blocks(j)
        _, vc, sk = ctx[kv]
        v = jnp.concatenate([v_ref[kv, pl.ds(pl.multiple_of(kb * sub, sub), sub), :]
                             for kb in (prev_blk, blk, next_blk)], axis=0)
        s_prev = jnp.where(jnp.logical_and(c_in >= r_in, blk > 0), s[:, 0:sub], NEG_INF)
        s_own = s[:, sub:2 * sub]
        s_next = jnp.where(jnp.logical_and(c_in <= r_in, blk < nkb - 1), s[:, 2 * sub:3 * sub], NEG_INF)
        mx = jnp.maximum(jnp.maximum(s_prev, s_own), s_next)
        for t in range(sc.shape[1] // LANES):
            mx = jnp.maximum(mx, sc[:, t * LANES:(t + 1) * LANES])
        m = jnp.maximum(jnp.max(mx, axis=-1, keepdims=True), sk)
        p = jnp.concatenate([jnp.exp(s_prev - m), jnp.exp(s_own - m), jnp.exp(s_next - m)], axis=1)
        pv = jnp.dot(p.astype(BF16), v, preferred_element_type=F32)
        pv = pv + jnp.dot(jnp.exp(sc - m).astype(BF16), vc, preferred_element_type=F32)
        l = pv[:, HEAD_DIM:HEAD_DIM + 1] + jnp.exp(sk - m)
        o = pv / l
        return [o[g * sub:(g + 1) * sub] for g in range(KV_GROUP)]

    return logits, attend


def _store_heads(o_ref, row0, outs):
    nrows = outs[0].shape[0]
    lo = lax.broadcasted_iota(jnp.int32, (nrows, LANES), 1) < HEAD_DIM
    for c, chunk in enumerate(_merge_heads(outs, lo)):
        o_ref[pl.ds(row0, nrows), c * LANES:(c + 1) * LANES] = chunk.astype(BF16)


def _window_attn_kernel(q_ref, kt_ref, v_ref, ktc_ref, vc_ref, sink_ref, o_ref):
    tq = q_ref.shape[1]
    sub = WINDOW
    logits, attend = _window_units(q_ref, kt_ref, v_ref, ktc_ref, vc_ref, sink_ref, pl.program_id(1) * (tq // sub))
    units = [(j, kv) for j in range(tq // sub) for kv in range(N_KV_HEADS)]
    pending = logits(*units[0])
    outs = []
    for idx, (j, kv) in enumerate(units):
        ahead = logits(*units[idx + 1]) if idx + 1 < len(units) else None
        outs += attend(j, kv, *pending)
        pending = ahead
        if kv == N_KV_HEADS - 1:
            _store_heads(o_ref, j * sub, outs)
            outs = []


def _window_attn(q, kt, v, ktc, vc, sink, tq):
    b, _, n, _ = q.shape
    lc = vc.shape[2]
    assert tq % WINDOW == 0 and WINDOW == LANES
    batch4 = lambda bi, i: (bi, 0, 0, 0)
    batch5 = lambda bi, i: (bi, 0, 0, 0, 0)
    return pl.pallas_call(
        _window_attn_kernel,
        out_shape=jax.ShapeDtypeStruct((b, n, N_Q_HEADS * HEAD_DIM), BF16),
        grid=(b, n // tq),
        in_specs=[
            pl.BlockSpec((None, N_Q_HEADS, tq, LANES), lambda bi, i: (bi, 0, i, 0)),
            pl.BlockSpec((None, N_KV_HEADS, n // LANES, LANES, LANES), batch5),
            pl.BlockSpec((None, N_KV_HEADS, n, LANES), batch4),
            pl.BlockSpec((None, N_KV_HEADS, lc // LANES, LANES, LANES), batch5),
            pl.BlockSpec((None, N_KV_HEADS, lc, LANES), batch4),
            pl.BlockSpec(sink.shape, lambda bi, i: (0, 0)),
        ],
        out_specs=pl.BlockSpec((None, tq, N_Q_HEADS * HEAD_DIM), lambda bi, i: (bi, i, 0)),
        compiler_params=_cparams("arbitrary", "arbitrary"),
        name="window_attn",
    )(q, kt, v, ktc, vc, sink)


def _ctx_attn_kernel(*refs, has_sink):
    if has_sink:
        q_ref, kt_ref, v_ref, sink_ref, o_ref = refs
    else:
        q_ref, kt_ref, v_ref, o_ref = refs
    tq = q_ref.shape[1]
    rows = KV_GROUP * tq
    outs = []
    for kv in range(N_KV_HEADS):
        q = q_ref[kv * KV_GROUP:(kv + 1) * KV_GROUP].reshape(rows, LANES)
        s = jnp.dot(q, _key_tile(kt_ref.at[kv], 0, kt_ref.shape[1]), preferred_element_type=F32)
        m = jnp.max(s, axis=-1, keepdims=True)
        if has_sink:
            sk = _stacked_sink(sink_ref, kv, tq)
            m = jnp.maximum(m, sk)
        pv = jnp.dot(jnp.exp(s - m).astype(BF16), v_ref[kv], preferred_element_type=F32)
        l = pv[:, HEAD_DIM:HEAD_DIM + 1]
        if has_sink:
            l = l + jnp.exp(sk - m)
        o = pv / l
        outs += [o[g * tq:(g + 1) * tq] for g in range(KV_GROUP)]
    _store_heads(o_ref, 0, outs)


def _ctx_attn(q, kt, v, sink):
    b, _, n, _ = q.shape
    args = [q, kt, v]
    in_specs = [pl.BlockSpec((None,) + a.shape[1:], lambda bi, nd=a.ndim: (bi,) + (0,) * (nd - 1)) for a in args]
    if sink is not None:
        args.append(sink)
        in_specs.append(pl.BlockSpec(sink.shape, lambda bi: (0, 0)))
    return pl.pallas_call(
        functools.partial(_ctx_attn_kernel, has_sink=sink is not None),
        out_shape=jax.ShapeDtypeStruct((b, n, N_Q_HEADS * HEAD_DIM), BF16),
        grid=(b,),
        in_specs=in_specs,
        out_specs=pl.BlockSpec((None, n, N_Q_HEADS * HEAD_DIM), lambda bi: (bi, 0, 0)),
        compiler_params=_cparams("arbitrary"),
        name="ctx_attn",
    )(*args)


def _global_attn_kernel(q_ref, kt_ref, v_ref, ktc_ref, vc_ref, o_ref, qs_ref, acc_ref, mx_ref, *, tk, rb):
    tq = q_ref.shape[1]
    rows = KV_GROUP * tq
    kb_per_step = tk // LANES
    n_steps = kt_ref.shape[1] // kb_per_step
    lane = lax.broadcasted_iota(jnp.int32, (rows, LANES), 1)

    def key_steps(fn):
        def step(kc, carry):
            for kv in range(N_KV_HEADS):
                fn(kv, _key_tile(kt_ref.at[kv], kc * kb_per_step, kb_per_step),
                   v_ref[kv, pl.ds(pl.multiple_of(kc * tk, tk), tk), :])
            return carry
        lax.fori_loop(0, n_steps, step, 0)
        for kv in range(N_KV_HEADS):
            fn(kv, _key_tile(ktc_ref.at[kv], 0, ktc_ref.shape[1]), vc_ref[kv])

    def q_rows(kv):
        return q_ref[kv * KV_GROUP:(kv + 1) * KV_GROUP].reshape(rows, LANES)

    mx_ref[...] = jnp.full(mx_ref.shape, NEG_INF, F32)

    def track_max(kv, kt, v):
        s = jnp.dot(q_rows(kv), kt, preferred_element_type=F32)
        mx = mx_ref[kv]
        for j in range(kt.shape[1] // LANES):
            mx = jnp.maximum(mx, s[:, j * LANES:(j + 1) * LANES])
        mx_ref[kv] = mx

    key_steps(track_max)

    for kv in range(N_KV_HEADS):
        qs = q_rows(kv).astype(F32)
        rest = jnp.max(mx_ref[kv], axis=-1, keepdims=True)
        for piece in range(N_SHIFT):
            part = rest.astype(BF16).astype(F32)
            qs = jnp.where(lane == HEAD_DIM + piece, -part, qs)
            rest = rest - part
        qs_ref[kv] = qs.astype(BF16)

    acc_ref[...] = jnp.zeros(acc_ref.shape, F32)

    def accumulate(kv, kt, v):
        for r in range(rows // rb):
            rsl = slice(r * rb, (r + 1) * rb)
            s = jnp.dot(qs_ref[kv, rsl, :], kt, preferred_element_type=F32)
            p = jnp.exp(s).astype(BF16)
            acc_ref[kv, rsl, :] += jnp.dot(p, v, preferred_element_type=F32)

    key_steps(accumulate)

    outs = []
    for hd in range(N_Q_HEADS):
        a = acc_ref[hd // KV_GROUP, (hd % KV_GROUP) * tq:(hd % KV_GROUP + 1) * tq, :]
        outs.append(a / a[:, HEAD_DIM:HEAD_DIM + 1])
    _store_heads(o_ref, 0, outs)


def _global_attn(q, kt, v, ktc, vc, *, tq, tk, rb):
    b, _, n, _ = q.shape
    lc = vc.shape[2]
    rows = KV_GROUP * tq
    batch4 = lambda bi, i: (bi, 0, 0, 0)
    batch5 = lambda bi, i: (bi, 0, 0, 0, 0)
    return pl.pallas_call(
        functools.partial(_global_attn_kernel, tk=tk, rb=rb),
        out_shape=jax.ShapeDtypeStruct((b, n, N_Q_HEADS * HEAD_DIM), BF16),
        grid=(b, n // tq),
        in_specs=[
            pl.BlockSpec((None, N_Q_HEADS, tq, LANES), lambda bi, i: (bi, 0, i, 0)),
            pl.BlockSpec((None, N_KV_HEADS, n // LANES, LANES, LANES), batch5),
            pl.BlockSpec((None, N_KV_HEADS, n, LANES), batch4),
            pl.BlockSpec((None, N_KV_HEADS, lc // LANES, LANES, LANES), batch5),
            pl.BlockSpec((None, N_KV_HEADS, lc, LANES), batch4),
        ],
        out_specs=pl.BlockSpec((None, tq, N_Q_HEADS * HEAD_DIM), lambda bi, i: (bi, i, 0)),
        scratch_shapes=[pltpu.VMEM((N_KV_HEADS, rows, LANES), BF16), pltpu.VMEM((N_KV_HEADS, rows, LANES), F32),
                        pltpu.VMEM((N_KV_HEADS, rows, LANES), F32)],
        compiler_params=_cparams("arbitrary", "arbitrary"),
        name="global_attn_exact_max",
    )(q, kt, v, ktc, vc)


def _attn_kernel(qg_ref, ktg_ref, vg_ref, ktgc_ref, vgc_ref, qw_ref, ktw_ref, vw_ref, ktwc_ref, vwc_ref, sink_ref,
                 og_ref, ow_ref, acc_ref, *, tk, rb):
    tq = qg_ref.shape[1]
    rows = KV_GROUP * tq
    sub = WINDOW
    n_sub = tq // sub
    kb_per_step = tk // LANES
    steps_per_sub = ktg_ref.shape[1] // kb_per_step // n_sub
    assert steps_per_sub * n_sub * kb_per_step == ktg_ref.shape[1] and tq % rb == 0

    def global_keys(kv, kt, v, assign):
        for r in range(rows // rb):
            hd, off = kv * KV_GROUP + (r * rb) // tq, (r * rb) % tq
            s = jnp.dot(qg_ref[hd, off:off + rb, :], kt, preferred_element_type=F32)
            pv = jnp.dot(jnp.exp(s).astype(BF16), v, preferred_element_type=F32)
            if assign:
                acc_ref[kv, r * rb:(r + 1) * rb, :] = pv
            else:
                acc_ref[kv, r * rb:(r + 1) * rb, :] += pv

    def global_step(kc):
        for kv in range(N_KV_HEADS):
            global_keys(kv, _key_tile(ktg_ref.at[kv], kc * kb_per_step, kb_per_step),
                        vg_ref[kv, pl.ds(pl.multiple_of(kc * tk, tk), tk), :], False)

    for kv in range(N_KV_HEADS):
        global_keys(kv, _key_tile(ktgc_ref.at[kv], 0, ktgc_ref.shape[1]), vgc_ref[kv], True)

    logits, attend = _window_units(qw_ref, ktw_ref, vw_ref, ktwc_ref, vwc_ref, sink_ref, pl.program_id(1) * n_sub)

    def body(j, carry):
        pending = logits(j, 0)
        outs = []
        for kv in range(N_KV_HEADS):
            for t in range(steps_per_sub // N_KV_HEADS):
                global_step(j * steps_per_sub + kv * (steps_per_sub // N_KV_HEADS) + t)
            ahead = logits(j, kv + 1) if kv + 1 < N_KV_HEADS else None
            outs += attend(j, kv, *pending)
            pending = ahead
        _store_heads(ow_ref, pl.multiple_of(j * sub, sub), outs)
        return carry

    assert steps_per_sub % N_KV_HEADS == 0
    lax.fori_loop(0, n_sub, body, 0, unroll=True)

    outs = []
    for hd in range(N_Q_HEADS):
        a = acc_ref[hd // KV_GROUP, (hd % KV_GROUP) * tq:(hd % KV_GROUP + 1) * tq, :]
        outs.append(a / a[:, HEAD_DIM:HEAD_DIM + 1])
    _store_heads(og_ref, 0, outs)


def _attn(qg, ktg, vg, ktgc, vgc, qw, ktw, vw, ktwc, vwc, sink, *, tq, tk, rb):
    b, _, n, _ = qg.shape
    lc = vgc.shape[2]
    batch4 = lambda bi, i: (bi, 0, 0, 0)
    batch5 = lambda bi, i: (bi, 0, 0, 0, 0)
    branch_specs = [
        pl.BlockSpec((None, N_Q_HEADS, tq, LANES), lambda bi, i: (bi, 0, i, 0)),
        pl.BlockSpec((None, N_KV_HEADS, n // LANES, LANES, LANES), batch5),
        pl.BlockSpec((None, N_KV_HEADS, n, LANES), batch4),
        pl.BlockSpec((None, N_KV_HEADS, lc // LANES, LANES, LANES), batch5),
        pl.BlockSpec((None, N_KV_HEADS, lc, LANES), batch4),
    ]
    oshape = jax.ShapeDtypeStruct((b, n, N_Q_HEADS * HEAD_DIM), BF16)
    ospec = pl.BlockSpec((None, tq, N_Q_HEADS * HEAD_DIM), lambda bi, i: (bi, i, 0))
    return pl.pallas_call(
        functools.partial(_attn_kernel, tk=tk, rb=rb),
        out_shape=(oshape, oshape),
        grid=(b, n // tq),
        in_specs=branch_specs + branch_specs + [pl.BlockSpec(sink.shape, lambda bi, i: (0, 0))],
        out_specs=(ospec, ospec),
        scratch_shapes=[pltpu.VMEM((N_KV_HEADS, KV_GROUP * tq, LANES), F32)],
        compiler_params=_cparams("arbitrary", "arbitrary"),
        name="attn",
    )(qg, ktg, vg, ktgc, vgc, qw, ktw, vw, ktwc, vwc, sink)


def _out_proj_tile(has_prev, has_next, x_ref, a_ref, ap_ref, an_ref, icnt_ref, ow_ref, og_ref, wpool_ref, pscale_ref,
                   wo_ref, g_ref, mod_ref, ext_ref, s2_ref, s4_ref, s8_ref):
    tm = x_ref.shape[0]
    ext_ref[0:POOL_HALO, :] = jnp.where(has_prev, ap_ref[...], 0.0)
    ext_ref[POOL_HALO:POOL_HALO + tm, :] = a_ref[...]
    ext_ref[POOL_HALO + tm:2 * POOL_HALO + tm, :] = jnp.where(has_next, an_ref[...], 0.0)

    pw = a_ref.shape[1]
    aw = ow_ref.shape[1]
    y = jnp.dot(ow_ref[...], wo_ref[pw:pw + aw, :], preferred_element_type=F32)
    y = y + jnp.dot(og_ref[...], wo_ref[pw + aw:pw + 2 * aw, :], preferred_element_type=F32)

    lo = lax.broadcasted_iota(jnp.int32, (tm, LANES), 1) < HEAD_DIM

    pad = POOL_HALO // 2
    ext_ref[2 * POOL_HALO + tm:2 * POOL_HALO + tm + pad, :] = jnp.zeros((pad, ext_ref.shape[1]), F32)
    r2, r4, r8 = tm + 2 * POOL_HALO, tm + 2 * POOL_HALO - pad, tm + POOL_HALO
    s2_ref[0:r2, :] = ext_ref[0:r2, :] + ext_ref[1:r2 + 1, :]
    s4_ref[0:r4, :] = s2_ref[0:r4, :] + s2_ref[2:r4 + 2, :]
    s8_ref[0:r8, :] = s4_ref[0:r8, LANES:2 * LANES] + s4_ref[4:r8 + 4, LANES:2 * LANES]

    def window_sum(w, c):
        first = POOL_HALO - w // 2
        cols = slice(c * LANES, (c + 1) * LANES)
        if w == 2:
            return s2_ref[first:first + tm, cols]
        if w == 4:
            return s4_ref[first:first + tm, cols]
        if w == 8:
            return s8_ref[first:first + tm, :]
        return s8_ref[first:first + tm, :] + s8_ref[first + 8:first + 8 + tm, :]

    feats = []
    for c in range(2):
        w_a, w_b = POOL_WINDOWS[2 * c], POOL_WINDOWS[2 * c + 1]
        x0 = ext_ref[POOL_HALO:POOL_HALO + tm, c * LANES:(c + 1) * LANES]
        mean = jnp.where(lo, window_sum(w_a, c), window_sum(w_b, c)) * icnt_ref[:, c * LANES:(c + 1) * LANES]
        feats.append(mean - x0)
    feats = jnp.concatenate(feats, axis=1).astype(BF16)
    y_pool = jnp.dot(feats, wpool_ref[...], preferred_element_type=F32) * pscale_ref[...]

    y = y + jnp.dot(y_pool.astype(BF16), wo_ref[0:pw, :], preferred_element_type=F32)
    return x_ref[...] + mod_ref[2:3, :] * _rms(y, g_ref[...])


N_OUT_PROJ_IN = 12


def _out_proj_kernel(*refs):
    ins, o_ref, scratch = refs[:N_OUT_PROJ_IN], refs[N_OUT_PROJ_IN], refs[N_OUT_PROJ_IN + 1:]
    i = pl.program_id(1)
    o_ref[...] = _out_proj_tile(i > 0, i < pl.num_programs(1) - 1, *ins, *scratch)


def _out_proj(x, a, inv_count, o_win, o_glb, w_pool_bd, pool_scale, w_out, g_post, mod, tm):
    b, n, d = x.shape
    pw = a.shape[2]
    aw = o_win.shape[2]
    hb = tm // POOL_HALO
    row = lambda bi, i: (0, 0)
    tile = lambda bi, i: (bi, i, 0)
    return pl.pallas_call(
        _out_proj_kernel,
        out_shape=jax.ShapeDtypeStruct((b, n, d), F32),
        grid=(b, n // tm),
        in_specs=[
            pl.BlockSpec((None, tm, d), tile),
            pl.BlockSpec((None, tm, pw), tile),
            pl.BlockSpec((None, POOL_HALO, pw), lambda bi, i: (bi, jnp.maximum(i * hb - 1, 0), 0)),
            pl.BlockSpec((None, POOL_HALO, pw), lambda bi, i: (bi, jnp.minimum((i + 1) * hb, n // POOL_HALO - 1), 0)),
            pl.BlockSpec((tm, pw), lambda bi, i: (i, 0)),
            pl.BlockSpec((None, tm, aw), tile),
            pl.BlockSpec((None, tm, aw), tile),
            _resident((pw, pw), row),
            pl.BlockSpec((1, pw), row),
            _layer_spec(w_out),
            pl.BlockSpec((1, d), row),
            pl.BlockSpec((None, N_MOD, d), lambda bi, i: (bi, 0, 0)),
        ],
        out_specs=pl.BlockSpec((None, tm, d), tile),
        scratch_shapes=[pltpu.VMEM((tm + 2 * POOL_HALO + POOL_HALO // 2, pw), F32),
                        pltpu.VMEM((tm + 2 * POOL_HALO, pw), F32), pltpu.VMEM((tm + 2 * POOL_HALO, pw), F32),
                        pltpu.VMEM((tm + 2 * POOL_HALO, LANES), F32)],
        compiler_params=_cparams("arbitrary", "arbitrary"),
        name="out_proj",
    )(x, a, a, a, inv_count, o_win, o_glb, w_pool_bd, pool_scale, w_out[0], g_post, mod)


def _ffn_tile(x, mod_ref, gpre_ref, gpost_ref, wg_ref, wu_ref, wd_ref, chunk):
    h = _rms(x, gpre_ref[...])
    h = (h * (1.0 + mod_ref[4:5, :]) + mod_ref[3:4, :]).astype(BF16)
    f = None
    for c in range(wg_ref.shape[1] // chunk):
        cols = slice(c * chunk, (c + 1) * chunk)
        g = jnp.dot(h, wg_ref[:, cols], preferred_element_type=F32)
        u = jnp.dot(h, wu_ref[:, cols], preferred_element_type=F32)
        act = ((g * jax.nn.sigmoid(g)) * u).astype(BF16)
        part = jnp.dot(act, wd_ref[cols, :], preferred_element_type=F32)
        f = part if f is None else f + part
    return x + mod_ref[5:6, :] * _rms(f, gpost_ref[...])


def _ffn_kernel(x_ref, mod_ref, gpre_ref, gpost_ref, wg_ref, wu_ref, wd_ref, o_ref, *, chunk):
    o_ref[...] = _ffn_tile(x_ref[...], mod_ref, gpre_ref, gpost_ref, wg_ref, wu_ref, wd_ref, chunk)


def _ffn(x, mod, g_pre, g_post, w_gate, w_up, w_down, tm, chunk):
    b, n, d = x.shape
    row = lambda bi, i: (0, 0)
    tile = lambda bi, i: (bi, i, 0)
    return pl.pallas_call(
        functools.partial(_ffn_kernel, chunk=chunk),
        out_shape=jax.ShapeDtypeStruct((b, n, d), F32),
        grid=(b, n // tm),
        in_specs=[
            pl.BlockSpec((None, tm, d), tile),
            pl.BlockSpec((None, N_MOD, d), lambda bi, i: (bi, 0, 0)),
            pl.BlockSpec((1, d), row),
            pl.BlockSpec((1, d), row),
            _layer_spec(w_gate),
            _layer_spec(w_up),
            _layer_spec(w_down),
        ],
        out_specs=pl.BlockSpec((None, tm, d), tile),
        compiler_params=_cparams("arbitrary", "arbitrary"),
        name="ffn",
    )(x, mod, g_pre, g_post, w_gate[0], w_up[0], w_down[0])


N_FFN_IN = 5
N_PROJ_IN = 8
N_PROJ_OUT = 7


def _tail_kernel(*refs, tpb, chunk, st, with_proj):
    refs = list(refs)
    op_in, refs = refs[:N_OUT_PROJ_IN], refs[N_OUT_PROJ_IN:]
    ffn_in, refs = refs[:N_FFN_IN], refs[N_FFN_IN:]
    if with_proj:
        (modi_ref, gin_ref, win_ref, cos_ref, sin_ref, gq_ref, gk_ref, qfill_ref), refs = refs[:N_PROJ_IN], refs[N_PROJ_IN:]
    o_ref, refs = refs[0], refs[1:]
    if with_proj:
        proj_out, refs = refs[:N_PROJ_OUT], refs[N_PROJ_OUT:]
        u_ref, refs = refs[0], refs[1:]
    pool_scratch = refs
    tm = o_ref.shape[0]
    step = pl.program_id(0)
    i = lax.rem(jnp.minimum(step, pl.num_programs(0) - (2 if with_proj else 1)), tpb)

    if with_proj:
        @pl.when(step == 0)
        def _():
            u_ref[...] = jnp.zeros(u_ref.shape, F32)

        _in_proj_epilogue(lambda sub: u_ref[sub * st:(sub + 1) * st, :], tm, st, cos_ref, sin_ref, gq_ref, gk_ref,
                          qfill_ref, *proj_out)

    x_mid = _out_proj_tile(i > 0, i < tpb - 1, *op_in, *pool_scratch)
    y = _ffn_tile(x_mid, op_in[-1], *ffn_in, chunk)
    o_ref[...] = y
    if with_proj:
        u_ref[...] = _project(y, modi_ref, gin_ref, win_ref)


def _tail(x, a, inv_count, o_win, o_glb, w_pool_bd, pool_scale, w_out, g_post, mod, g_pre_f, g_post_f, w_gate, w_up, w_down,
          proj, tm, chunk):
    b, n, d = x.shape
    pw, aw = a.shape[2], o_win.shape[2]
    hb = tm // POOL_HALO
    tpb = n // tm
    last = b * tpb - 1
    with_proj = proj is not None
    cur = lambda s: (jnp.minimum(s, last) // tpb, jnp.minimum(s, last) % tpb)
    prev = lambda s: (jnp.maximum(s - 1, 0) // tpb, jnp.maximum(s - 1, 0) % tpb)
    row = lambda s: (0, 0)
    tile = lambda s: (cur(s)[0], cur(s)[1], 0)
    in_specs = [
        pl.BlockSpec((None, tm, d), tile),
        pl.BlockSpec((None, tm, pw), tile),
        pl.BlockSpec((None, POOL_HALO, pw), lambda s: (cur(s)[0], jnp.maximum(cur(s)[1] * hb - 1, 0), 0)),
        pl.BlockSpec((None, POOL_HALO, pw), lambda s: (cur(s)[0], jnp.minimum((cur(s)[1] + 1) * hb, n // POOL_HALO - 1), 0)),
        pl.BlockSpec((tm, pw), lambda s: (cur(s)[1], 0)),
        pl.BlockSpec((None, tm, aw), tile),
        pl.BlockSpec((None, tm, aw), tile),
        _resident((pw, pw), row),
        pl.BlockSpec((1, pw), row),
        _layer_spec(w_out),
        pl.BlockSpec((1, d), row),
        pl.BlockSpec((None, N_MOD, d), lambda s: (cur(s)[0], 0, 0)),
        pl.BlockSpec((1, d), row),
        pl.BlockSpec((1, d), row),
        _layer_spec(w_gate),
        _layer_spec(w_up),
        _layer_spec(w_down),
    ]
    args = [x, a, a, a, inv_count, o_win, o_glb, w_pool_bd, pool_scale, w_out[0], g_post, mod,
            g_pre_f, g_post_f, w_gate[0], w_up[0], w_down[0]]
    out_shape = (jax.ShapeDtypeStruct((b, n, d), F32),)
    out_specs = (pl.BlockSpec((None, tm, d), tile),)
    scratch = [pltpu.VMEM((tm + 2 * POOL_HALO + POOL_HALO // 2, pw), F32),
               pltpu.VMEM((tm + 2 * POOL_HALO, pw), F32), pltpu.VMEM((tm + 2 * POOL_HALO, pw), F32),
               pltpu.VMEM((tm + 2 * POOL_HALO, LANES), F32)]
    if with_proj:
        mod_i, g_pre_i, w_in, cos, sin, gq, gk, qfill = proj
        nw = w_in[0].shape[2]
        in_specs += [
            pl.BlockSpec((None, N_MOD, d), lambda s: (cur(s)[0], 0, 0)),
            pl.BlockSpec((1, d), row),
            _layer_spec(w_in),
            pl.BlockSpec((tm, LANES), lambda s: (prev(s)[1], 0)),
            pl.BlockSpec((tm, LANES), lambda s: (prev(s)[1], 0)),
            pl.BlockSpec((1, LANES), row),
            pl.BlockSpec((1, LANES), row),
            pl.BlockSpec((1, LANES), row),
        ]
        args += [mod_i, g_pre_i, w_in[0], cos, sin, gq, gk, qfill]
        proj_shape, proj_specs = _in_proj_outputs(b, n, tm, prev)
        out_shape += proj_shape
        out_specs += proj_specs
        scratch = [pltpu.VMEM((tm, nw), F32)] + scratch
    outs = pl.pallas_call(
        functools.partial(_tail_kernel, tpb=tpb, chunk=chunk, st=min(tm, 256), with_proj=with_proj),
        out_shape=out_shape,
        grid=(b * tpb + (1 if with_proj else 0),),
        in_specs=in_specs,
        out_specs=out_specs,
        scratch_shapes=scratch,
        compiler_params=_cparams("arbitrary"),
        name="tail_proj" if with_proj else "tail",
    )(*args)
    return outs[0], outs[1:]


def _rope_tables(n):
    rows = n // GRID_W
    row = jnp.broadcast_to(jnp.arange(rows)[:, None], (rows, GRID_W)).reshape(-1).astype(F32)
    col = jnp.broadcast_to(jnp.arange(GRID_W)[None, :], (rows, GRID_W)).reshape(-1).astype(F32)
    freq = ROPE_THETA ** (-jnp.arange(ROPE_FREQS, dtype=F32) / ROPE_FREQS)
    ar, ac = row[:, None] * freq, col[:, None] * freq
    cos = jnp.concatenate([jnp.cos(ar), jnp.cos(ar), jnp.cos(ac), jnp.cos(ac)], axis=1)
    sin = jnp.concatenate([-jnp.sin(ar), jnp.sin(ar), -jnp.sin(ac), jnp.sin(ac)], axis=1)
    return jnp.tile(cos, (1, LANES // HEAD_DIM)), jnp.tile(sin, (1, LANES // HEAD_DIM))


def _shift_row(m):
    pieces, rest = [], m.astype(F32)
    for _ in range(N_SHIFT):
        part = rest.astype(BF16).astype(F32)
        pieces.append(-part)
        rest = rest - part
    return jnp.zeros((1, LANES), F32).at[0, HEAD_DIM:HEAD_DIM + N_SHIFT].set(jnp.stack(pieces))


def _pool_inv_counts(n):
    t = jnp.arange(n)
    cols = []
    for w in POOL_WINDOWS:
        hi = jnp.clip(t + (w - 1 - w // 2), 0, n - 1)
        lo = jnp.clip(t - w // 2, 0, n - 1)
        cols.append(jnp.broadcast_to((1.0 / (hi - lo + 1).astype(F32))[:, None], (n, HEAD_DIM)))
    return jnp.concatenate(cols, axis=1)


def _tile(n, pref):
    t = min(n, pref)
    assert n % t == 0, (n, t)
    return t


def kernel(x, c, ctx, c_ctx, w_mod, b_mod, g_pre_mix, g_post_mix, g_pre_ffn, g_post_ffn, w_in, w_pool, pool_scale, win_sink, g_qnorm, g_knorm, w_out, w_gate, w_up, w_down):
    b, n, d = x.shape
    lc = ctx.shape[1]
    depth = w_mod.shape[0]
    assert w_in.shape[2] == IN_WIDTH and KV_WIDTH == LANES and n % GRID_W == 0

    cos, sin = _rope_tables(n)
    cos_c, sin_c = jnp.ones((lc, LANES), F32), jnp.zeros((lc, LANES), F32)

    mod_rows = -(-(b + 1) // 8) * 8
    cc = jnp.concatenate([c, c_ctx[None, :], jnp.zeros((mod_rows - b - 1, d), F32)], axis=0)
    mod_all = _modulation(cc, w_mod, b_mod)

    tm = _tile(n, 512)
    tq_win = _tile(n, 1024)
    tq_glb = _tile(n, 512)
    tk_glb = n // (N_KV_HEADS * (tq_glb // WINDOW))
    assert tk_glb % LANES == 0
    rb_glb = 512
    ffn_chunk = 256

    def layer(l):
        bound = ATTN_SCALE * HEAD_DIM * jnp.max(jnp.abs(g_qnorm[l])) * jnp.max(jnp.abs(g_knorm[l]))
        return dict(
            m=mod_all[l, :b].reshape(b, N_MOD, d),
            mc=jnp.broadcast_to(mod_all[l, b].reshape(1, N_MOD, d), (b, N_MOD, d)),
            w_in=(w_in_b, l),
            gq=jnp.tile(g_qnorm[l], LANES // HEAD_DIM)[None, :],
            gk=jnp.tile(g_knorm[l], LANES // HEAD_DIM)[None, :],
            g_pre=g_pre_mix[l][None, :],
            bound=bound,
            qfill=jnp.where(bound <= SAFE_SHIFT, _shift_row(bound), 0.0),
        )

    w_in_b, w_out_b, w_gate_b, w_up_b, w_down_b = (w.astype(BF16) for w in (w_in, w_out, w_gate, w_up, w_down))
    no_fill = jnp.zeros((1, LANES), F32)
    icnt, icnt_c = _pool_inv_counts(n), _pool_inv_counts(lc)
    xc = ctx
    p = layer(0)
    proj = _in_proj(x, p["m"], p["g_pre"], p["w_in"], cos, sin, p["gq"], p["gk"], p["qfill"], tm, carried=True)
    for l in range(depth):
        last = l == depth - 1
        m, mc, gq, gk = p["m"], p["mc"], p["gq"], p["gk"]
        w_out_l, w_gate_l, w_up_l, w_down_l = (w_out_b, l), (w_gate_b, l), (w_up_b, l), (w_down_b, l)
        w_pool_bd = jax.scipy.linalg.block_diag(*[w_pool[l, g] for g in range(len(POOL_WINDOWS))]).astype(BF16)
        pscale = pool_scale[l][None, :]
        sink = jnp.zeros((8, LANES), F32).at[:N_Q_HEADS].set(jnp.broadcast_to(win_sink[l][:, None], (N_Q_HEADS, LANES)))
        g_post = g_post_mix[l][None, :]
        g_pre_f, g_post_f = g_pre_ffn[l][None, :], g_post_ffn[l][None, :]

        a, qw, ktw, vw, qg, ktg, vg = proj
        ac, qwc, ktwc, vwc, qgc, ktgc, vgc = _in_proj(xc, mc, p["g_pre"], p["w_in"], cos_c, sin_c, gq, gk, no_fill, lc)

        def attn_bounded(*ops):
            o_g, o_w = _attn(*ops, tq=tq_glb, tk=tk_glb, rb=rb_glb)
            return o_w, o_g

        def attn_exact(qg, ktg, vg, ktgc, vgc, qw, ktw, vw, ktwc, vwc, sink):
            return (_window_attn(qw, ktw, vw, ktwc, vwc, sink, tq_win),
                    _global_attn(qg, ktg, vg, ktgc, vgc, tq=tq_glb, tk=tk_glb, rb=rb_glb))

        o_win, o_glb = lax.cond(p["bound"] <= SAFE_SHIFT, attn_bounded, attn_exact,
                                qg, ktg, vg, ktgc, vgc, qw, ktw, vw, ktwc, vwc, sink)
        tail_args = (x, a, icnt, o_win, o_glb, w_pool_bd, pscale, w_out_l, g_post, m, g_pre_f, g_post_f,
                     w_gate_l, w_up_l, w_down_l)
        if last:
            x, _ = _tail(*tail_args, None, tm, ffn_chunk)
        else:
            oc_win = _ctx_attn(qwc, ktwc, vwc, sink)
            oc_glb = _ctx_attn(qgc, ktgc, vgc, None)
            xc = _out_proj(xc, ac, icnt_c, oc_win, oc_glb, w_pool_bd, pscale, w_out_l, g_post, mc, lc)
            xc = _ffn(xc, mc, g_pre_f, g_post_f, w_gate_l, w_up_l, w_down_l, lc, ffn_chunk)
            p = layer(l + 1)
            x, proj = _tail(*tail_args, (p["m"], p["g_pre"], p["w_in"], cos, sin, p["gq"], p["gk"], p["qfill"]),
                            tm, ffn_chunk)
    return x
```

```python
import functools

import jax
import jax.numpy as jnp
from jax import lax
from jax.experimental import pallas as pl
from jax.experimental.pallas import tpu as pltpu

F32 = jnp.float32
BF16 = jnp.bfloat16

HEAD_DIM = 64
GRID_W = 64
ROPE_FREQS = HEAD_DIM // 4
ROPE_THETA = 10000.0
NORM_EPS = 1e-6
NEG_INF = -1e30
ATTN_SCALE = HEAD_DIM ** -0.5
WINDOW = 128
POOL_WINDOWS = (2, 4, 8, 16)
POOL_HALO = 16
N_Q_HEADS = 6
N_KV_HEADS = 2
KV_GROUP = N_Q_HEADS // N_KV_HEADS
N_MOD = 6
POOL_WIDTH = len(POOL_WINDOWS) * HEAD_DIM
Q_WIDTH = N_Q_HEADS * HEAD_DIM
KV_WIDTH = N_KV_HEADS * HEAD_DIM
COL_QW = POOL_WIDTH
COL_KW = COL_QW + Q_WIDTH
COL_VW = COL_KW + KV_WIDTH
COL_QG = COL_VW + KV_WIDTH
COL_KG = COL_QG + Q_WIDTH
COL_VG = COL_KG + KV_WIDTH
IN_WIDTH = COL_VG + KV_WIDTH
N_SHIFT = 3
SAFE_SHIFT = 30.0

LANES = 128
VMEM_LIMIT = 56 * 1024 * 1024


def _cparams(*sem):
    return pltpu.CompilerParams(dimension_semantics=sem, vmem_limit_bytes=VMEM_LIMIT)


def _resident(block_shape, index_map):
    return pl.BlockSpec(block_shape, index_map, pipeline_mode=pl.Buffered(1))


def _layer_spec(wl):
    w, l = wl
    return pl.BlockSpec((None,) + w.shape[1:], lambda *g: (l,) + (0,) * (w.ndim - 1), pipeline_mode=pl.Buffered(1))


def _rms(x, g):
    ms = jnp.mean(x * x, axis=-1, keepdims=True)
    return (x * lax.rsqrt(ms + NORM_EPS)) * g


def _mod_kernel(c_ref, w_ref, b_ref, o_ref):
    c = c_ref[...]
    s = c * jax.nn.sigmoid(c)
    o_ref[...] = jnp.dot(s.astype(BF16), w_ref[...].astype(BF16), preferred_element_type=F32) + b_ref[...]


def _modulation(cc, w_mod, b_mod):
    depth, d, dm = w_mod.shape
    rows = cc.shape[0]
    tn = dm // 4
    return pl.pallas_call(
        _mod_kernel,
        out_shape=jax.ShapeDtypeStruct((depth, rows, dm), F32),
        grid=(depth, dm // tn),
        in_specs=[
            pl.BlockSpec((rows, d), lambda l, j: (0, 0)),
            pl.BlockSpec((None, d, tn), lambda l, j: (l, 0, j)),
            pl.BlockSpec((None, 1, tn), lambda l, j: (l, 0, j)),
        ],
        out_specs=pl.BlockSpec((None, rows, tn), lambda l, j: (l, 0, j)),
        compiler_params=_cparams("arbitrary", "arbitrary"),
        name="modulation",
    )(cc, w_mod, b_mod.reshape(depth, 1, dm))


def _in_proj_epilogue(u_of, n_rows, st, cos_ref, sin_ref, gq_ref, gk_ref, qfill_ref,
                      a_ref, qw_ref, ktw_ref, vw_ref, qg_ref, ktg_ref, vg_ref):
    lane = lax.broadcasted_iota(jnp.int32, (st, LANES), 1)
    lo = lane < HEAD_DIM
    first = (lane & ROPE_FREQS) == 0
    shift_rows = (lax.broadcasted_iota(jnp.int32, (HEAD_DIM, LANES), 0) < N_SHIFT).astype(BF16)
    sum_lane = (lane == HEAD_DIM).astype(F32)
    gq, gk = gq_ref[...], gk_ref[...]
    qfill = qfill_ref[...]

    def head_norm(t, g):
        sq = t * t
        s_lo = jnp.sum(jnp.where(lo, sq, 0.0), axis=-1, keepdims=True)
        s_hi = jnp.sum(jnp.where(lo, 0.0, sq), axis=-1, keepdims=True)
        ms = jnp.where(lo, s_lo, s_hi) * (1.0 / HEAD_DIM)
        return (t * lax.rsqrt(ms + NORM_EPS)) * g

    for sub in range(n_rows // st):
        rs = slice(sub * st, (sub + 1) * st)
        u = u_of(sub)
        cos = cos_ref[rs, :]
        sin = sin_ref[rs, :]

        def rope(t):
            partner = jnp.where(first, pltpu.roll(t, LANES - ROPE_FREQS, 1), pltpu.roll(t, ROPE_FREQS, 1))
            return t * cos + partner * sin

        def store_branch(q_chunks, k, v, fill, q_ref, kt_ref, v_ref):
            for hd in range(N_Q_HEADS):
                t = q_chunks[hd // 2]
                if hd % 2:
                    t = pltpu.roll(t, HEAD_DIM, 1)
                q_ref[hd, rs, :] = jnp.where(lo, t, fill).astype(BF16)
            for r in range(st // LANES):
                kb = sub * (st // LANES) + r
                kt = jnp.transpose(k[r * LANES:(r + 1) * LANES, :]).astype(BF16)
                for kv in range(N_KV_HEADS):
                    kt_ref[kv, kb, 0:HEAD_DIM, :] = kt[kv * HEAD_DIM:(kv + 1) * HEAD_DIM, :]
                    kt_ref[kv, kb, HEAD_DIM:LANES, :] = shift_rows
            v_ref[0, rs, :] = jnp.where(lo, v, sum_lane).astype(BF16)
            v_ref[1, rs, :] = jnp.where(lo, pltpu.roll(v, HEAD_DIM, 1), sum_lane).astype(BF16)

        def cols(start, width):
            return u[:, start:start + width]

        a_ref[rs, :] = cols(0, POOL_WIDTH)
        store_branch([rope(cols(COL_QW + LANES * c, LANES)) * ATTN_SCALE for c in range(Q_WIDTH // LANES)],
                     rope(cols(COL_KW, KV_WIDTH)), cols(COL_VW, KV_WIDTH), 0.0, qw_ref, ktw_ref, vw_ref)
        store_branch([rope(head_norm(cols(COL_QG + LANES * c, LANES), gq)) * ATTN_SCALE for c in range(Q_WIDTH // LANES)],
                     rope(head_norm(cols(COL_KG, KV_WIDTH), gk)), cols(COL_VG, KV_WIDTH), qfill, qg_ref, ktg_ref, vg_ref)


def _project(x, mod_ref, g_ref, w_ref):
    h = _rms(x, g_ref[...])
    h = h * (1.0 + mod_ref[1:2, :]) + mod_ref[0:1, :]
    return jnp.dot(h.astype(BF16), w_ref[...], preferred_element_type=F32)


def _in_proj_kernel(x_ref, mod_ref, g_ref, w_ref, cos_ref, sin_ref, gq_ref, gk_ref, qfill_ref, *out_refs, st):
    tm = x_ref.shape[0]
    us = [_project(x_ref[sub * st:(sub + 1) * st, :], mod_ref, g_ref, w_ref) for sub in range(tm // st)]
    _in_proj_epilogue(lambda sub: us[sub], tm, st, cos_ref, sin_ref, gq_ref, gk_ref, qfill_ref, *out_refs)


def _in_proj_outputs(b, n, tm, tile_of):
    qshape = jax.ShapeDtypeStruct((b, N_Q_HEADS, n, LANES), BF16)
    ktshape = jax.ShapeDtypeStruct((b, N_KV_HEADS, n // LANES, LANES, LANES), BF16)
    vshape = jax.ShapeDtypeStruct((b, N_KV_HEADS, n, LANES), BF16)

    def spec(block, where):
        return pl.BlockSpec(block, lambda *g: where(*tile_of(*g)))

    aspec = spec((None, tm, POOL_WIDTH), lambda bi, i: (bi, i, 0))
    qspec = spec((None, N_Q_HEADS, tm, LANES), lambda bi, i: (bi, 0, i, 0))
    ktspec = spec((None, N_KV_HEADS, tm // LANES, LANES, LANES), lambda bi, i: (bi, 0, i, 0, 0))
    vspec = spec((None, N_KV_HEADS, tm, LANES), lambda bi, i: (bi, 0, i, 0))
    shapes = (jax.ShapeDtypeStruct((b, n, POOL_WIDTH), F32), qshape, ktshape, vshape, qshape, ktshape, vshape)
    return shapes, (aspec, qspec, ktspec, vspec, qspec, ktspec, vspec)


def _in_proj_carried_kernel(x_ref, mod_ref, g_ref, w_ref, cos_ref, sin_ref, gq_ref, gk_ref, qfill_ref, *refs, st):
    out_refs, u_ref = refs[:-1], refs[-1]

    @pl.when(pl.program_id(0) == 0)
    def _():
        u_ref[...] = jnp.zeros(u_ref.shape, F32)

    _in_proj_epilogue(lambda sub: u_ref[sub * st:(sub + 1) * st, :], x_ref.shape[0], st, cos_ref, sin_ref,
                      gq_ref, gk_ref, qfill_ref, *out_refs)
    u_ref[...] = _project(x_ref[...], mod_ref, g_ref, w_ref)


def _in_proj(x, mod, g_pre, w_in, cos, sin, gq, gk, qfill, tm, carried=False):
    b, n, d = x.shape
    nw = w_in[0].shape[2]
    tpb = n // tm
    last = b * tpb - 1
    if carried:
        grid = (b * tpb + 1,)
        cur = lambda s: (jnp.minimum(s, last) // tpb, jnp.minimum(s, last) % tpb)
        prev = lambda s: (jnp.maximum(s - 1, 0) // tpb, jnp.maximum(s - 1, 0) % tpb)
        kern, scratch, sem = _in_proj_carried_kernel, [pltpu.VMEM((tm, nw), F32)], ("arbitrary",)
    else:
        grid = (b, tpb)
        cur = prev = lambda bi, i: (bi, i)
        kern, scratch, sem = _in_proj_kernel, [], ("arbitrary", "arbitrary")
    out_shape, out_specs = _in_proj_outputs(b, n, tm, prev)
    row = lambda *g: (0, 0)
    return pl.pallas_call(
        functools.partial(kern, st=min(tm, 512)),
        out_shape=out_shape,
        grid=grid,
        in_specs=[
            pl.BlockSpec((None, tm, d), lambda *g: (cur(*g)[0], cur(*g)[1], 0)),
            pl.BlockSpec((None, N_MOD, d), lambda *g: (cur(*g)[0], 0, 0)),
            pl.BlockSpec((1, d), row),
            _layer_spec(w_in),
            pl.BlockSpec((tm, LANES), lambda *g: (prev(*g)[1], 0)),
            pl.BlockSpec((tm, LANES), lambda *g: (prev(*g)[1], 0)),
            pl.BlockSpec((1, LANES), row),
            pl.BlockSpec((1, LANES), row),
            pl.BlockSpec((1, LANES), row),
        ],
        out_specs=out_specs,
        scratch_shapes=scratch,
        compiler_params=_cparams(*sem),
        name="in_proj",
    )(x, mod, g_pre, w_in[0], cos, sin, gq, gk, qfill)


def _merge_heads(outs, lo):
    return [jnp.where(lo, outs[2 * c], pltpu.roll(outs[2 * c + 1], HEAD_DIM, 1)) for c in range(N_Q_HEADS // 2)]


def _key_tile(kt_ref, first_block, n_blocks):
    return jnp.concatenate([kt_ref[first_block + j] for j in range(n_blocks)], axis=1)


def _stacked_sink(sink_ref, kv, rows_per_head):
    return jnp.concatenate([jnp.broadcast_to(sink_ref[hd:hd + 1, 0:1], (rows_per_head, 1))
                            for hd in range(kv * KV_GROUP, (kv + 1) * KV_GROUP)], axis=0)


def _window_units(q_ref, kt_ref, v_ref, ktc_ref, vc_ref, sink_ref, first_blk):
    nkb = kt_ref.shape[1]
    sub = WINDOW
    rows = KV_GROUP * sub
    r_in = lax.broadcasted_iota(jnp.int32, (rows, LANES), 0) & (sub - 1)
    c_in = lax.broadcasted_iota(jnp.int32, (rows, LANES), 1)
    ctx = [(_key_tile(ktc_ref.at[kv], 0, ktc_ref.shape[1]), vc_ref[kv], _stacked_sink(sink_ref, kv, sub))
           for kv in range(N_KV_HEADS)]

    def blocks(j):
        blk = first_blk + j
        return blk, jnp.maximum(blk - 1, 0), jnp.minimum(blk + 1, nkb - 1)

    def logits(j, kv):
        blk, prev_blk, next_blk = blocks(j)
        q = q_ref[kv * KV_GROUP:(kv + 1) * KV_GROUP, pl.ds(pl.multiple_of(j * sub, sub), sub), :].reshape(rows, LANES)
        kt = jnp.concatenate([kt_ref[kv, prev_blk], kt_ref[kv, blk], kt_ref[kv, next_blk]], axis=1)
        s = jnp.dot(q, kt, preferred_element_type=F32)
        sc = jnp.dot(q, ctx[kv][0], preferred_element_type=F32)
        return s, sc

    def attend(j, kv, s, sc):
        blk, prev_blk, next_blk = blocks(j)
        _, vc, sk = ctx[kv]
        v = jnp.concatenate([v_ref[kv, pl.ds(pl.multiple_of(kb * sub, sub), sub), :]
                             for kb in (prev_blk, blk, next_blk)], axis=0)
        s_prev = jnp.where(jnp.logical_and(c_in >= r_in, blk > 0), s[:, 0:sub], NEG_INF)
        s_own = s[:, sub:2 * sub]
        s_next = jnp.where(jnp.logical_and(c_in <= r_in, blk < nkb - 1), s[:, 2 * sub:3 * sub], NEG_INF)
        mx = jnp.maximum(jnp.maximum(s_prev, s_own), s_next)
        for t in range(sc.shape[1] // LANES):
            mx = jnp.maximum(mx, sc[:, t * LANES:(t + 1) * LANES])
        m = jnp.maximum(jnp.max(mx, axis=-1, keepdims=True), sk)
        p = jnp.concatenate([jnp.exp(s_prev - m), jnp.exp(s_own - m), jnp.exp(s_next - m)], axis=1)
        pv = jnp.dot(p.astype(BF16), v, preferred_element_type=F32)
        pv = pv + jnp.dot(jnp.exp(sc - m).astype(BF16), vc, preferred_element_type=F32)
        l = pv[:, HEAD_DIM:HEAD_DIM + 1] + jnp.exp(sk - m)
        o = pv / l
        return [o[g * sub:(g + 1) * sub] for g in range(KV_GROUP)]

    return logits, attend


def _store_heads(o_ref, row0, outs):
    nrows = outs[0].shape[0]
    lo = lax.broadcasted_iota(jnp.int32, (nrows, LANES), 1) < HEAD_DIM
    for c, chunk in enumerate(_merge_heads(outs, lo)):
        o_ref[pl.ds(row0, nrows), c * LANES:(c + 1) * LANES] = chunk.astype(BF16)


def _window_attn_kernel(q_ref, kt_ref, v_ref, ktc_ref, vc_ref, sink_ref, o_ref):
    tq = q_ref.shape[1]
    sub = WINDOW
    logits, attend = _window_units(q_ref, kt_ref, v_ref, ktc_ref, vc_ref, sink_ref, pl.program_id(1) * (tq // sub))
    units = [(j, kv) for j in range(tq // sub) for kv in range(N_KV_HEADS)]
    pending = logits(*units[0])
    outs = []
    for idx, (j, kv) in enumerate(units):
        ahead = logits(*units[idx + 1]) if idx + 1 < len(units) else None
        outs += attend(j, kv, *pending)
        pending = ahead
        if kv == N_KV_HEADS - 1:
            _store_heads(o_ref, j * sub, outs)
            outs = []


def _window_attn(q, kt, v, ktc, vc, sink, tq):
    b, _, n, _ = q.shape
    lc = vc.shape[2]
    assert tq % WINDOW == 0 and WINDOW == LANES
    batch4 = lambda bi, i: (bi, 0, 0, 0)
    batch5 = lambda bi, i: (bi, 0, 0, 0, 0)
    return pl.pallas_call(
        _window_attn_kernel,
        out_shape=jax.ShapeDtypeStruct((b, n, N_Q_HEADS * HEAD_DIM), BF16),
        grid=(b, n // tq),
        in_specs=[
            pl.BlockSpec((None, N_Q_HEADS, tq, LANES), lambda bi, i: (bi, 0, i, 0)),
            pl.BlockSpec((None, N_KV_HEADS, n // LANES, LANES, LANES), batch5),
            pl.BlockSpec((None, N_KV_HEADS, n, LANES), batch4),
            pl.BlockSpec((None, N_KV_HEADS, lc // LANES, LANES, LANES), batch5),
            pl.BlockSpec((None, N_KV_HEADS, lc, LANES), batch4),
            pl.BlockSpec(sink.shape, lambda bi, i: (0, 0)),
        ],
        out_specs=pl.BlockSpec((None, tq, N_Q_HEADS * HEAD_DIM), lambda bi, i: (bi, i, 0)),
        compiler_params=_cparams("arbitrary", "arbitrary"),
        name="window_attn",
    )(q, kt, v, ktc, vc, sink)


def _ctx_attn_kernel(*refs, has_sink):
    if has_sink:
        q_ref, kt_ref, v_ref, sink_ref, o_ref = refs
    else:
        q_ref, kt_ref, v_ref, o_ref = refs
    tq = q_ref.shape[1]
    rows = KV_GROUP * tq
    outs = []
    for kv in range(N_KV_HEADS):
        q = q_ref[kv * KV_GROUP:(kv + 1) * KV_GROUP].reshape(rows, LANES)
        s = jnp.dot(q, _key_tile(kt_ref.at[kv], 0, kt_ref.shape[1]), preferred_element_type=F32)
        m = jnp.max(s, axis=-1, keepdims=True)
        if has_sink:
            sk = _stacked_sink(sink_ref, kv, tq)
            m = jnp.maximum(m, sk)
        pv = jnp.dot(jnp.exp(s - m).astype(BF16), v_ref[kv], preferred_element_type=F32)
        l = pv[:, HEAD_DIM:HEAD_DIM + 1]
        if has_sink:
            l = l + jnp.exp(sk - m)
        o = pv / l
        outs += [o[g * tq:(g + 1) * tq] for g in range(KV_GROUP)]
    _store_heads(o_ref, 0, outs)


def _ctx_attn(q, kt, v, sink):
    b, _, n, _ = q.shape
    args = [q, kt, v]
    in_specs = [pl.BlockSpec((None,) + a.shape[1:], lambda bi, nd=a.ndim: (bi,) + (0,) * (nd - 1)) for a in args]
    if sink is not None:
        args.append(sink)
        in_specs.append(pl.BlockSpec(sink.shape, lambda bi: (0, 0)))
    return pl.pallas_call(
        functools.partial(_ctx_attn_kernel, has_sink=sink is not None),
        out_shape=jax.ShapeDtypeStruct((b, n, N_Q_HEADS * HEAD_DIM), BF16),
        grid=(b,),
        in_specs=in_specs,
        out_specs=pl.BlockSpec((None, n, N_Q_HEADS * HEAD_DIM), lambda bi: (bi, 0, 0)),
        compiler_params=_cparams("arbitrary"),
        name="ctx_attn",
    )(*args)


def _global_attn_kernel(q_ref, kt_ref, v_ref, ktc_ref, vc_ref, o_ref, qs_ref, acc_ref, mx_ref, *, tk, rb):
    tq = q_ref.shape[1]
    rows = KV_GROUP * tq
    kb_per_step = tk // LANES
    n_steps = kt_ref.shape[1] // kb_per_step
    lane = lax.broadcasted_iota(jnp.int32, (rows, LANES), 1)

    def key_steps(fn):
        def step(kc, carry):
            for kv in range(N_KV_HEADS):
                fn(kv, _key_tile(kt_ref.at[kv], kc * kb_per_step, kb_per_step),
                   v_ref[kv, pl.ds(pl.multiple_of(kc * tk, tk), tk), :])
            return carry
        lax.fori_loop(0, n_steps, step, 0)
        for kv in range(N_KV_HEADS):
            fn(kv, _key_tile(ktc_ref.at[kv], 0, ktc_ref.shape[1]), vc_ref[kv])

    def q_rows(kv):
        return q_ref[kv * KV_GROUP:(kv + 1) * KV_GROUP].reshape(rows, LANES)

    mx_ref[...] = jnp.full(mx_ref.shape, NEG_INF, F32)

    def track_max(kv, kt, v):
        s = jnp.dot(q_rows(kv), kt, preferred_element_type=F32)
        mx = mx_ref[kv]
        for j in range(kt.shape[1] // LANES):
            mx = jnp.maximum(mx, s[:, j * LANES:(j + 1) * LANES])
        mx_ref[kv] = mx

    key_steps(track_max)

    for kv in range(N_KV_HEADS):
        qs = q_rows(kv).astype(F32)
        rest = jnp.max(mx_ref[kv], axis=-1, keepdims=True)
        for piece in range(N_SHIFT):
            part = rest.astype(BF16).astype(F32)
            qs = jnp.where(lane == HEAD_DIM + piece, -part, qs)
            rest = rest - part
        qs_ref[kv] = qs.astype(BF16)

    acc_ref[...] = jnp.zeros(acc_ref.shape, F32)

    def accumulate(kv, kt, v):
        for r in range(rows // rb):
            rsl = slice(r * rb, (r + 1) * rb)
            s = jnp.dot(qs_ref[kv, rsl, :], kt, preferred_element_type=F32)
            p = jnp.exp(s).astype(BF16)
            acc_ref[kv, rsl, :] += jnp.dot(p, v, preferred_element_type=F32)

    key_steps(accumulate)

    outs = []
    for hd in range(N_Q_HEADS):
        a = acc_ref[hd // KV_GROUP, (hd % KV_GROUP) * tq:(hd % KV_GROUP + 1) * tq, :]
        outs.append(a / a[:, HEAD_DIM:HEAD_DIM + 1])
    _store_heads(o_ref, 0, outs)


def _global_attn(q, kt, v, ktc, vc, *, tq, tk, rb):
    b, _, n, _ = q.shape
    lc = vc.shape[2]
    rows = KV_GROUP * tq
    batch4 = lambda bi, i: (bi, 0, 0, 0)
    batch5 = lambda bi, i: (bi, 0, 0, 0, 0)
    return pl.pallas_call(
        functools.partial(_global_attn_kernel, tk=tk, rb=rb),
        out_shape=jax.ShapeDtypeStruct((b, n, N_Q_HEADS * HEAD_DIM), BF16),
        grid=(b, n // tq),
        in_specs=[
            pl.BlockSpec((None, N_Q_HEADS, tq, LANES), lambda bi, i: (bi, 0, i, 0)),
            pl.BlockSpec((None, N_KV_HEADS, n // LANES, LANES, LANES), batch5),
            pl.BlockSpec((None, N_KV_HEADS, n, LANES), batch4),
            pl.BlockSpec((None, N_KV_HEADS, lc // LANES, LANES, LANES), batch5),
            pl.BlockSpec((None, N_KV_HEADS, lc, LANES), batch4),
        ],
        out_specs=pl.BlockSpec((None, tq, N_Q_HEADS * HEAD_DIM), lambda bi, i: (bi, i, 0)),
        scratch_shapes=[pltpu.VMEM((N_KV_HEADS, rows, LANES), BF16), pltpu.VMEM((N_KV_HEADS, rows, LANES), F32),
                        pltpu.VMEM((N_KV_HEADS, rows, LANES), F32)],
        compiler_params=_cparams("arbitrary", "arbitrary"),
        name="global_attn_exact_max",
    )(q, kt, v, ktc, vc)


def _attn_kernel(qg_ref, ktg_ref, vg_ref, ktgc_ref, vgc_ref, qw_ref, ktw_ref, vw_ref, ktwc_ref, vwc_ref, sink_ref,
                 og_ref, ow_ref, acc_ref, *, tk, rb):
    tq = qg_ref.shape[1]
    rows = KV_GROUP * tq
    sub = WINDOW
    n_sub = tq // sub
    kb_per_step = tk // LANES
    steps_per_sub = ktg_ref.shape[1] // kb_per_step // n_sub
    assert steps_per_sub * n_sub * kb_per_step == ktg_ref.shape[1] and tq % rb == 0

    def global_keys(kv, kt, v, assign):
        for r in range(rows // rb):
            hd, off = kv * KV_GROUP + (r * rb) // tq, (r * rb) % tq
            s = jnp.dot(qg_ref[hd, off:off + rb, :], kt, preferred_element_type=F32)
            pv = jnp.dot(jnp.exp(s).astype(BF16), v, preferred_element_type=F32)
            if assign:
                acc_ref[kv, r * rb:(r + 1) * rb, :] = pv
            else:
                acc_ref[kv, r * rb:(r + 1) * rb, :] += pv

    def global_step(kc):
        for kv in range(N_KV_HEADS):
            global_keys(kv, _key_tile(ktg_ref.at[kv], kc * kb_per_step, kb_per_step),
                        vg_ref[kv, pl.ds(pl.multiple_of(kc * tk, tk), tk), :], False)

    for kv in range(N_KV_HEADS):
        global_keys(kv, _key_tile(ktgc_ref.at[kv], 0, ktgc_ref.shape[1]), vgc_ref[kv], True)

    logits, attend = _window_units(qw_ref, ktw_ref, vw_ref, ktwc_ref, vwc_ref, sink_ref, pl.program_id(1) * n_sub)

    def body(j, carry):
        pending = logits(j, 0)
        outs = []
        for kv in range(N_KV_HEADS):
            for t in range(steps_per_sub // N_KV_HEADS):
                global_step(j * steps_per_sub + kv * (steps_per_sub // N_KV_HEADS) + t)
            ahead = logits(j, kv + 1) if kv + 1 < N_KV_HEADS else None
            outs += attend(j, kv, *pending)
            pending = ahead
        _store_heads(ow_ref, pl.multiple_of(j * sub, sub), outs)
        return carry

    assert steps_per_sub % N_KV_HEADS == 0
    lax.fori_loop(0, n_sub, body, 0, unroll=True)

    outs = []
    for hd in range(N_Q_HEADS):
        a = acc_ref[hd // KV_GROUP, (hd % KV_GROUP) * tq:(hd % KV_GROUP + 1) * tq, :]
        outs.append(a / a[:, HEAD_DIM:HEAD_DIM + 1])
    _store_heads(og_ref, 0, outs)


def _attn(qg, ktg, vg, ktgc, vgc, qw, ktw, vw, ktwc, vwc, sink, *, tq, tk, rb):
    b, _, n, _ = qg.shape
    lc = vgc.shape[2]
    batch4 = lambda bi, i: (bi, 0, 0, 0)
    batch5 = lambda bi, i: (bi, 0, 0, 0, 0)
    branch_specs = [
        pl.BlockSpec((None, N_Q_HEADS, tq, LANES), lambda bi, i: (bi, 0, i, 0)),
        pl.BlockSpec((None, N_KV_HEADS, n // LANES, LANES, LANES), batch5),
        pl.BlockSpec((None, N_KV_HEADS, n, LANES), batch4),
        pl.BlockSpec((None, N_KV_HEADS, lc // LANES, LANES, LANES), batch5),
        pl.BlockSpec((None, N_KV_HEADS, lc, LANES), batch4),
    ]
    oshape = jax.ShapeDtypeStruct((b, n, N_Q_HEADS * HEAD_DIM), BF16)
    ospec = pl.BlockSpec((None, tq, N_Q_HEADS * HEAD_DIM), lambda bi, i: (bi, i, 0))
    return pl.pallas_call(
        functools.partial(_attn_kernel, tk=tk, rb=rb),
        out_shape=(oshape, oshape),
        grid=(b, n // tq),
        in_specs=branch_specs + branch_specs + [pl.BlockSpec(sink.shape, lambda bi, i: (0, 0))],
        out_specs=(ospec, ospec),
        scratch_shapes=[pltpu.VMEM((N_KV_HEADS, KV_GROUP * tq, LANES), F32)],
        compiler_params=_cparams("arbitrary", "arbitrary"),
        name="attn",
    )(qg, ktg, vg, ktgc, vgc, qw, ktw, vw, ktwc, vwc, sink)


def _out_proj_tile(has_prev, has_next, x_ref, a_ref, ap_ref, an_ref, icnt_ref, ow_ref, og_ref, wpool_ref, pscale_ref,
                   wo_ref, g_ref, mod_ref, ext_ref, s2_ref, s4_ref, s8_ref):
    tm = x_ref.shape[0]
    ext_ref[0:POOL_HALO, :] = jnp.where(has_prev, ap_ref[...], 0.0)
    ext_ref[POOL_HALO:POOL_HALO + tm, :] = a_ref[...]
    ext_ref[POOL_HALO + tm:2 * POOL_HALO + tm, :] = jnp.where(has_next, an_ref[...], 0.0)

    pw = a_ref.shape[1]
    aw = ow_ref.shape[1]
    y = jnp.dot(ow_ref[...], wo_ref[pw:pw + aw, :], preferred_element_type=F32)
    y = y + jnp.dot(og_ref[...], wo_ref[pw + aw:pw + 2 * aw, :], preferred_element_type=F32)

    lo = lax.broadcasted_iota(jnp.int32, (tm, LANES), 1) < HEAD_DIM

    pad = POOL_HALO // 2
    ext_ref[2 * POOL_HALO + tm:2 * POOL_HALO + tm + pad, :] = jnp.zeros((pad, ext_ref.shape[1]), F32)
    r2, r4, r8 = tm + 2 * POOL_HALO, tm + 2 * POOL_HALO - pad, tm + POOL_HALO
    s2_ref[0:r2, :] = ext_ref[0:r2, :] + ext_ref[1:r2 + 1, :]
    s4_ref[0:r4, :] = s2_ref[0:r4, :] + s2_ref[2:r4 + 2, :]
    s8_ref[0:r8, :] = s4_ref[0:r8, LANES:2 * LANES] + s4_ref[4:r8 + 4, LANES:2 * LANES]

    def window_sum(w, c):
        first = POOL_HALO - w // 2
        cols = slice(c * LANES, (c + 1) * LANES)
        if w == 2:
            return s2_ref[first:first + tm, cols]
        if w == 4:
            return s4_ref[first:first + tm, cols]
        if w == 8:
            return s8_ref[first:first + tm, :]
        return s8_ref[first:first + tm, :] + s8_ref[first + 8:first + 8 + tm, :]

    feats = []
    for c in range(2):
        w_a, w_b = POOL_WINDOWS[2 * c], POOL_WINDOWS[2 * c + 1]
        x0 = ext_ref[POOL_HALO:POOL_HALO + tm, c * LANES:(c + 1) * LANES]
        mean = jnp.where(lo, window_sum(w_a, c), window_sum(w_b, c)) * icnt_ref[:, c * LANES:(c + 1) * LANES]
        feats.append(mean - x0)
    feats = jnp.concatenate(feats, axis=1).astype(BF16)
    y_pool = jnp.dot(feats, wpool_ref[...], preferred_element_type=F32) * pscale_ref[...]

    y = y + jnp.dot(y_pool.astype(BF16), wo_ref[0:pw, :], preferred_element_type=F32)
    return x_ref[...] + mod_ref[2:3, :] * _rms(y, g_ref[...])


N_OUT_PROJ_IN = 12


def _out_proj_kernel(*refs):
    ins, o_ref, scratch = refs[:N_OUT_PROJ_IN], refs[N_OUT_PROJ_IN], refs[N_OUT_PROJ_IN + 1:]
    i = pl.program_id(1)
    o_ref[...] = _out_proj_tile(i > 0, i < pl.num_programs(1) - 1, *ins, *scratch)


def _out_proj(x, a, inv_count, o_win, o_glb, w_pool_bd, pool_scale, w_out, g_post, mod, tm):
    b, n, d = x.shape
    pw = a.shape[2]
    aw = o_win.shape[2]
    hb = tm // POOL_HALO
    row = lambda bi, i: (0, 0)
    tile = lambda bi, i: (bi, i, 0)
    return pl.pallas_call(
        _out_proj_kernel,
        out_shape=jax.ShapeDtypeStruct((b, n, d), F32),
        grid=(b, n // tm),
        in_specs=[
            pl.BlockSpec((None, tm, d), tile),
            pl.BlockSpec((None, tm, pw), tile),
            pl.BlockSpec((None, POOL_HALO, pw), lambda bi, i: (bi, jnp.maximum(i * hb - 1, 0), 0)),
            pl.BlockSpec((None, POOL_HALO, pw), lambda bi, i: (bi, jnp.minimum((i + 1) * hb, n // POOL_HALO - 1), 0)),
            pl.BlockSpec((tm, pw), lambda bi, i: (i, 0)),
            pl.BlockSpec((None, tm, aw), tile),
            pl.BlockSpec((None, tm, aw), tile),
            _resident((pw, pw), row),
            pl.BlockSpec((1, pw), row),
            _layer_spec(w_out),
            pl.BlockSpec((1, d), row),
            pl.BlockSpec((None, N_MOD, d), lambda bi, i: (bi, 0, 0)),
        ],
        out_specs=pl.BlockSpec((None, tm, d), tile),
        scratch_shapes=[pltpu.VMEM((tm + 2 * POOL_HALO + POOL_HALO // 2, pw), F32),
                        pltpu.VMEM((tm + 2 * POOL_HALO, pw), F32), pltpu.VMEM((tm + 2 * POOL_HALO, pw), F32),
                        pltpu.VMEM((tm + 2 * POOL_HALO, LANES), F32)],
        compiler_params=_cparams("arbitrary", "arbitrary"),
        name="out_proj",
    )(x, a, a, a, inv_count, o_win, o_glb, w_pool_bd, pool_scale, w_out[0], g_post, mod)


def _ffn_tile(x, mod_ref, gpre_ref, gpost_ref, wg_ref, wu_ref, wd_ref, chunk):
    h = _rms(x, gpre_ref[...])
    h = (h * (1.0 + mod_ref[4:5, :]) + mod_ref[3:4, :]).astype(BF16)
    f = None
    for c in range(wg_ref.shape[1] // chunk):
        cols = slice(c * chunk, (c + 1) * chunk)
        g = jnp.dot(h, wg_ref[:, cols], preferred_element_type=F32)
        u = jnp.dot(h, wu_ref[:, cols], preferred_element_type=F32)
        act = ((g * jax.nn.sigmoid(g)) * u).astype(BF16)
        part = jnp.dot(act, wd_ref[cols, :], preferred_element_type=F32)
        f = part if f is None else f + part
    return x + mod_ref[5:6, :] * _rms(f, gpost_ref[...])


def _ffn_kernel(x_ref, mod_ref, gpre_ref, gpost_ref, wg_ref, wu_ref, wd_ref, o_ref, *, chunk):
    o_ref[...] = _ffn_tile(x_ref[...], mod_ref, gpre_ref, gpost_ref, wg_ref, wu_ref, wd_ref, chunk)


def _ffn(x, mod, g_pre, g_post, w_gate, w_up, w_down, tm, chunk):
    b, n, d = x.shape
    row = lambda bi, i: (0, 0)
    tile = lambda bi, i: (bi, i, 0)
    return pl.pallas_call(
        functools.partial(_ffn_kernel, chunk=chunk),
        out_shape=jax.ShapeDtypeStruct((b, n, d), F32),
        grid=(b, n // tm),
        in_specs=[
            pl.BlockSpec((None, tm, d), tile),
            pl.BlockSpec((None, N_MOD, d), lambda bi, i: (bi, 0, 0)),
            pl.BlockSpec((1, d), row),
            pl.BlockSpec((1, d), row),
            _layer_spec(w_gate),
            _layer_spec(w_up),
            _layer_spec(w_down),
        ],
        out_specs=pl.BlockSpec((None, tm, d), tile),
        compiler_params=_cparams("arbitrary", "arbitrary"),
        name="ffn",
    )(x, mod, g_pre, g_post, w_gate[0], w_up[0], w_down[0])


N_FFN_IN = 5
N_PROJ_IN = 8
N_PROJ_OUT = 7


def _tail_kernel(*refs, tpb, chunk, st, with_proj):
    refs = list(refs)
    op_in, refs = refs[:N_OUT_PROJ_IN], refs[N_OUT_PROJ_IN:]
    ffn_in, refs = refs[:N_FFN_IN], refs[N_FFN_IN:]
    if with_proj:
        (modi_ref, gin_ref, win_ref, cos_ref, sin_ref, gq_ref, gk_ref, qfill_ref), refs = refs[:N_PROJ_IN], refs[N_PROJ_IN:]
    o_ref, refs = refs[0], refs[1:]
    if with_proj:
        proj_out, refs = refs[:N_PROJ_OUT], refs[N_PROJ_OUT:]
        u_ref, refs = refs[0], refs[1:]
    pool_scratch = refs
    tm = o_ref.shape[0]
    step = pl.program_id(0)
    i = lax.rem(jnp.minimum(step, pl.num_programs(0) - (2 if with_proj else 1)), tpb)

    if with_proj:
        @pl.when(step == 0)
        def _():
            u_ref[...] = jnp.zeros(u_ref.shape, F32)

        _in_proj_epilogue(lambda sub: u_ref[sub * st:(sub + 1) * st, :], tm, st, cos_ref, sin_ref, gq_ref, gk_ref,
                          qfill_ref, *proj_out)

    x_mid = _out_proj_tile(i > 0, i < tpb - 1, *op_in, *pool_scratch)
    y = _ffn_tile(x_mid, op_in[-1], *ffn_in, chunk)
    o_ref[...] = y
    if with_proj:
        u_ref[...] = _project(y, modi_ref, gin_ref, win_ref)


def _tail(x, a, inv_count, o_win, o_glb, w_pool_bd, pool_scale, w_out, g_post, mod, g_pre_f, g_post_f, w_gate, w_up, w_down,
          proj, tm, chunk):
    b, n, d = x.shape
    pw, aw = a.shape[2], o_win.shape[2]
    hb = tm // POOL_HALO
    tpb = n // tm
    last = b * tpb - 1
    with_proj = proj is not None
    cur = lambda s: (jnp.minimum(s, last) // tpb, jnp.minimum(s, last) % tpb)
    prev = lambda s: (jnp.maximum(s - 1, 0) // tpb, jnp.maximum(s - 1, 0) % tpb)
    row = lambda s: (0, 0)
    tile = lambda s: (cur(s)[0], cur(s)[1], 0)
    in_specs = [
        pl.BlockSpec((None, tm, d), tile),
        pl.BlockSpec((None, tm, pw), tile),
        pl.BlockSpec((None, POOL_HALO, pw), lambda s: (cur(s)[0], jnp.maximum(cur(s)[1] * hb - 1, 0), 0)),
        pl.BlockSpec((None, POOL_HALO, pw), lambda s: (cur(s)[0], jnp.minimum((cur(s)[1] + 1) * hb, n // POOL_HALO - 1), 0)),
        pl.BlockSpec((tm, pw), lambda s: (cur(s)[1], 0)),
        pl.BlockSpec((None, tm, aw), tile),
        pl.BlockSpec((None, tm, aw), tile),
        _resident((pw, pw), row),
        pl.BlockSpec((1, pw), row),
        _layer_spec(w_out),
        pl.BlockSpec((1, d), row),
        pl.BlockSpec((None, N_MOD, d), lambda s: (cur(s)[0], 0, 0)),
        pl.BlockSpec((1, d), row),
        pl.BlockSpec((1, d), row),
        _layer_spec(w_gate),
        _layer_spec(w_up),
        _layer_spec(w_down),
    ]
    args = [x, a, a, a, inv_count, o_win, o_glb, w_pool_bd, pool_scale, w_out[0], g_post, mod,
            g_pre_f, g_post_f, w_gate[0], w_up[0], w_down[0]]
    out_shape = (jax.ShapeDtypeStruct((b, n, d), F32),)
    out_specs = (pl.BlockSpec((None, tm, d), tile),)
    scratch = [pltpu.VMEM((tm + 2 * POOL_HALO + POOL_HALO // 2, pw), F32),
               pltpu.VMEM((tm + 2 * POOL_HALO, pw), F32), pltpu.VMEM((tm + 2 * POOL_HALO, pw), F32),
               pltpu.VMEM((tm + 2 * POOL_HALO, LANES), F32)]
    if with_proj:
        mod_i, g_pre_i, w_in, cos, sin, gq, gk, qfill = proj
        nw = w_in[0].shape[2]
        in_specs += [
            pl.BlockSpec((None, N_MOD, d), lambda s: (cur(s)[0], 0, 0)),
            pl.BlockSpec((1, d), row),
            _layer_spec(w_in),
            pl.BlockSpec((tm, LANES), lambda s: (prev(s)[1], 0)),
            pl.BlockSpec((tm, LANES), lambda s: (prev(s)[1], 0)),
            pl.BlockSpec((1, LANES), row),
            pl.BlockSpec((1, LANES), row),
            pl.BlockSpec((1, LANES), row),
        ]
        args += [mod_i, g_pre_i, w_in[0], cos, sin, gq, gk, qfill]
        proj_shape, proj_specs = _in_proj_outputs(b, n, tm, prev)
        out_shape += proj_shape
        out_specs += proj_specs
        scratch = [pltpu.VMEM((tm, nw), F32)] + scratch
    outs = pl.pallas_call(
        functools.partial(_tail_kernel, tpb=tpb, chunk=chunk, st=min(tm, 512), with_proj=with_proj),
        out_shape=out_shape,
        grid=(b * tpb + (1 if with_proj else 0),),
        in_specs=in_specs,
        out_specs=out_specs,
        scratch_shapes=scratch,
        compiler_params=_cparams("arbitrary"),
        name="tail_proj" if with_proj else "tail",
    )(*args)
    return outs[0], outs[1:]


def _rope_tables(n):
    rows = n // GRID_W
    row = jnp.broadcast_to(jnp.arange(rows)[:, None], (rows, GRID_W)).reshape(-1).astype(F32)
    col = jnp.broadcast_to(jnp.arange(GRID_W)[None, :], (rows, GRID_W)).reshape(-1).astype(F32)
    freq = ROPE_THETA ** (-jnp.arange(ROPE_FREQS, dtype=F32) / ROPE_FREQS)
    ar, ac = row[:, None] * freq, col[:, None] * freq
    cos = jnp.concatenate([jnp.cos(ar), jnp.cos(ar), jnp.cos(ac), jnp.cos(ac)], axis=1)
    sin = jnp.concatenate([-jnp.sin(ar), jnp.sin(ar), -jnp.sin(ac), jnp.sin(ac)], axis=1)
    return jnp.tile(cos, (1, LANES // HEAD_DIM)), jnp.tile(sin, (1, LANES // HEAD_DIM))


def _shift_row(m):
    pieces, rest = [], m.astype(F32)
    for _ in range(N_SHIFT):
        part = rest.astype(BF16).astype(F32)
        pieces.append(-part)
        rest = rest - part
    return jnp.zeros((1, LANES), F32).at[0, HEAD_DIM:HEAD_DIM + N_SHIFT].set(jnp.stack(pieces))


def _pool_inv_counts(n):
    t = jnp.arange(n)
    cols = []
    for w in POOL_WINDOWS:
        hi = jnp.clip(t + (w - 1 - w // 2), 0, n - 1)
        lo = jnp.clip(t - w // 2, 0, n - 1)
        cols.append(jnp.broadcast_to((1.0 / (hi - lo + 1).astype(F32))[:, None], (n, HEAD_DIM)))
    return jnp.concatenate(cols, axis=1)


def _tile(n, pref):
    t = min(n, pref)
    assert n % t == 0, (n, t)
    return t


def kernel(x, c, ctx, c_ctx, w_mod, b_mod, g_pre_mix, g_post_mix, g_pre_ffn, g_post_ffn, w_in, w_pool, pool_scale, win_sink, g_qnorm, g_knorm, w_out, w_gate, w_up, w_down):
    b, n, d = x.shape
    lc = ctx.shape[1]
    depth = w_mod.shape[0]
    assert w_in.shape[2] == IN_WIDTH and KV_WIDTH == LANES and n % GRID_W == 0

    cos, sin = _rope_tables(n)
    cos_c, sin_c = jnp.ones((lc, LANES), F32), jnp.zeros((lc, LANES), F32)

    mod_rows = -(-(b + 1) // 8) * 8
    cc = jnp.concatenate([c, c_ctx[None, :], jnp.zeros((mod_rows - b - 1, d), F32)], axis=0)
    mod_all = _modulation(cc, w_mod, b_mod)

    tm = _tile(n, 512)
    tq_win = _tile(n, 1024)
    tq_glb = _tile(n, 512)
    tk_glb = n // (N_KV_HEADS * (tq_glb // WINDOW))
    assert tk_glb % LANES == 0
    rb_glb = 512
    ffn_chunk = 256

    def layer(l):
        bound = ATTN_SCALE * HEAD_DIM * jnp.max(jnp.abs(g_qnorm[l])) * jnp.max(jnp.abs(g_knorm[l]))
        return dict(
            m=mod_all[l, :b].reshape(b, N_MOD, d),
            mc=jnp.broadcast_to(mod_all[l, b].reshape(1, N_MOD, d), (b, N_MOD, d)),
            w_in=(w_in_b, l),
            gq=jnp.tile(g_qnorm[l], LANES // HEAD_DIM)[None, :],
            gk=jnp.tile(g_knorm[l], LANES // HEAD_DIM)[None, :],
            g_pre=g_pre_mix[l][None, :],
            bound=bound,
            qfill=jnp.where(bound <= SAFE_SHIFT, _shift_row(bound), 0.0),
        )

    w_in_b, w_out_b, w_gate_b, w_up_b, w_down_b = (w.astype(BF16) for w in (w_in, w_out, w_gate, w_up, w_down))
    no_fill = jnp.zeros((1, LANES), F32)
    icnt, icnt_c = _pool_inv_counts(n), _pool_inv_counts(lc)
    xc = ctx
    p = layer(0)
    proj = _in_proj(x, p["m"], p["g_pre"], p["w_in"], cos, sin, p["gq"], p["gk"], p["qfill"], tm, carried=True)
    for l in range(depth):
        last = l == depth - 1
        m, mc, gq, gk = p["m"], p["mc"], p["gq"], p["gk"]
        w_out_l, w_gate_l, w_up_l, w_down_l = (w_out_b, l), (w_gate_b, l), (w_up_b, l), (w_down_b, l)
        w_pool_bd = jax.scipy.linalg.block_diag(*[w_pool[l, g] for g in range(len(POOL_WINDOWS))]).astype(BF16)
        pscale = pool_scale[l][None, :]
        sink = jnp.zeros((8, LANES), F32).at[:N_Q_HEADS].set(jnp.broadcast_to(win_sink[l][:, None], (N_Q_HEADS, LANES)))
        g_post = g_post_mix[l][None, :]
        g_pre_f, g_post_f = g_pre_ffn[l][None, :], g_post_ffn[l][None, :]

        a, qw, ktw, vw, qg, ktg, vg = proj
        ac, qwc, ktwc, vwc, qgc, ktgc, vgc = _in_proj(xc, mc, p["g_pre"], p["w_in"], cos_c, sin_c, gq, gk, no_fill, lc)

        def attn_bounded(*ops):
            o_g, o_w = _attn(*ops, tq=tq_glb, tk=tk_glb, rb=rb_glb)
            return o_w, o_g

        def attn_exact(qg, ktg, vg, ktgc, vgc, qw, ktw, vw, ktwc, vwc, sink):
            return (_window_attn(qw, ktw, vw, ktwc, vwc, sink, tq_win),
                    _global_attn(qg, ktg, vg, ktgc, vgc, tq=tq_glb, tk=tk_glb, rb=rb_glb))

        o_win, o_glb = lax.cond(p["bound"] <= SAFE_SHIFT, attn_bounded, attn_exact,
                                qg, ktg, vg, ktgc, vgc, qw, ktw, vw, ktwc, vwc, sink)
        tail_args = (x, a, icnt, o_win, o_glb, w_pool_bd, pscale, w_out_l, g_post, m, g_pre_f, g_post_f,
                     w_gate_l, w_up_l, w_down_l)
        if last:
            x, _ = _tail(*tail_args, None, tm, ffn_chunk)
        else:
            oc_win = _ctx_attn(qwc, ktwc, vwc, sink)
            oc_glb = _ctx_attn(qgc, ktgc, vgc, None)
            xc = _out_proj(xc, ac, icnt_c, oc_win, oc_glb, w_pool_bd, pscale, w_out_l, g_post, mc, lc)
            xc = _ffn(xc, mc, g_pre_f, g_post_f, w_gate_l, w_up_l, w_down_l, lc, ffn_chunk)
            p = layer(l + 1)
            x, proj = _tail(*tail_args, (p["m"], p["g_pre"], p["w_in"], cos, sin, p["gq"], p["gk"], p["qfill"]),
                            tm, ffn_chunk)
    return x
```

```python
import functools

import jax
import jax.numpy as jnp
from jax import lax
from jax.experimental import pallas as pl
from jax.experimental.pallas import tpu as pltpu

F32 = jnp.float32
BF16 = jnp.bfloat16

HEAD_DIM = 64
GRID_W = 64
ROPE_FREQS = HEAD_DIM // 4
ROPE_THETA = 10000.0
NORM_EPS = 1e-6
NEG_INF = -1e30
ATTN_SCALE = HEAD_DIM ** -0.5
WINDOW = 128
POOL_WINDOWS = (2, 4, 8, 16)
POOL_HALO = 16
N_Q_HEADS = 6
N_KV_HEADS = 2
KV_GROUP = N_Q_HEADS // N_KV_HEADS
N_MOD = 6
POOL_WIDTH = len(POOL_WINDOWS) * HEAD_DIM
Q_WIDTH = N_Q_HEADS * HEAD_DIM
KV_WIDTH = N_KV_HEADS * HEAD_DIM
COL_QW = POOL_WIDTH
COL_KW = COL_QW + Q_WIDTH
COL_VW = COL_KW + KV_WIDTH
COL_QG = COL_VW + KV_WIDTH
COL_KG = COL_QG + Q_WIDTH
COL_VG = COL_KG + KV_WIDTH
IN_WIDTH = COL_VG + KV_WIDTH
N_SHIFT = 3
SAFE_SHIFT = 30.0

LANES = 128
VMEM_LIMIT = 56 * 1024 * 1024


def _cparams(*sem):
    return pltpu.CompilerParams(dimension_semantics=sem, vmem_limit_bytes=VMEM_LIMIT)


def _resident(block_shape, index_map):
    return pl.BlockSpec(block_shape, index_map, pipeline_mode=pl.Buffered(1))


def _layer_spec(wl):
    w, l = wl
    return pl.BlockSpec((None,) + w.shape[1:], lambda *g: (l,) + (0,) * (w.ndim - 1), pipeline_mode=pl.Buffered(1))


def _rms(x, g):
    ms = jnp.mean(x * x, axis=-1, keepdims=True)
    return (x * lax.rsqrt(ms + NORM_EPS)) * g


def _mod_kernel(c_ref, w_ref, b_ref, o_ref):
    c = c_ref[...]
    s = c * jax.nn.sigmoid(c)
    o_ref[...] = jnp.dot(s.astype(BF16), w_ref[...].astype(BF16), preferred_element_type=F32) + b_ref[...]


def _modulation(cc, w_mod, b_mod):
    depth, d, dm = w_mod.shape
    rows = cc.shape[0]
    tn = dm // 4
    return pl.pallas_call(
        _mod_kernel,
        out_shape=jax.ShapeDtypeStruct((depth, rows, dm), F32),
        grid=(depth, dm // tn),
        in_specs=[
            pl.BlockSpec((rows, d), lambda l, j: (0, 0)),
            pl.BlockSpec((None, d, tn), lambda l, j: (l, 0, j)),
            pl.BlockSpec((None, 1, tn), lambda l, j: (l, 0, j)),
        ],
        out_specs=pl.BlockSpec((None, rows, tn), lambda l, j: (l, 0, j)),
        compiler_params=_cparams("arbitrary", "arbitrary"),
        name="modulation",
    )(cc, w_mod, b_mod.reshape(depth, 1, dm))


def _in_proj_epilogue(u_of, n_rows, st, cos_ref, sin_ref, gq_ref, gk_ref, qfill_ref,
                      a_ref, qw_ref, ktw_ref, vw_ref, qg_ref, ktg_ref, vg_ref):
    lane = lax.broadcasted_iota(jnp.int32, (st, LANES), 1)
    lo = lane < HEAD_DIM
    first = (lane & ROPE_FREQS) == 0
    shift_rows = (lax.broadcasted_iota(jnp.int32, (HEAD_DIM, LANES), 0) < N_SHIFT).astype(BF16)
    sum_lane = (lane == HEAD_DIM).astype(F32)
    gq, gk = gq_ref[...], gk_ref[...]
    qfill = qfill_ref[...]

    def head_norm(t, g):
        sq = t * t
        s_lo = jnp.sum(jnp.where(lo, sq, 0.0), axis=-1, keepdims=True)
        s_hi = jnp.sum(jnp.where(lo, 0.0, sq), axis=-1, keepdims=True)
        ms = jnp.where(lo, s_lo, s_hi) * (1.0 / HEAD_DIM)
        return (t * lax.rsqrt(ms + NORM_EPS)) * g

    for sub in range(n_rows // st):
        rs = slice(sub * st, (sub + 1) * st)
        u = u_of(sub)
        cos = cos_ref[rs, :]
        sin = sin_ref[rs, :]

        def rope(t):
            partner = jnp.where(first, pltpu.roll(t, LANES - ROPE_FREQS, 1), pltpu.roll(t, ROPE_FREQS, 1))
            return t * cos + partner * sin

        def store_branch(q_chunks, k, v, fill, q_ref, kt_ref, v_ref):
            for hd in range(N_Q_HEADS):
                t = q_chunks[hd // 2]
                if hd % 2:
                    t = pltpu.roll(t, HEAD_DIM, 1)
                q_ref[hd, rs, :] = jnp.where(lo, t, fill).astype(BF16)
            for r in range(st // LANES):
                kb = sub * (st // LANES) + r
                kt = jnp.transpose(k[r * LANES:(r + 1) * LANES, :]).astype(BF16)
                for kv in range(N_KV_HEADS):
                    kt_ref[kv, kb, 0:HEAD_DIM, :] = kt[kv * HEAD_DIM:(kv + 1) * HEAD_DIM, :]
                    kt_ref[kv, kb, HEAD_DIM:LANES, :] = shift_rows
            v_ref[0, rs, :] = jnp.where(lo, v, sum_lane).astype(BF16)
            v_ref[1, rs, :] = jnp.where(lo, pltpu.roll(v, HEAD_DIM, 1), sum_lane).astype(BF16)

        def cols(start, width):
            return u[:, start:start + width]

        a_ref[rs, :] = cols(0, POOL_WIDTH)
        store_branch([rope(cols(COL_QW + LANES * c, LANES)) * ATTN_SCALE for c in range(Q_WIDTH // LANES)],
                     rope(cols(COL_KW, KV_WIDTH)), cols(COL_VW, KV_WIDTH), 0.0, qw_ref, ktw_ref, vw_ref)
        store_branch([rope(head_norm(cols(COL_QG + LANES * c, LANES), gq)) * ATTN_SCALE for c in range(Q_WIDTH // LANES)],
                     rope(head_norm(cols(COL_KG, KV_WIDTH), gk)), cols(COL_VG, KV_WIDTH), qfill, qg_ref, ktg_ref, vg_ref)


def _project(x, mod_ref, g_ref, w_ref):
    h = _rms(x, g_ref[...])
    h = h * (1.0 + mod_ref[1:2, :]) + mod_ref[0:1, :]
    return jnp.dot(h.astype(BF16), w_ref[...], preferred_element_type=F32)


def _in_proj_kernel(x_ref, mod_ref, g_ref, w_ref, cos_ref, sin_ref, gq_ref, gk_ref, qfill_ref, *out_refs, st):
    tm = x_ref.shape[0]
    us = [_project(x_ref[sub * st:(sub + 1) * st, :], mod_ref, g_ref, w_ref) for sub in range(tm // st)]
    _in_proj_epilogue(lambda sub: us[sub], tm, st, cos_ref, sin_ref, gq_ref, gk_ref, qfill_ref, *out_refs)


def _in_proj_outputs(b, n, tm, tile_of):
    qshape = jax.ShapeDtypeStruct((b, N_Q_HEADS, n, LANES), BF16)
    ktshape = jax.ShapeDtypeStruct((b, N_KV_HEADS, n // LANES, LANES, LANES), BF16)
    vshape = jax.ShapeDtypeStruct((b, N_KV_HEADS, n, LANES), BF16)

    def spec(block, where):
        return pl.BlockSpec(block, lambda *g: where(*tile_of(*g)))

    aspec = spec((None, tm, POOL_WIDTH), lambda bi, i: (bi, i, 0))
    qspec = spec((None, N_Q_HEADS, tm, LANES), lambda bi, i: (bi, 0, i, 0))
    ktspec = spec((None, N_KV_HEADS, tm // LANES, LANES, LANES), lambda bi, i: (bi, 0, i, 0, 0))
    vspec = spec((None, N_KV_HEADS, tm, LANES), lambda bi, i: (bi, 0, i, 0))
    shapes = (jax.ShapeDtypeStruct((b, n, POOL_WIDTH), F32), qshape, ktshape, vshape, qshape, ktshape, vshape)
    return shapes, (aspec, qspec, ktspec, vspec, qspec, ktspec, vspec)


def _in_proj_carried_kernel(x_ref, mod_ref, g_ref, w_ref, cos_ref, sin_ref, gq_ref, gk_ref, qfill_ref, *refs, st):
    out_refs, u_ref = refs[:-1], refs[-1]

    @pl.when(pl.program_id(0) == 0)
    def _():
        u_ref[...] = jnp.zeros(u_ref.shape, F32)

    _in_proj_epilogue(lambda sub: u_ref[sub * st:(sub + 1) * st, :], x_ref.shape[0], st, cos_ref, sin_ref,
                      gq_ref, gk_ref, qfill_ref, *out_refs)
    u_ref[...] = _project(x_ref[...], mod_ref, g_ref, w_ref)


def _in_proj(x, mod, g_pre, w_in, cos, sin, gq, gk, qfill, tm, carried=False):
    b, n, d = x.shape
    nw = w_in[0].shape[2]
    tpb = n // tm
    last = b * tpb - 1
    if carried:
        grid = (b * tpb + 1,)
        cur = lambda s: (jnp.minimum(s, last) // tpb, jnp.minimum(s, last) % tpb)
        prev = lambda s: (jnp.maximum(s - 1, 0) // tpb, jnp.maximum(s - 1, 0) % tpb)
        kern, scratch, sem = _in_proj_carried_kernel, [pltpu.VMEM((tm, nw), F32)], ("arbitrary",)
    else:
        grid = (b, tpb)
        cur = prev = lambda bi, i: (bi, i)
        kern, scratch, sem = _in_proj_kernel, [], ("arbitrary", "arbitrary")
    out_shape, out_specs = _in_proj_outputs(b, n, tm, prev)
    row = lambda *g: (0, 0)
    return pl.pallas_call(
        functools.partial(kern, st=min(tm, 512)),
        out_shape=out_shape,
        grid=grid,
        in_specs=[
            pl.BlockSpec((None, tm, d), lambda *g: (cur(*g)[0], cur(*g)[1], 0)),
            pl.BlockSpec((None, N_MOD, d), lambda *g: (cur(*g)[0], 0, 0)),
            pl.BlockSpec((1, d), row),
            _layer_spec(w_in),
            pl.BlockSpec((tm, LANES), lambda *g: (prev(*g)[1], 0)),
            pl.BlockSpec((tm, LANES), lambda *g: (prev(*g)[1], 0)),
            pl.BlockSpec((1, LANES), row),
            pl.BlockSpec((1, LANES), row),
            pl.BlockSpec((1, LANES), row),
        ],
        out_specs=out_specs,
        scratch_shapes=scratch,
        compiler_params=_cparams(*sem),
        name="in_proj",
    )(x, mod, g_pre, w_in[0], cos, sin, gq, gk, qfill)


def _merge_heads(outs, lo):
    return [jnp.where(lo, outs[2 * c], pltpu.roll(outs[2 * c + 1], HEAD_DIM, 1)) for c in range(N_Q_HEADS // 2)]


def _key_tile(kt_ref, first_block, n_blocks):
    return jnp.concatenate([kt_ref[first_block + j] for j in range(n_blocks)], axis=1)


def _stacked_sink(sink_ref, kv, rows_per_head):
    return jnp.concatenate([jnp.broadcast_to(sink_ref[hd:hd + 1, 0:1], (rows_per_head, 1))
                            for hd in range(kv * KV_GROUP, (kv + 1) * KV_GROUP)], axis=0)


def _window_units(q_ref, kt_ref, v_ref, ktc_ref, vc_ref, sink_ref, first_blk):
    nkb = kt_ref.shape[1]
    sub = WINDOW
    rows = KV_GROUP * sub
    r_in = lax.broadcasted_iota(jnp.int32, (rows, LANES), 0) & (sub - 1)
    c_in = lax.broadcasted_iota(jnp.int32, (rows, LANES), 1)
    ctx = [(_key_tile(ktc_ref.at[kv], 0, ktc_ref.shape[1]), vc_ref[kv], _stacked_sink(sink_ref, kv, sub))
           for kv in range(N_KV_HEADS)]

    def blocks(j):
        blk = first_blk + j
        return blk, jnp.maximum(blk - 1, 0), jnp.minimum(blk + 1, nkb - 1)

    def logits(j, kv):
        blk, prev_blk, next_blk = blocks(j)
        q = q_ref[kv * KV_GROUP:(kv + 1) * KV_GROUP, pl.ds(pl.multiple_of(j * sub, sub), sub), :].reshape(rows, LANES)
        kt = jnp.concatenate([kt_ref[kv, prev_blk], kt_ref[kv, blk], kt_ref[kv, next_blk]], axis=1)
        s = jnp.dot(q, kt, preferred_element_type=F32)
        sc = jnp.dot(q, ctx[kv][0], preferred_element_type=F32)
        return s, sc

    def attend(j, kv, s, sc):
        blk, prev_blk, next_blk = blocks(j)
        _, vc, sk = ctx[kv]
        v = jnp.concatenate([v_ref[kv, pl.ds(pl.multiple_of(kb * sub, sub), sub), :]
                             for kb in (prev_blk, blk, next_blk)], axis=0)
        s_prev = jnp.where(jnp.logical_and(c_in >= r_in, blk > 0), s[:, 0:sub], NEG_INF)
        s_own = s[:, sub:2 * sub]
        s_next = jnp.where(jnp.logical_and(c_in <= r_in, blk < nkb - 1), s[:, 2 * sub:3 * sub], NEG_INF)
        mx = jnp.maximum(jnp.maximum(s_prev, s_own), s_next)
        for t in range(sc.shape[1] // LANES):
            mx = jnp.maximum(mx, sc[:, t * LANES:(t + 1) * LANES])
        m = jnp.maximum(jnp.max(mx, axis=-1, keepdims=True), sk)
        p = jnp.concatenate([jnp.exp(s_prev - m), jnp.exp(s_own - m), jnp.exp(s_next - m)], axis=1)
        pv = jnp.dot(p.astype(BF16), v, preferred_element_type=F32)
        pv = pv + jnp.dot(jnp.exp(sc - m).astype(BF16), vc, preferred_element_type=F32)
        l = pv[:, HEAD_DIM:HEAD_DIM + 1] + jnp.exp(sk - m)
        o = pv / l
        return [o[g * sub:(g + 1) * sub] for g in range(KV_GROUP)]

    return logits, attend


def _store_heads(o_ref, row0, outs):
    nrows = outs[0].shape[0]
    lo = lax.broadcasted_iota(jnp.int32, (nrows, LANES), 1) < HEAD_DIM
    for c, chunk in enumerate(_merge_heads(outs, lo)):
        o_ref[pl.ds(row0, nrows), c * LANES:(c + 1) * LANES] = chunk.astype(BF16)


def _window_attn_kernel(q_ref, kt_ref, v_ref, ktc_ref, vc_ref, sink_ref, o_ref):
    tq = q_ref.shape[1]
    sub = WINDOW
    logits, attend = _window_units(q_ref, kt_ref, v_ref, ktc_ref, vc_ref, sink_ref, pl.program_id(1) * (tq // sub))
    units = [(j, kv) for j in range(tq // sub) for kv in range(N_KV_HEADS)]
    pending = logits(*units[0])
    outs = []
    for idx, (j, kv) in enumerate(units):
        ahead = logits(*units[idx + 1]) if idx + 1 < len(units) else None
        outs += attend(j, kv, *pending)
        pending = ahead
        if kv == N_KV_HEADS - 1:
            _store_heads(o_ref, j * sub, outs)
            outs = []


def _window_attn(q, kt, v, ktc, vc, sink, tq):
    b, _, n, _ = q.shape
    lc = vc.shape[2]
    assert tq % WINDOW == 0 and WINDOW == LANES
    batch4 = lambda bi, i: (bi, 0, 0, 0)
    batch5 = lambda bi, i: (bi, 0, 0, 0, 0)
    return pl.pallas_call(
        _window_attn_kernel,
        out_shape=jax.ShapeDtypeStruct((b, n, N_Q_HEADS * HEAD_DIM), BF16),
        grid=(b, n // tq),
        in_specs=[
            pl.BlockSpec((None, N_Q_HEADS, tq, LANES), lambda bi, i: (bi, 0, i, 0)),
            pl.BlockSpec((None, N_KV_HEADS, n // LANES, LANES, LANES), batch5),
            pl.BlockSpec((None, N_KV_HEADS, n, LANES), batch4),
            pl.BlockSpec((None, N_KV_HEADS, lc // LANES, LANES, LANES), batch5),
            pl.BlockSpec((None, N_KV_HEADS, lc, LANES), batch4),
            pl.BlockSpec(sink.shape, lambda bi, i: (0, 0)),
        ],
        out_specs=pl.BlockSpec((None, tq, N_Q_HEADS * HEAD_DIM), lambda bi, i: (bi, i, 0)),
        compiler_params=_cparams("arbitrary", "arbitrary"),
        name="window_attn",
    )(q, kt, v, ktc, vc, sink)


def _ctx_attn_kernel(*refs, has_sink):
    if has_sink:
        q_ref, kt_ref, v_ref, sink_ref, o_ref = refs
    else:
        q_ref, kt_ref, v_ref, o_ref = refs
    tq = q_ref.shape[1]
    rows = KV_GROUP * tq
    outs = []
    for kv in range(N_KV_HEADS):
        q = q_ref[kv * KV_GROUP:(kv + 1) * KV_GROUP].reshape(rows, LANES)
        s = jnp.dot(q, _key_tile(kt_ref.at[kv], 0, kt_ref.shape[1]), preferred_element_type=F32)
        m = jnp.max(s, axis=-1, keepdims=True)
        if has_sink:
            sk = _stacked_sink(sink_ref, kv, tq)
            m = jnp.maximum(m, sk)
        pv = jnp.dot(jnp.exp(s - m).astype(BF16), v_ref[kv], preferred_element_type=F32)
        l = pv[:, HEAD_DIM:HEAD_DIM + 1]
        if has_sink:
            l = l + jnp.exp(sk - m)
        o = pv / l
        outs += [o[g * tq:(g + 1) * tq] for g in range(KV_GROUP)]
    _store_heads(o_ref, 0, outs)


def _ctx_attn(q, kt, v, sink):
    b, _, n, _ = q.shape
    args = [q, kt, v]
    in_specs = [pl.BlockSpec((None,) + a.shape[1:], lambda bi, nd=a.ndim: (bi,) + (0,) * (nd - 1)) for a in args]
    if sink is not None:
        args.append(sink)
        in_specs.append(pl.BlockSpec(sink.shape, lambda bi: (0, 0)))
    return pl.pallas_call(
        functools.partial(_ctx_attn_kernel, has_sink=sink is not None),
        out_shape=jax.ShapeDtypeStruct((b, n, N_Q_HEADS * HEAD_DIM), BF16),
        grid=(b,),
        in_specs=in_specs,
        out_specs=pl.BlockSpec((None, n, N_Q_HEADS * HEAD_DIM), lambda bi: (bi, 0, 0)),
        compiler_params=_cparams("arbitrary"),
        name="ctx_attn",
    )(*args)


def _global_attn_kernel(q_ref, kt_ref, v_ref, ktc_ref, vc_ref, o_ref, qs_ref, acc_ref, mx_ref, *, tk, rb):
    tq = q_ref.shape[1]
    rows = KV_GROUP * tq
    kb_per_step = tk // LANES
    n_steps = kt_ref.shape[1] // kb_per_step
    lane = lax.broadcasted_iota(jnp.int32, (rows, LANES), 1)

    def key_steps(fn):
        def step(kc, carry):
            for kv in range(N_KV_HEADS):
                fn(kv, _key_tile(kt_ref.at[kv], kc * kb_per_step, kb_per_step),
                   v_ref[kv, pl.ds(pl.multiple_of(kc * tk, tk), tk), :])
            return carry
        lax.fori_loop(0, n_steps, step, 0)
        for kv in range(N_KV_HEADS):
            fn(kv, _key_tile(ktc_ref.at[kv], 0, ktc_ref.shape[1]), vc_ref[kv])

    def q_rows(kv):
        return q_ref[kv * KV_GROUP:(kv + 1) * KV_GROUP].reshape(rows, LANES)

    mx_ref[...] = jnp.full(mx_ref.shape, NEG_INF, F32)

    def track_max(kv, kt, v):
        s = jnp.dot(q_rows(kv), kt, preferred_element_type=F32)
        mx = mx_ref[kv]
        for j in range(kt.shape[1] // LANES):
            mx = jnp.maximum(mx, s[:, j * LANES:(j + 1) * LANES])
        mx_ref[kv] = mx

    key_steps(track_max)

    for kv in range(N_KV_HEADS):
        qs = q_rows(kv).astype(F32)
        rest = jnp.max(mx_ref[kv], axis=-1, keepdims=True)
        for piece in range(N_SHIFT):
            part = rest.astype(BF16).astype(F32)
            qs = jnp.where(lane == HEAD_DIM + piece, -part, qs)
            rest = rest - part
        qs_ref[kv] = qs.astype(BF16)

    acc_ref[...] = jnp.zeros(acc_ref.shape, F32)

    def accumulate(kv, kt, v):
        for r in range(rows // rb):
            rsl = slice(r * rb, (r + 1) * rb)
            s = jnp.dot(qs_ref[kv, rsl, :], kt, preferred_element_type=F32)
            p = jnp.exp(s).astype(BF16)
            acc_ref[kv, rsl, :] += jnp.dot(p, v, preferred_element_type=F32)

    key_steps(accumulate)

    outs = []
    for hd in range(N_Q_HEADS):
        a = acc_ref[hd // KV_GROUP, (hd % KV_GROUP) * tq:(hd % KV_GROUP + 1) * tq, :]
        outs.append(a / a[:, HEAD_DIM:HEAD_DIM + 1])
    _store_heads(o_ref, 0, outs)


def _global_attn(q, kt, v, ktc, vc, *, tq, tk, rb):
    b, _, n, _ = q.shape
    lc = vc.shape[2]
    rows = KV_GROUP * tq
    batch4 = lambda bi, i: (bi, 0, 0, 0)
    batch5 = lambda bi, i: (bi, 0, 0, 0, 0)
    return pl.pallas_call(
        functools.partial(_global_attn_kernel, tk=tk, rb=rb),
        out_shape=jax.ShapeDtypeStruct((b, n, N_Q_HEADS * HEAD_DIM), BF16),
        grid=(b, n // tq),
        in_specs=[
            pl.BlockSpec((None, N_Q_HEADS, tq, LANES), lambda bi, i: (bi, 0, i, 0)),
            pl.BlockSpec((None, N_KV_HEADS, n // LANES, LANES, LANES), batch5),
            pl.BlockSpec((None, N_KV_HEADS, n, LANES), batch4),
            pl.BlockSpec((None, N_KV_HEADS, lc // LANES, LANES, LANES), batch5),
            pl.BlockSpec((None, N_KV_HEADS, lc, LANES), batch4),
        ],
        out_specs=pl.BlockSpec((None, tq, N_Q_HEADS * HEAD_DIM), lambda bi, i: (bi, i, 0)),
        scratch_shapes=[pltpu.VMEM((N_KV_HEADS, rows, LANES), BF16), pltpu.VMEM((N_KV_HEADS, rows, LANES), F32),
                        pltpu.VMEM((N_KV_HEADS, rows, LANES), F32)],
        compiler_params=_cparams("arbitrary", "arbitrary"),
        name="global_attn_exact_max",
    )(q, kt, v, ktc, vc)


def _attn_kernel(qg_ref, ktg_ref, vg_ref, ktgc_ref, vgc_ref, qw_ref, ktw_ref, vw_ref, ktwc_ref, vwc_ref, sink_ref,
                 og_ref, ow_ref, acc_ref, *, tk, rb):
    tq = qg_ref.shape[1]
    rows = KV_GROUP * tq
    sub = WINDOW
    n_sub = tq // sub
    kb_per_step = tk // LANES
    steps_per_sub = ktg_ref.shape[1] // kb_per_step // n_sub
    assert steps_per_sub * n_sub * kb_per_step == ktg_ref.shape[1] and tq % rb == 0

    def global_keys(kv, kt, v, assign):
        for r in range(rows // rb):
            hd, off = kv * KV_GROUP + (r * rb) // tq, (r * rb) % tq
            s = jnp.dot(qg_ref[hd, off:off + rb, :], kt, preferred_element_type=F32)
            pv = jnp.dot(jnp.exp(s).astype(BF16), v, preferred_element_type=F32)
            if assign:
                acc_ref[kv, r * rb:(r + 1) * rb, :] = pv
            else:
                acc_ref[kv, r * rb:(r + 1) * rb, :] += pv

    def global_step(kc):
        for kv in range(N_KV_HEADS):
            global_keys(kv, _key_tile(ktg_ref.at[kv], kc * kb_per_step, kb_per_step),
                        vg_ref[kv, pl.ds(pl.multiple_of(kc * tk, tk), tk), :], False)

    for kv in range(N_KV_HEADS):
        global_keys(kv, _key_tile(ktgc_ref.at[kv], 0, ktgc_ref.shape[1]), vgc_ref[kv], True)

    logits, attend = _window_units(qw_ref, ktw_ref, vw_ref, ktwc_ref, vwc_ref, sink_ref, pl.program_id(1) * n_sub)

    def body(j, carry):
        pending = logits(j, 0)
        outs = []
        for kv in range(N_KV_HEADS):
            for t in range(steps_per_sub // N_KV_HEADS):
                global_step(j * steps_per_sub + kv * (steps_per_sub // N_KV_HEADS) + t)
            ahead = logits(j, kv + 1) if kv + 1 < N_KV_HEADS else None
            outs += attend(j, kv, *pending)
            pending = ahead
        _store_heads(ow_ref, pl.multiple_of(j * sub, sub), outs)
        return carry

    assert steps_per_sub % N_KV_HEADS == 0
    lax.fori_loop(0, n_sub, body, 0, unroll=True)

    outs = []
    for hd in range(N_Q_HEADS):
        a = acc_ref[hd // KV_GROUP, (hd % KV_GROUP) * tq:(hd % KV_GROUP + 1) * tq, :]
        outs.append(a / a[:, HEAD_DIM:HEAD_DIM + 1])
    _store_heads(og_ref, 0, outs)


def _attn(qg, ktg, vg, ktgc, vgc, qw, ktw, vw, ktwc, vwc, sink, *, tq, tk, rb):
    b, _, n, _ = qg.shape
    lc = vgc.shape[2]
    batch4 = lambda bi, i: (bi, 0, 0, 0)
    batch5 = lambda bi, i: (bi, 0, 0, 0, 0)
    branch_specs = [
        pl.BlockSpec((None, N_Q_HEADS, tq, LANES), lambda bi, i: (bi, 0, i, 0)),
        pl.BlockSpec((None, N_KV_HEADS, n // LANES, LANES, LANES), batch5),
        pl.BlockSpec((None, N_KV_HEADS, n, LANES), batch4),
        pl.BlockSpec((None, N_KV_HEADS, lc // LANES, LANES, LANES), batch5),
        pl.BlockSpec((None, N_KV_HEADS, lc, LANES), batch4),
    ]
    oshape = jax.ShapeDtypeStruct((b, n, N_Q_HEADS * HEAD_DIM), BF16)
    ospec = pl.BlockSpec((None, tq, N_Q_HEADS * HEAD_DIM), lambda bi, i: (bi, i, 0))
    return pl.pallas_call(
        functools.partial(_attn_kernel, tk=tk, rb=rb),
        out_shape=(oshape, oshape),
        grid=(b, n // tq),
        in_specs=branch_specs + branch_specs + [pl.BlockSpec(sink.shape, lambda bi, i: (0, 0))],
        out_specs=(ospec, ospec),
        scratch_shapes=[pltpu.VMEM((N_KV_HEADS, KV_GROUP * tq, LANES), F32)],
        compiler_params=_cparams("arbitrary", "arbitrary"),
        name="attn",
    )(qg, ktg, vg, ktgc, vgc, qw, ktw, vw, ktwc, vwc, sink)


def _out_proj_tile(has_prev, has_next, x_ref, a_ref, ap_ref, an_ref, icnt_ref, ow_ref, og_ref, wpool_ref, pscale_ref,
                   wo_ref, g_ref, mod_ref, ext_ref, s2_ref, s4_ref, s8_ref):
    tm = x_ref.shape[0]
    ext_ref[0:POOL_HALO, :] = jnp.where(has_prev, ap_ref[...], 0.0)
    ext_ref[POOL_HALO:POOL_HALO + tm, :] = a_ref[...]
    ext_ref[POOL_HALO + tm:2 * POOL_HALO + tm, :] = jnp.where(has_next, an_ref[...], 0.0)

    pw = a_ref.shape[1]
    aw = ow_ref.shape[1]
    y = jnp.dot(ow_ref[...], wo_ref[pw:pw + aw, :], preferred_element_type=F32)
    y = y + jnp.dot(og_ref[...], wo_ref[pw + aw:pw + 2 * aw, :], preferred_element_type=F32)

    lo = lax.broadcasted_iota(jnp.int32, (tm, LANES), 1) < HEAD_DIM

    pad = POOL_HALO // 2
    ext_ref[2 * POOL_HALO + tm:2 * POOL_HALO + tm + pad, :] = jnp.zeros((pad, ext_ref.shape[1]), F32)
    r2, r4, r8 = tm + 2 * POOL_HALO, tm + 2 * POOL_HALO - pad, tm + POOL_HALO
    s2_ref[0:r2, :] = ext_ref[0:r2, :] + ext_ref[1:r2 + 1, :]
    s4_ref[0:r4, :] = s2_ref[0:r4, :] + s2_ref[2:r4 + 2, :]
    s8_ref[0:r8, :] = s4_ref[0:r8, LANES:2 * LANES] + s4_ref[4:r8 + 4, LANES:2 * LANES]

    def window_sum(w, c):
        first = POOL_HALO - w // 2
        cols = slice(c * LANES, (c + 1) * LANES)
        if w == 2:
            return s2_ref[first:first + tm, cols]
        if w == 4:
            return s4_ref[first:first + tm, cols]
        if w == 8:
            return s8_ref[first:first + tm, :]
        return s8_ref[first:first + tm, :] + s8_ref[first + 8:first + 8 + tm, :]

    feats = []
    for c in range(2):
        w_a, w_b = POOL_WINDOWS[2 * c], POOL_WINDOWS[2 * c + 1]
        x0 = ext_ref[POOL_HALO:POOL_HALO + tm, c * LANES:(c + 1) * LANES]
        mean = jnp.where(lo, window_sum(w_a, c), window_sum(w_b, c)) * icnt_ref[:, c * LANES:(c + 1) * LANES]
        feats.append(mean - x0)
    feats = jnp.concatenate(feats, axis=1).astype(BF16)
    y_pool = jnp.dot(feats, wpool_ref[...], preferred_element_type=F32) * pscale_ref[...]

    y = y + jnp.dot(y_pool.astype(BF16), wo_ref[0:pw, :], preferred_element_type=F32)
    return x_ref[...] + mod_ref[2:3, :] * _rms(y, g_ref[...])


N_OUT_PROJ_IN = 12


def _out_proj_kernel(*refs):
    ins, o_ref, scratch = refs[:N_OUT_PROJ_IN], refs[N_OUT_PROJ_IN], refs[N_OUT_PROJ_IN + 1:]
    i = pl.program_id(1)
    o_ref[...] = _out_proj_tile(i > 0, i < pl.num_programs(1) - 1, *ins, *scratch)


def _out_proj(x, a, inv_count, o_win, o_glb, w_pool_bd, pool_scale, w_out, g_post, mod, tm):
    b, n, d = x.shape
    pw = a.shape[2]
    aw = o_win.shape[2]
    hb = tm // POOL_HALO
    row = lambda bi, i: (0, 0)
    tile = lambda bi, i: (bi, i, 0)
    return pl.pallas_call(
        _out_proj_kernel,
        out_shape=jax.ShapeDtypeStruct((b, n, d), F32),
        grid=(b, n // tm),
        in_specs=[
            pl.BlockSpec((None, tm, d), tile),
            pl.BlockSpec((None, tm, pw), tile),
            pl.BlockSpec((None, POOL_HALO, pw), lambda bi, i: (bi, jnp.maximum(i * hb - 1, 0), 0)),
            pl.BlockSpec((None, POOL_HALO, pw), lambda bi, i: (bi, jnp.minimum((i + 1) * hb, n // POOL_HALO - 1), 0)),
            pl.BlockSpec((tm, pw), lambda bi, i: (i, 0)),
            pl.BlockSpec((None, tm, aw), tile),
            pl.BlockSpec((None, tm, aw), tile),
            _resident((pw, pw), row),
            pl.BlockSpec((1, pw), row),
            _layer_spec(w_out),
            pl.BlockSpec((1, d), row),
            pl.BlockSpec((None, N_MOD, d), lambda bi, i: (bi, 0, 0)),
        ],
        out_specs=pl.BlockSpec((None, tm, d), tile),
        scratch_shapes=[pltpu.VMEM((tm + 2 * POOL_HALO + POOL_HALO // 2, pw), F32),
                        pltpu.VMEM((tm + 2 * POOL_HALO, pw), F32), pltpu.VMEM((tm + 2 * POOL_HALO, pw), F32),
                        pltpu.VMEM((tm + 2 * POOL_HALO, LANES), F32)],
        compiler_params=_cparams("arbitrary", "arbitrary"),
        name="out_proj",
    )(x, a, a, a, inv_count, o_win, o_glb, w_pool_bd, pool_scale, w_out[0], g_post, mod)


def _ffn_tile(x, mod_ref, gpre_ref, gpost_ref, wg_ref, wu_ref, wd_ref, chunk):
    h = _rms(x, gpre_ref[...])
    h = (h * (1.0 + mod_ref[4:5, :]) + mod_ref[3:4, :]).astype(BF16)
    f = None
    for c in range(wg_ref.shape[1] // chunk):
        cols = slice(c * chunk, (c + 1) * chunk)
        g = jnp.dot(h, wg_ref[:, cols], preferred_element_type=F32)
        u = jnp.dot(h, wu_ref[:, cols], preferred_element_type=F32)
        act = ((g * jax.nn.sigmoid(g)) * u).astype(BF16)
        part = jnp.dot(act, wd_ref[cols, :], preferred_element_type=F32)
        f = part if f is None else f + part
    return x + mod_ref[5:6, :] * _rms(f, gpost_ref[...])


def _ffn_kernel(x_ref, mod_ref, gpre_ref, gpost_ref, wg_ref, wu_ref, wd_ref, o_ref, *, chunk):
    o_ref[...] = _ffn_tile(x_ref[...], mod_ref, gpre_ref, gpost_ref, wg_ref, wu_ref, wd_ref, chunk)


def _ffn(x, mod, g_pre, g_post, w_gate, w_up, w_down, tm, chunk):
    b, n, d = x.shape
    row = lambda bi, i: (0, 0)
    tile = lambda bi, i: (bi, i, 0)
    return pl.pallas_call(
        functools.partial(_ffn_kernel, chunk=chunk),
        out_shape=jax.ShapeDtypeStruct((b, n, d), F32),
        grid=(b, n // tm),
        in_specs=[
            pl.BlockSpec((None, tm, d), tile),
            pl.BlockSpec((None, N_MOD, d), lambda bi, i: (bi, 0, 0)),
            pl.BlockSpec((1, d), row),
            pl.BlockSpec((1, d), row),
            _layer_spec(w_gate),
            _layer_spec(w_up),
            _layer_spec(w_down),
        ],
        out_specs=pl.BlockSpec((None, tm, d), tile),
        compiler_params=_cparams("arbitrary", "arbitrary"),
        name="ffn",
    )(x, mod, g_pre, g_post, w_gate[0], w_up[0], w_down[0])


N_FFN_IN = 5
N_PROJ_IN = 8
N_PROJ_OUT = 7


def _tail_kernel(*refs, tpb, chunk, st, with_proj):
    refs = list(refs)
    op_in, refs = refs[:N_OUT_PROJ_IN], refs[N_OUT_PROJ_IN:]
    ffn_in, refs = refs[:N_FFN_IN], refs[N_FFN_IN:]
    if with_proj:
        (modi_ref, gin_ref, win_ref, cos_ref, sin_ref, gq_ref, gk_ref, qfill_ref), refs = refs[:N_PROJ_IN], refs[N_PROJ_IN:]
    o_ref, refs = refs[0], refs[1:]
    if with_proj:
        proj_out, refs = refs[:N_PROJ_OUT], refs[N_PROJ_OUT:]
        u_ref, refs = refs[0], refs[1:]
    pool_scratch = refs
    tm = o_ref.shape[0]
    step = pl.program_id(0)
    i = lax.rem(jnp.minimum(step, pl.num_programs(0) - (2 if with_proj else 1)), tpb)

    if with_proj:
        @pl.when(step == 0)
        def _():
            u_ref[...] = jnp.zeros(u_ref.shape, F32)

        _in_proj_epilogue(lambda sub: u_ref[sub * st:(sub + 1) * st, :], tm, st, cos_ref, sin_ref, gq_ref, gk_ref,
                          qfill_ref, *proj_out)

    x_mid = _out_proj_tile(i > 0, i < tpb - 1, *op_in, *pool_scratch)
    y = _ffn_tile(x_mid, op_in[-1], *ffn_in, chunk)
    o_ref[...] = y
    if with_proj:
        u_ref[...] = _project(y, modi_ref, gin_ref, win_ref)


def _tail(x, a, inv_count, o_win, o_glb, w_pool_bd, pool_scale, w_out, g_post, mod, g_pre_f, g_post_f, w_gate, w_up, w_down,
          proj, tm, chunk):
    b, n, d = x.shape
    pw, aw = a.shape[2], o_win.shape[2]
    hb = tm // POOL_HALO
    tpb = n // tm
    last = b * tpb - 1
    with_proj = proj is not None
    cur = lambda s: (jnp.minimum(s, last) // tpb, jnp.minimum(s, last) % tpb)
    prev = lambda s: (jnp.maximum(s - 1, 0) // tpb, jnp.maximum(s - 1, 0) % tpb)
    row = lambda s: (0, 0)
    tile = lambda s: (cur(s)[0], cur(s)[1], 0)
    in_specs = [
        pl.BlockSpec((None, tm, d), tile),
        pl.BlockSpec((None, tm, pw), tile),
        pl.BlockSpec((None, POOL_HALO, pw), lambda s: (cur(s)[0], jnp.maximum(cur(s)[1] * hb - 1, 0), 0)),
        pl.BlockSpec((None, POOL_HALO, pw), lambda s: (cur(s)[0], jnp.minimum((cur(s)[1] + 1) * hb, n // POOL_HALO - 1), 0)),
        pl.BlockSpec((tm, pw), lambda s: (cur(s)[1], 0)),
        pl.BlockSpec((None, tm, aw), tile),
        pl.BlockSpec((None, tm, aw), tile),
        _resident((pw, pw), row),
        pl.BlockSpec((1, pw), row),
        _layer_spec(w_out),
        pl.BlockSpec((1, d), row),
        pl.BlockSpec((None, N_MOD, d), lambda s: (cur(s)[0], 0, 0)),
        pl.BlockSpec((1, d), row),
        pl.BlockSpec((1, d), row),
        _layer_spec(w_gate),
        _layer_spec(w_up),
        _layer_spec(w_down),
    ]
    args = [x, a, a, a, inv_count, o_win, o_glb, w_pool_bd, pool_scale, w_out[0], g_post, mod,
            g_pre_f, g_post_f, w_gate[0], w_up[0], w_down[0]]
    out_shape = (jax.ShapeDtypeStruct((b, n, d), F32),)
    out_specs = (pl.BlockSpec((None, tm, d), tile),)
    scratch = [pltpu.VMEM((tm + 2 * POOL_HALO + POOL_HALO // 2, pw), F32),
               pltpu.VMEM((tm + 2 * POOL_HALO, pw), F32), pltpu.VMEM((tm + 2 * POOL_HALO, pw), F32),
               pltpu.VMEM((tm + 2 * POOL_HALO, LANES), F32)]
    if with_proj:
        mod_i, g_pre_i, w_in, cos, sin, gq, gk, qfill = proj
        nw = w_in[0].shape[2]
        in_specs += [
            pl.BlockSpec((None, N_MOD, d), lambda s: (cur(s)[0], 0, 0)),
            pl.BlockSpec((1, d), row),
            _layer_spec(w_in),
            pl.BlockSpec((tm, LANES), lambda s: (prev(s)[1], 0)),
            pl.BlockSpec((tm, LANES), lambda s: (prev(s)[1], 0)),
            pl.BlockSpec((1, LANES), row),
            pl.BlockSpec((1, LANES), row),
            pl.BlockSpec((1, LANES), row),
        ]
        args += [mod_i, g_pre_i, w_in[0], cos, sin, gq, gk, qfill]
        proj_shape, proj_specs = _in_proj_outputs(b, n, tm, prev)
        out_shape += proj_shape
        out_specs += proj_specs
        scratch = [pltpu.VMEM((tm, nw), F32)] + scratch
    outs = pl.pallas_call(
        functools.partial(_tail_kernel, tpb=tpb, chunk=chunk, st=min(tm, 512), with_proj=with_proj),
        out_shape=out_shape,
        grid=(b * tpb + (1 if with_proj else 0),),
        in_specs=in_specs,
        out_specs=out_specs,
        scratch_shapes=scratch,
        compiler_params=_cparams("arbitrary"),
        name="tail_proj" if with_proj else "tail",
    )(*args)
    return outs[0], outs[1:]


def _rope_tables(n):
    rows = n // GRID_W
    row = jnp.broadcast_to(jnp.arange(rows)[:, None], (rows, GRID_W)).reshape(-1).astype(F32)
    col = jnp.broadcast_to(jnp.arange(GRID_W)[None, :], (rows, GRID_W)).reshape(-1).astype(F32)
    freq = ROPE_THETA ** (-jnp.arange(ROPE_FREQS, dtype=F32) / ROPE_FREQS)
    ar, ac = row[:, None] * freq, col[:, None] * freq
    cos = jnp.concatenate([jnp.cos(ar), jnp.cos(ar), jnp.cos(ac), jnp.cos(ac)], axis=1)
    sin = jnp.concatenate([-jnp.sin(ar), jnp.sin(ar), -jnp.sin(ac), jnp.sin(ac)], axis=1)
    return jnp.tile(cos, (1, LANES // HEAD_DIM)), jnp.tile(sin, (1, LANES // HEAD_DIM))


def _shift_row(m):
    pieces, rest = [], m.astype(F32)
    for _ in range(N_SHIFT):
        part = rest.astype(BF16).astype(F32)
        pieces.append(-part)
        rest = rest - part
    return jnp.zeros((1, LANES), F32).at[0, HEAD_DIM:HEAD_DIM + N_SHIFT].set(jnp.stack(pieces))


def _pool_inv_counts(n):
    t = jnp.arange(n)
    cols = []
    for w in POOL_WINDOWS:
        hi = jnp.clip(t + (w - 1 - w // 2), 0, n - 1)
        lo = jnp.clip(t - w // 2, 0, n - 1)
        cols.append(jnp.broadcast_to((1.0 / (hi - lo + 1).astype(F32))[:, None], (n, HEAD_DIM)))
    return jnp.concatenate(cols, axis=1)


def _tile(n, pref):
    t = min(n, pref)
    assert n % t == 0, (n, t)
    return t


def kernel(x, c, ctx, c_ctx, w_mod, b_mod, g_pre_mix, g_post_mix, g_pre_ffn, g_post_ffn, w_in, w_pool, pool_scale, win_sink, g_qnorm, g_knorm, w_out, w_gate, w_up, w_down):
    b, n, d = x.shape
    lc = ctx.shape[1]
    depth = w_mod.shape[0]
    assert w_in.shape[2] == IN_WIDTH and KV_WIDTH == LANES and n % GRID_W == 0

    cos, sin = _rope_tables(n)
    cos_c, sin_c = jnp.ones((lc, LANES), F32), jnp.zeros((lc, LANES), F32)

    mod_rows = -(-(b + 1) // 8) * 8
    cc = jnp.concatenate([c, c_ctx[None, :], jnp.zeros((mod_rows - b - 1, d), F32)], axis=0)
    mod_all = _modulation(cc, w_mod, b_mod)

    tm = _tile(n, 512)
    tq_win = _tile(n, 1024)
    tq_glb = _tile(n, 512)
    tk_glb = n // (N_KV_HEADS * (tq_glb // WINDOW))
    assert tk_glb % LANES == 0
    rb_glb = 256
    ffn_chunk = 256

    def layer(l):
        bound = ATTN_SCALE * HEAD_DIM * jnp.max(jnp.abs(g_qnorm[l])) * jnp.max(jnp.abs(g_knorm[l]))
        return dict(
            m=mod_all[l, :b].reshape(b, N_MOD, d),
            mc=jnp.broadcast_to(mod_all[l, b].reshape(1, N_MOD, d), (b, N_MOD, d)),
            w_in=(w_in_b, l),
            gq=jnp.tile(g_qnorm[l], LANES // HEAD_DIM)[None, :],
            gk=jnp.tile(g_knorm[l], LANES // HEAD_DIM)[None, :],
            g_pre=g_pre_mix[l][None, :],
            bound=bound,
            qfill=jnp.where(bound <= SAFE_SHIFT, _shift_row(bound), 0.0),
        )

    w_in_b, w_out_b, w_gate_b, w_up_b, w_down_b = (w.astype(BF16) for w in (w_in, w_out, w_gate, w_up, w_down))
    no_fill = jnp.zeros((1, LANES), F32)
    icnt, icnt_c = _pool_inv_counts(n), _pool_inv_counts(lc)
    xc = ctx
    p = layer(0)
    proj = _in_proj(x, p["m"], p["g_pre"], p["w_in"], cos, sin, p["gq"], p["gk"], p["qfill"], tm, carried=True)
    for l in range(depth):
        last = l == depth - 1
        m, mc, gq, gk = p["m"], p["mc"], p["gq"], p["gk"]
        w_out_l, w_gate_l, w_up_l, w_down_l = (w_out_b, l), (w_gate_b, l), (w_up_b, l), (w_down_b, l)
        w_pool_bd = jax.scipy.linalg.block_diag(*[w_pool[l, g] for g in range(len(POOL_WINDOWS))]).astype(BF16)
        pscale = pool_scale[l][None, :]
        sink = jnp.zeros((8, LANES), F32).at[:N_Q_HEADS].set(jnp.broadcast_to(win_sink[l][:, None], (N_Q_HEADS, LANES)))
        g_post = g_post_mix[l][None, :]
        g_pre_f, g_post_f = g_pre_ffn[l][None, :], g_post_ffn[l][None, :]

        a, qw, ktw, vw, qg, ktg, vg = proj
        ac, qwc, ktwc, vwc, qgc, ktgc, vgc = _in_proj(xc, mc, p["g_pre"], p["w_in"], cos_c, sin_c, gq, gk, no_fill, lc)

        def attn_bounded(*ops):
            o_g, o_w = _attn(*ops, tq=tq_glb, tk=tk_glb, rb=rb_glb)
            return o_w, o_g

        def attn_exact(qg, ktg, vg, ktgc, vgc, qw, ktw, vw, ktwc, vwc, sink):
            return (_window_attn(qw, ktw, vw, ktwc, vwc, sink, tq_win),
                    _global_attn(qg, ktg, vg, ktgc, vgc, tq=tq_glb, tk=tk_glb, rb=rb_glb))

        o_win, o_glb = lax.cond(p["bound"] <= SAFE_SHIFT, attn_bounded, attn_exact,
                                qg, ktg, vg, ktgc, vgc, qw, ktw, vw, ktwc, vwc, sink)
        tail_args = (x, a, icnt, o_win, o_glb, w_pool_bd, pscale, w_out_l, g_post, m, g_pre_f, g_post_f,
                     w_gate_l, w_up_l, w_down_l)
        if last:
            x, _ = _tail(*tail_args, None, tm, ffn_chunk)
        else:
            oc_win = _ctx_attn(qwc, ktwc, vwc, sink)
            oc_glb = _ctx_attn(qgc, ktgc, vgc, None)
            xc = _out_proj(xc, ac, icnt_c, oc_win, oc_glb, w_pool_bd, pscale, w_out_l, g_post, mc, lc)
            xc = _ffn(xc, mc, g_pre_f, g_post_f, w_gate_l, w_up_l, w_down_l, lc, ffn_chunk)
            p = layer(l + 1)
            x, proj = _tail(*tail_args, (p["m"], p["g_pre"], p["w_in"], cos, sin, p["gq"], p["gk"], p["qfill"]),
                            tm, ffn_chunk)
    return x
```

```python
import functools

import jax
import jax.numpy as jnp
from jax import lax
from jax.experimental import pallas as pl
from jax.experimental.pallas import tpu as pltpu

F32 = jnp.float32
BF16 = jnp.bfloat16

HEAD_DIM = 64
GRID_W = 64
ROPE_FREQS = HEAD_DIM // 4
ROPE_THETA = 10000.0
NORM_EPS = 1e-6
NEG_INF = -1e30
ATTN_SCALE = HEAD_DIM ** -0.5
WINDOW = 128
POOL_WINDOWS = (2, 4, 8, 16)
POOL_HALO = 16
N_Q_HEADS = 6
N_KV_HEADS = 2
KV_GROUP = N_Q_HEADS // N_KV_HEADS
N_MOD = 6
POOL_WIDTH = len(POOL_WINDOWS) * HEAD_DIM
Q_WIDTH = N_Q_HEADS * HEAD_DIM
KV_WIDTH = N_KV_HEADS * HEAD_DIM
COL_QW = POOL_WIDTH
COL_KW = COL_QW + Q_WIDTH
COL_VW = COL_KW + KV_WIDTH
COL_QG = COL_VW + KV_WIDTH
COL_KG = COL_QG + Q_WIDTH
COL_VG = COL_KG + KV_WIDTH
IN_WIDTH = COL_VG + KV_WIDTH
N_SHIFT = 3
SAFE_SHIFT = 30.0

LANES = 128
VMEM_LIMIT = 56 * 1024 * 1024


def _cparams(*sem):
    return pltpu.CompilerParams(dimension_semantics=sem, vmem_limit_bytes=VMEM_LIMIT)


def _resident(block_shape, index_map):
    return pl.BlockSpec(block_shape, index_map, pipeline_mode=pl.Buffered(1))


def _layer_spec(wl):
    w, l = wl
    return pl.BlockSpec((None,) + w.shape[1:], lambda *g: (l,) + (0,) * (w.ndim - 1), pipeline_mode=pl.Buffered(1))


def _rms(x, g):
    ms = jnp.mean(x * x, axis=-1, keepdims=True)
    return (x * lax.rsqrt(ms + NORM_EPS)) * g


def _mod_kernel(c_ref, w_ref, b_ref, o_ref):
    c = c_ref[...]
    s = c * jax.nn.sigmoid(c)
    o_ref[...] = jnp.dot(s.astype(BF16), w_ref[...].astype(BF16), preferred_element_type=F32) + b_ref[...]


def _modulation(cc, w_mod, b_mod):
    depth, d, dm = w_mod.shape
    rows = cc.shape[0]
    tn = dm // 4
    return pl.pallas_call(
        _mod_kernel,
        out_shape=jax.ShapeDtypeStruct((depth, rows, dm), F32),
        grid=(depth, dm // tn),
        in_specs=[
            pl.BlockSpec((rows, d), lambda l, j: (0, 0)),
            pl.BlockSpec((None, d, tn), lambda l, j: (l, 0, j)),
            pl.BlockSpec((None, 1, tn), lambda l, j: (l, 0, j)),
        ],
        out_specs=pl.BlockSpec((None, rows, tn), lambda l, j: (l, 0, j)),
        compiler_params=_cparams("arbitrary", "arbitrary"),
        name="modulation",
    )(cc, w_mod, b_mod.reshape(depth, 1, dm))


def _in_proj_epilogue(u_of, n_rows, st, cos_ref, sin_ref, gq_ref, gk_ref, qfill_ref,
                      a_ref, qw_ref, ktw_ref, vw_ref, qg_ref, ktg_ref, vg_ref):
    lane = lax.broadcasted_iota(jnp.int32, (st, LANES), 1)
    lo = lane < HEAD_DIM
    first = (lane & ROPE_FREQS) == 0
    shift_rows = (lax.broadcasted_iota(jnp.int32, (HEAD_DIM, LANES), 0) < N_SHIFT).astype(BF16)
    sum_lane = (lane == HEAD_DIM).astype(F32)
    gq, gk = gq_ref[...], gk_ref[...]
    qfill = qfill_ref[...]

    def head_norm(t, g):
        sq = t * t
        s_lo = jnp.sum(jnp.where(lo, sq, 0.0), axis=-1, keepdims=True)
        s_hi = jnp.sum(jnp.where(lo, 0.0, sq), axis=-1, keepdims=True)
        ms = jnp.where(lo, s_lo, s_hi) * (1.0 / HEAD_DIM)
        return (t * lax.rsqrt(ms + NORM_EPS)) * g

    for sub in range(n_rows // st):
        rs = slice(sub * st, (sub + 1) * st)
        u = u_of(sub)
        cos = cos_ref[rs, :]
        sin = sin_ref[rs, :]

        def rope(t):
            partner = jnp.where(first, pltpu.roll(t, LANES - ROPE_FREQS, 1), pltpu.roll(t, ROPE_FREQS, 1))
            return t * cos + partner * sin

        def store_branch(q_chunks, k, v, fill, q_ref, kt_ref, v_ref):
            for hd in range(N_Q_HEADS):
                t = q_chunks[hd // 2]
                if hd % 2:
                    t = pltpu.roll(t, HEAD_DIM, 1)
                q_ref[hd, rs, :] = jnp.where(lo, t, fill).astype(BF16)
            for r in range(st // LANES):
                kb = sub * (st // LANES) + r
                kt = jnp.transpose(k[r * LANES:(r + 1) * LANES, :]).astype(BF16)
                for kv in range(N_KV_HEADS):
                    kt_ref[kv, kb, 0:HEAD_DIM, :] = kt[kv * HEAD_DIM:(kv + 1) * HEAD_DIM, :]
                    kt_ref[kv, kb, HEAD_DIM:LANES, :] = shift_rows
            v_ref[0, rs, :] = jnp.where(lo, v, sum_lane).astype(BF16)
            v_ref[1, rs, :] = jnp.where(lo, pltpu.roll(v, HEAD_DIM, 1), sum_lane).astype(BF16)

        def cols(start, width):
            return u[:, start:start + width]

        a_ref[rs, :] = cols(0, POOL_WIDTH)
        store_branch([rope(cols(COL_QW + LANES * c, LANES)) * ATTN_SCALE for c in range(Q_WIDTH // LANES)],
                     rope(cols(COL_KW, KV_WIDTH)), cols(COL_VW, KV_WIDTH), 0.0, qw_ref, ktw_ref, vw_ref)
        store_branch([rope(head_norm(cols(COL_QG + LANES * c, LANES), gq)) * ATTN_SCALE for c in range(Q_WIDTH // LANES)],
                     rope(head_norm(cols(COL_KG, KV_WIDTH), gk)), cols(COL_VG, KV_WIDTH), qfill, qg_ref, ktg_ref, vg_ref)


def _project(x, mod_ref, g_ref, w_ref):
    h = _rms(x, g_ref[...])
    h = h * (1.0 + mod_ref[1:2, :]) + mod_ref[0:1, :]
    return jnp.dot(h.astype(BF16), w_ref[...], preferred_element_type=F32)


def _in_proj_kernel(x_ref, mod_ref, g_ref, w_ref, cos_ref, sin_ref, gq_ref, gk_ref, qfill_ref, *out_refs, st):
    tm = x_ref.shape[0]
    us = [_project(x_ref[sub * st:(sub + 1) * st, :], mod_ref, g_ref, w_ref) for sub in range(tm // st)]
    _in_proj_epilogue(lambda sub: us[sub], tm, st, cos_ref, sin_ref, gq_ref, gk_ref, qfill_ref, *out_refs)


def _in_proj_outputs(b, n, tm, tile_of):
    qshape = jax.ShapeDtypeStruct((b, N_Q_HEADS, n, LANES), BF16)
    ktshape = jax.ShapeDtypeStruct((b, N_KV_HEADS, n // LANES, LANES, LANES), BF16)
    vshape = jax.ShapeDtypeStruct((b, N_KV_HEADS, n, LANES), BF16)

    def spec(block, where):
        return pl.BlockSpec(block, lambda *g: where(*tile_of(*g)))

    aspec = spec((None, tm, POOL_WIDTH), lambda bi, i: (bi, i, 0))
    qspec = spec((None, N_Q_HEADS, tm, LANES), lambda bi, i: (bi, 0, i, 0))
    ktspec = spec((None, N_KV_HEADS, tm // LANES, LANES, LANES), lambda bi, i: (bi, 0, i, 0, 0))
    vspec = spec((None, N_KV_HEADS, tm, LANES), lambda bi, i: (bi, 0, i, 0))
    shapes = (jax.ShapeDtypeStruct((b, n, POOL_WIDTH), F32), qshape, ktshape, vshape, qshape, ktshape, vshape)
    return shapes, (aspec, qspec, ktspec, vspec, qspec, ktspec, vspec)


def _in_proj_carried_kernel(x_ref, mod_ref, g_ref, w_ref, cos_ref, sin_ref, gq_ref, gk_ref, qfill_ref, *refs, st):
    out_refs, u_ref = refs[:-1], refs[-1]

    @pl.when(pl.program_id(0) == 0)
    def _():
        u_ref[...] = jnp.zeros(u_ref.shape, F32)

    _in_proj_epilogue(lambda sub: u_ref[sub * st:(sub + 1) * st, :], x_ref.shape[0], st, cos_ref, sin_ref,
                      gq_ref, gk_ref, qfill_ref, *out_refs)
    u_ref[...] = _project(x_ref[...], mod_ref, g_ref, w_ref)


def _in_proj(x, mod, g_pre, w_in, cos, sin, gq, gk, qfill, tm, carried=False):
    b, n, d = x.shape
    nw = w_in[0].shape[2]
    tpb = n // tm
    last = b * tpb - 1
    if carried:
        grid = (b * tpb + 1,)
        cur = lambda s: (jnp.minimum(s, last) // tpb, jnp.minimum(s, last) % tpb)
        prev = lambda s: (jnp.maximum(s - 1, 0) // tpb, jnp.maximum(s - 1, 0) % tpb)
        kern, scratch, sem = _in_proj_carried_kernel, [pltpu.VMEM((tm, nw), F32)], ("arbitrary",)
    else:
        grid = (b, tpb)
        cur = prev = lambda bi, i: (bi, i)
        kern, scratch, sem = _in_proj_kernel, [], ("arbitrary", "arbitrary")
    out_shape, out_specs = _in_proj_outputs(b, n, tm, prev)
    row = lambda *g: (0, 0)
    return pl.pallas_call(
        functools.partial(kern, st=min(tm, 512)),
        out_shape=out_shape,
        grid=grid,
        in_specs=[
            pl.BlockSpec((None, tm, d), lambda *g: (cur(*g)[0], cur(*g)[1], 0)),
            pl.BlockSpec((None, N_MOD, d), lambda *g: (cur(*g)[0], 0, 0)),
            pl.BlockSpec((1, d), row),
            _layer_spec(w_in),
            pl.BlockSpec((tm, LANES), lambda *g: (prev(*g)[1], 0)),
            pl.BlockSpec((tm, LANES), lambda *g: (prev(*g)[1], 0)),
            pl.BlockSpec((1, LANES), row),
            pl.BlockSpec((1, LANES), row),
            pl.BlockSpec((1, LANES), row),
        ],
        out_specs=out_specs,
        scratch_shapes=scratch,
        compiler_params=_cparams(*sem),
        name="in_proj",
    )(x, mod, g_pre, w_in[0], cos, sin, gq, gk, qfill)


def _merge_heads(outs, lo):
    return [jnp.where(lo, outs[2 * c], pltpu.roll(outs[2 * c + 1], HEAD_DIM, 1)) for c in range(N_Q_HEADS // 2)]


def _key_tile(kt_ref, first_block, n_blocks):
    return jnp.concatenate([kt_ref[first_block + j] for j in range(n_blocks)], axis=1)


def _stacked_sink(sink_ref, kv, rows_per_head):
    return jnp.concatenate([jnp.broadcast_to(sink_ref[hd:hd + 1, 0:1], (rows_per_head, 1))
                            for hd in range(kv * KV_GROUP, (kv + 1) * KV_GROUP)], axis=0)


def _window_units(q_ref, kt_ref, v_ref, ktc_ref, vc_ref, sink_ref, first_blk):
    nkb = kt_ref.shape[1]
    sub = WINDOW
    rows = KV_GROUP * sub
    r_in = lax.broadcasted_iota(jnp.int32, (rows, LANES), 0) & (sub - 1)
    c_in = lax.broadcasted_iota(jnp.int32, (rows, LANES), 1)
    ctx = [(_key_tile(ktc_ref.at[kv], 0, ktc_ref.shape[1]), vc_ref[kv], _stacked_sink(sink_ref, kv, sub))
           for kv in range(N_KV_HEADS)]

    def blocks(j):
        blk = first_blk + j
        return blk, jnp.maximum(blk - 1, 0), jnp.minimum(blk + 1, nkb - 1)

    def logits(j, kv):
        blk, prev_blk, next_blk = blocks(j)
        q = q_ref[kv * KV_GROUP:(kv + 1) * KV_GROUP, pl.ds(pl.multiple_of(j * sub, sub), sub), :].reshape(rows, LANES)
        kt = jnp.concatenate([kt_ref[kv, prev_blk], kt_ref[kv, blk], kt_ref[kv, next_blk]], axis=1)
        s = jnp.dot(q, kt, preferred_element_type=F32)
        sc = jnp.dot(q, ctx[kv][0], preferred_element_type=F32)
        return s, sc

    def attend(j, kv, s, sc):
        blk, prev_blk, next_blk = blocks(j)
        _, vc, sk = ctx[kv]
        v = jnp.concatenate([v_ref[kv, pl.ds(pl.multiple_of(kb * sub, sub), sub), :]
                             for kb in (prev_blk, blk, next_blk)], axis=0)
        s_prev = jnp.where(jnp.logical_and(c_in >= r_in, blk > 0), s[:, 0:sub], NEG_INF)
        s_own = s[:, sub:2 * sub]
        s_next = jnp.where(jnp.logical_and(c_in <= r_in, blk < nkb - 1), s[:, 2 * sub:3 * sub], NEG_INF)
        mx = jnp.maximum(jnp.maximum(s_prev, s_own), s_next)
        for t in range(sc.shape[1] // LANES):
            mx = jnp.maximum(mx, sc[:, t * LANES:(t + 1) * LANES])
        m = jnp.maximum(jnp.max(mx, axis=-1, keepdims=True), sk)
        p = jnp.concatenate([jnp.exp(s_prev - m), jnp.exp(s_own - m), jnp.exp(s_next - m)], axis=1)
        pv = jnp.dot(p.astype(BF16), v, preferred_element_type=F32)
        pv = pv + jnp.dot(jnp.exp(sc - m).astype(BF16), vc, preferred_element_type=F32)
        l = pv[:, HEAD_DIM:HEAD_DIM + 1] + jnp.exp(sk - m)
        o = pv / l
        return [o[g * sub:(g + 1) * sub] for g in range(KV_GROUP)]

    return logits, attend


def _store_heads(o_ref, row0, outs):
    nrows = outs[0].shape[0]
    lo = lax.broadcasted_iota(jnp.int32, (nrows, LANES), 1) < HEAD_DIM
    for c, chunk in enumerate(_merge_heads(outs, lo)):
        o_ref[pl.ds(row0, nrows), c * LANES:(c + 1) * LANES] = chunk.astype(BF16)


def _window_attn_kernel(q_ref, kt_ref, v_ref, ktc_ref, vc_ref, sink_ref, o_ref):
    tq = q_ref.shape[1]
    sub = WINDOW
    logits, attend = _window_units(q_ref, kt_ref, v_ref, ktc_ref, vc_ref, sink_ref, pl.program_id(1) * (tq // sub))
    units = [(j, kv) for j in range(tq // sub) for kv in range(N_KV_HEADS)]
    pending = logits(*units[0])
    outs = []
    for idx, (j, kv) in enumerate(units):
        ahead = logits(*units[idx + 1]) if idx + 1 < len(units) else None
        outs += attend(j, kv, *pending)
        pending = ahead
        if kv == N_KV_HEADS - 1:
            _store_heads(o_ref, j * sub, outs)
            outs = []


def _window_attn(q, kt, v, ktc, vc, sink, tq):
    b, _, n, _ = q.shape
    lc = vc.shape[2]
    assert tq % WINDOW == 0 and WINDOW == LANES
    batch4 = lambda bi, i: (bi, 0, 0, 0)
    batch5 = lambda bi, i: (bi, 0, 0, 0, 0)
    return pl.pallas_call(
        _window_attn_kernel,
        out_shape=jax.ShapeDtypeStruct((b, n, N_Q_HEADS * HEAD_DIM), BF16),
        grid=(b, n // tq),
        in_specs=[
            pl.BlockSpec((None, N_Q_HEADS, tq, LANES), lambda bi, i: (bi, 0, i, 0)),
            pl.BlockSpec((None, N_KV_HEADS, n // LANES, LANES, LANES), batch5),
            pl.BlockSpec((None, N_KV_HEADS, n, LANES), batch4),
            pl.BlockSpec((None, N_KV_HEADS, lc // LANES, LANES, LANES), batch5),
            pl.BlockSpec((None, N_KV_HEADS, lc, LANES), batch4),
            pl.BlockSpec(sink.shape, lambda bi, i: (0, 0)),
        ],
        out_specs=pl.BlockSpec((None, tq, N_Q_HEADS * HEAD_DIM), lambda bi, i: (bi, i, 0)),
        compiler_params=_cparams("arbitrary", "arbitrary"),
        name="window_attn",
    )(q, kt, v, ktc, vc, sink)


def _ctx_attn_kernel(*refs, has_sink):
    if has_sink:
        q_ref, kt_ref, v_ref, sink_ref, o_ref = refs
    else:
        q_ref, kt_ref, v_ref, o_ref = refs
    tq = q_ref.shape[1]
    rows = KV_GROUP * tq
    outs = []
    for kv in range(N_KV_HEADS):
        q = q_ref[kv * KV_GROUP:(kv + 1) * KV_GROUP].reshape(rows, LANES)
        s = jnp.dot(q, _key_tile(kt_ref.at[kv], 0, kt_ref.shape[1]), preferred_element_type=F32)
        m = jnp.max(s, axis=-1, keepdims=True)
        if has_sink:
            sk = _stacked_sink(sink_ref, kv, tq)
            m = jnp.maximum(m, sk)
        pv = jnp.dot(jnp.exp(s - m).astype(BF16), v_ref[kv], preferred_element_type=F32)
        l = pv[:, HEAD_DIM:HEAD_DIM + 1]
        if has_sink:
            l = l + jnp.exp(sk - m)
        o = pv / l
        outs += [o[g * tq:(g + 1) * tq] for g in range(KV_GROUP)]
    _store_heads(o_ref, 0, outs)


def _ctx_attn(q, kt, v, sink):
    b, _, n, _ = q.shape
    args = [q, kt, v]
    in_specs = [pl.BlockSpec((None,) + a.shape[1:], lambda bi, nd=a.ndim: (bi,) + (0,) * (nd - 1)) for a in args]
    if sink is not None:
        args.append(sink)
        in_specs.append(pl.BlockSpec(sink.shape, lambda bi: (0, 0)))
    return pl.pallas_call(
        functools.partial(_ctx_attn_kernel, has_sink=sink is not None),
        out_shape=jax.ShapeDtypeStruct((b, n, N_Q_HEADS * HEAD_DIM), BF16),
        grid=(b,),
        in_specs=in_specs,
        out_specs=pl.BlockSpec((None, n, N_Q_HEADS * HEAD_DIM), lambda bi: (bi, 0, 0)),
        compiler_params=_cparams("arbitrary"),
        name="ctx_attn",
    )(*args)


def _global_attn_kernel(q_ref, kt_ref, v_ref, ktc_ref, vc_ref, o_ref, qs_ref, acc_ref, mx_ref, *, tk, rb):
    tq = q_ref.shape[1]
    rows = KV_GROUP * tq
    kb_per_step = tk // LANES
    n_steps = kt_ref.shape[1] // kb_per_step
    lane = lax.broadcasted_iota(jnp.int32, (rows, LANES), 1)

    def key_steps(fn):
        def step(kc, carry):
            for kv in range(N_KV_HEADS):
                fn(kv, _key_tile(kt_ref.at[kv], kc * kb_per_step, kb_per_step),
                   v_ref[kv, pl.ds(pl.multiple_of(kc * tk, tk), tk), :])
            return carry
        lax.fori_loop(0, n_steps, step, 0)
        for kv in range(N_KV_HEADS):
            fn(kv, _key_tile(ktc_ref.at[kv], 0, ktc_ref.shape[1]), vc_ref[kv])

    def q_rows(kv):
        return q_ref[kv * KV_GROUP:(kv + 1) * KV_GROUP].reshape(rows, LANES)

    mx_ref[...] = jnp.full(mx_ref.shape, NEG_INF, F32)

    def track_max(kv, kt, v):
        s = jnp.dot(q_rows(kv), kt, preferred_element_type=F32)
        mx = mx_ref[kv]
        for j in range(kt.shape[1] // LANES):
            mx = jnp.maximum(mx, s[:, j * LANES:(j + 1) * LANES])
        mx_ref[kv] = mx

    key_steps(track_max)

    for kv in range(N_KV_HEADS):
        qs = q_rows(kv).astype(F32)
        rest = jnp.max(mx_ref[kv], axis=-1, keepdims=True)
        for piece in range(N_SHIFT):
            part = rest.astype(BF16).astype(F32)
            qs = jnp.where(lane == HEAD_DIM + piece, -part, qs)
            rest = rest - part
        qs_ref[kv] = qs.astype(BF16)

    acc_ref[...] = jnp.zeros(acc_ref.shape, F32)

    def accumulate(kv, kt, v):
        for r in range(rows // rb):
            rsl = slice(r * rb, (r + 1) * rb)
            s = jnp.dot(qs_ref[kv, rsl, :], kt, preferred_element_type=F32)
            p = jnp.exp(s).astype(BF16)
            acc_ref[kv, rsl, :] += jnp.dot(p, v, preferred_element_type=F32)

    key_steps(accumulate)

    outs = []
    for hd in range(N_Q_HEADS):
        a = acc_ref[hd // KV_GROUP, (hd % KV_GROUP) * tq:(hd % KV_GROUP + 1) * tq, :]
        outs.append(a / a[:, HEAD_DIM:HEAD_DIM + 1])
    _store_heads(o_ref, 0, outs)


def _global_attn(q, kt, v, ktc, vc, *, tq, tk, rb):
    b, _, n, _ = q.shape
    lc = vc.shape[2]
    rows = KV_GROUP * tq
    batch4 = lambda bi, i: (bi, 0, 0, 0)
    batch5 = lambda bi, i: (bi, 0, 0, 0, 0)
    return pl.pallas_call(
        functools.partial(_global_attn_kernel, tk=tk, rb=rb),
        out_shape=jax.ShapeDtypeStruct((b, n, N_Q_HEADS * HEAD_DIM), BF16),
        grid=(b, n // tq),
        in_specs=[
            pl.BlockSpec((None, N_Q_HEADS, tq, LANES), lambda bi, i: (bi, 0, i, 0)),
            pl.BlockSpec((None, N_KV_HEADS, n // LANES, LANES, LANES), batch5),
            pl.BlockSpec((None, N_KV_HEADS, n, LANES), batch4),
            pl.BlockSpec((None, N_KV_HEADS, lc // LANES, LANES, LANES), batch5),
            pl.BlockSpec((None, N_KV_HEADS, lc, LANES), batch4),
        ],
        out_specs=pl.BlockSpec((None, tq, N_Q_HEADS * HEAD_DIM), lambda bi, i: (bi, i, 0)),
        scratch_shapes=[pltpu.VMEM((N_KV_HEADS, rows, LANES), BF16), pltpu.VMEM((N_KV_HEADS, rows, LANES), F32),
                        pltpu.VMEM((N_KV_HEADS, rows, LANES), F32)],
        compiler_params=_cparams("arbitrary", "arbitrary"),
        name="global_attn_exact_max",
    )(q, kt, v, ktc, vc)


def _attn_kernel(qg_ref, ktg_ref, vg_ref, ktgc_ref, vgc_ref, qw_ref, ktw_ref, vw_ref, ktwc_ref, vwc_ref, sink_ref,
                 og_ref, ow_ref, acc_ref, *, tk, rb):
    tq = qg_ref.shape[1]
    rows = KV_GROUP * tq
    sub = WINDOW
    n_sub = tq // sub
    kb_per_step = tk // LANES
    steps_per_sub = ktg_ref.shape[1] // kb_per_step // n_sub
    assert steps_per_sub * n_sub * kb_per_step == ktg_ref.shape[1] and tq % rb == 0

    def global_keys(kv, kt, v, assign):
        for r in range(rows // rb):
            hd, off = kv * KV_GROUP + (r * rb) // tq, (r * rb) % tq
            s = jnp.dot(qg_ref[hd, off:off + rb, :], kt, preferred_element_type=F32)
            pv = jnp.dot(jnp.exp(s).astype(BF16), v, preferred_element_type=F32)
            if assign:
                acc_ref[kv, r * rb:(r + 1) * rb, :] = pv
            else:
                acc_ref[kv, r * rb:(r + 1) * rb, :] += pv

    def global_step(kc):
        for kv in range(N_KV_HEADS):
            global_keys(kv, _key_tile(ktg_ref.at[kv], kc * kb_per_step, kb_per_step),
                        vg_ref[kv, pl.ds(pl.multiple_of(kc * tk, tk), tk), :], False)

    for kv in range(N_KV_HEADS):
        global_keys(kv, _key_tile(ktgc_ref.at[kv], 0, ktgc_ref.shape[1]), vgc_ref[kv], True)

    logits, attend = _window_units(qw_ref, ktw_ref, vw_ref, ktwc_ref, vwc_ref, sink_ref, pl.program_id(1) * n_sub)

    def body(j, carry):
        pending = logits(j, 0)
        outs = []
        for kv in range(N_KV_HEADS):
            for t in range(steps_per_sub // N_KV_HEADS):
                global_step(j * steps_per_sub + kv * (steps_per_sub // N_KV_HEADS) + t)
            ahead = logits(j, kv + 1) if kv + 1 < N_KV_HEADS else None
            outs += attend(j, kv, *pending)
            pending = ahead
        _store_heads(ow_ref, pl.multiple_of(j * sub, sub), outs)
        return carry

    assert steps_per_sub % N_KV_HEADS == 0
    lax.fori_loop(0, n_sub, body, 0, unroll=True)

    outs = []
    for hd in range(N_Q_HEADS):
        a = acc_ref[hd // KV_GROUP, (hd % KV_GROUP) * tq:(hd % KV_GROUP + 1) * tq, :]
        outs.append(a / a[:, HEAD_DIM:HEAD_DIM + 1])
    _store_heads(og_ref, 0, outs)


def _attn(qg, ktg, vg, ktgc, vgc, qw, ktw, vw, ktwc, vwc, sink, *, tq, tk, rb):
    b, _, n, _ = qg.shape
    lc = vgc.shape[2]
    batch4 = lambda bi, i: (bi, 0, 0, 0)
    batch5 = lambda bi, i: (bi, 0, 0, 0, 0)
    branch_specs = [
        pl.BlockSpec((None, N_Q_HEADS, tq, LANES), lambda bi, i: (bi, 0, i, 0)),
        pl.BlockSpec((None, N_KV_HEADS, n // LANES, LANES, LANES), batch5),
        pl.BlockSpec((None, N_KV_HEADS, n, LANES), batch4),
        pl.BlockSpec((None, N_KV_HEADS, lc // LANES, LANES, LANES), batch5),
        pl.BlockSpec((None, N_KV_HEADS, lc, LANES), batch4),
    ]
    oshape = jax.ShapeDtypeStruct((b, n, N_Q_HEADS * HEAD_DIM), BF16)
    ospec = pl.BlockSpec((None, tq, N_Q_HEADS * HEAD_DIM), lambda bi, i: (bi, i, 0))
    return pl.pallas_call(
        functools.partial(_attn_kernel, tk=tk, rb=rb),
        out_shape=(oshape, oshape),
        grid=(b, n // tq),
        in_specs=branch_specs + branch_specs + [pl.BlockSpec(sink.shape, lambda bi, i: (0, 0))],
        out_specs=(ospec, ospec),
        scratch_shapes=[pltpu.VMEM((N_KV_HEADS, KV_GROUP * tq, LANES), F32)],
        compiler_params=_cparams("arbitrary", "arbitrary"),
        name="attn",
    )(qg, ktg, vg, ktgc, vgc, qw, ktw, vw, ktwc, vwc, sink)


def _out_proj_tile(has_prev, has_next, x_ref, a_ref, ap_ref, an_ref, icnt_ref, ow_ref, og_ref, wpool_ref, pscale_ref,
                   wo_ref, g_ref, mod_ref, ext_ref, s2_ref, s4_ref, s8_ref):
    tm = x_ref.shape[0]
    ext_ref[0:POOL_HALO, :] = jnp.where(has_prev, ap_ref[...], 0.0)
    ext_ref[POOL_HALO:POOL_HALO + tm, :] = a_ref[...]
    ext_ref[POOL_HALO + tm:2 * POOL_HALO + tm, :] = jnp.where(has_next, an_ref[...], 0.0)

    pw = a_ref.shape[1]
    aw = ow_ref.shape[1]
    y = jnp.dot(ow_ref[...], wo_ref[pw:pw + aw, :], preferred_element_type=F32)
    y = y + jnp.dot(og_ref[...], wo_ref[pw + aw:pw + 2 * aw, :], preferred_element_type=F32)

    lo = lax.broadcasted_iota(jnp.int32, (tm, LANES), 1) < HEAD_DIM

    pad = POOL_HALO // 2
    ext_ref[2 * POOL_HALO + tm:2 * POOL_HALO + tm + pad, :] = jnp.zeros((pad, ext_ref.shape[1]), F32)
    r2, r4, r8 = tm + 2 * POOL_HALO, tm + 2 * POOL_HALO - pad, tm + POOL_HALO
    s2_ref[0:r2, :] = ext_ref[0:r2, :] + ext_ref[1:r2 + 1, :]
    s4_ref[0:r4, :] = s2_ref[0:r4, :] + s2_ref[2:r4 + 2, :]
    s8_ref[0:r8, :] = s4_ref[0:r8, LANES:2 * LANES] + s4_ref[4:r8 + 4, LANES:2 * LANES]

    def window_sum(w, c):
        first = POOL_HALO - w // 2
        cols = slice(c * LANES, (c + 1) * LANES)
        if w == 2:
            return s2_ref[first:first + tm, cols]
        if w == 4:
            return s4_ref[first:first + tm, cols]
        if w == 8:
            return s8_ref[first:first + tm, :]
        return s8_ref[first:first + tm, :] + s8_ref[first + 8:first + 8 + tm, :]

    feats = []
    for c in range(2):
        w_a, w_b = POOL_WINDOWS[2 * c], POOL_WINDOWS[2 * c + 1]
        x0 = ext_ref[POOL_HALO:POOL_HALO + tm, c * LANES:(c + 1) * LANES]
        mean = jnp.where(lo, window_sum(w_a, c), window_sum(w_b, c)) * icnt_ref[:, c * LANES:(c + 1) * LANES]
        feats.append(mean - x0)
    feats = jnp.concatenate(feats, axis=1).astype(BF16)
    y_pool = jnp.dot(feats, wpool_ref[...], preferred_element_type=F32) * pscale_ref[...]

    y = y + jnp.dot(y_pool.astype(BF16), wo_ref[0:pw, :], preferred_element_type=F32)
    return x_ref[...] + mod_ref[2:3, :] * _rms(y, g_ref[...])


N_OUT_PROJ_IN = 12


def _out_proj_kernel(*refs):
    ins, o_ref, scratch = refs[:N_OUT_PROJ_IN], refs[N_OUT_PROJ_IN], refs[N_OUT_PROJ_IN + 1:]
    i = pl.program_id(1)
    o_ref[...] = _out_proj_tile(i > 0, i < pl.num_programs(1) - 1, *ins, *scratch)


def _out_proj(x, a, inv_count, o_win, o_glb, w_pool_bd, pool_scale, w_out, g_post, mod, tm):
    b, n, d = x.shape
    pw = a.shape[2]
    aw = o_win.shape[2]
    hb = tm // POOL_HALO
    row = lambda bi, i: (0, 0)
    tile = lambda bi, i: (bi, i, 0)
    return pl.pallas_call(
        _out_proj_kernel,
        out_shape=jax.ShapeDtypeStruct((b, n, d), F32),
        grid=(b, n // tm),
        in_specs=[
            pl.BlockSpec((None, tm, d), tile),
            pl.BlockSpec((None, tm, pw), tile),
            pl.BlockSpec((None, POOL_HALO, pw), lambda bi, i: (bi, jnp.maximum(i * hb - 1, 0), 0)),
            pl.BlockSpec((None, POOL_HALO, pw), lambda bi, i: (bi, jnp.minimum((i + 1) * hb, n // POOL_HALO - 1), 0)),
            pl.BlockSpec((tm, pw), lambda bi, i: (i, 0)),
            pl.BlockSpec((None, tm, aw), tile),
            pl.BlockSpec((None, tm, aw), tile),
            _resident((pw, pw), row),
            pl.BlockSpec((1, pw), row),
            _layer_spec(w_out),
            pl.BlockSpec((1, d), row),
            pl.BlockSpec((None, N_MOD, d), lambda bi, i: (bi, 0, 0)),
        ],
        out_specs=pl.BlockSpec((None, tm, d), tile),
        scratch_shapes=[pltpu.VMEM((tm + 2 * POOL_HALO + POOL_HALO // 2, pw), F32),
                        pltpu.VMEM((tm + 2 * POOL_HALO, pw), F32), pltpu.VMEM((tm + 2 * POOL_HALO, pw), F32),
                        pltpu.VMEM((tm + 2 * POOL_HALO, LANES), F32)],
        compiler_params=_cparams("arbitrary", "arbitrary"),
        name="out_proj",
    )(x, a, a, a, inv_count, o_win, o_glb, w_pool_bd, pool_scale, w_out[0], g_post, mod)


def _ffn_tile(x, mod_ref, gpre_ref, gpost_ref, wg_ref, wu_ref, wd_ref, chunk):
    h = _rms(x, gpre_ref[...])
    h = (h * (1.0 + mod_ref[4:5, :]) + mod_ref[3:4, :]).astype(BF16)
    f = None
    for c in range(wg_ref.shape[1] // chunk):
        cols = slice(c * chunk, (c + 1) * chunk)
        g = jnp.dot(h, wg_ref[:, cols], preferred_element_type=F32)
        u = jnp.dot(h, wu_ref[:, cols], preferred_element_type=F32)
        act = ((g * jax.nn.sigmoid(g)) * u).astype(BF16)
        part = jnp.dot(act, wd_ref[cols, :], preferred_element_type=F32)
        f = part if f is None else f + part
    return x + mod_ref[5:6, :] * _rms(f, gpost_ref[...])


def _ffn_kernel(x_ref, mod_ref, gpre_ref, gpost_ref, wg_ref, wu_ref, wd_ref, o_ref, *, chunk):
    o_ref[...] = _ffn_tile(x_ref[...], mod_ref, gpre_ref, gpost_ref, wg_ref, wu_ref, wd_ref, chunk)


def _ffn(x, mod, g_pre, g_post, w_gate, w_up, w_down, tm, chunk):
    b, n, d = x.shape
    row = lambda bi, i: (0, 0)
    tile = lambda bi, i: (bi, i, 0)
    return pl.pallas_call(
        functools.partial(_ffn_kernel, chunk=chunk),
        out_shape=jax.ShapeDtypeStruct((b, n, d), F32),
        grid=(b, n // tm),
        in_specs=[
            pl.BlockSpec((None, tm, d), tile),
            pl.BlockSpec((None, N_MOD, d), lambda bi, i: (bi, 0, 0)),
            pl.BlockSpec((1, d), row),
            pl.BlockSpec((1, d), row),
            _layer_spec(w_gate),
            _layer_spec(w_up),
            _layer_spec(w_down),
        ],
        out_specs=pl.BlockSpec((None, tm, d), tile),
        compiler_params=_cparams("arbitrary", "arbitrary"),
        name="ffn",
    )(x, mod, g_pre, g_post, w_gate[0], w_up[0], w_down[0])


N_FFN_IN = 5
N_PROJ_IN = 8
N_PROJ_OUT = 7


def _tail_kernel(*refs, tpb, chunk, st, with_proj):
    refs = list(refs)
    op_in, refs = refs[:N_OUT_PROJ_IN], refs[N_OUT_PROJ_IN:]
    ffn_in, refs = refs[:N_FFN_IN], refs[N_FFN_IN:]
    if with_proj:
        (modi_ref, gin_ref, win_ref, cos_ref, sin_ref, gq_ref, gk_ref, qfill_ref), refs = refs[:N_PROJ_IN], refs[N_PROJ_IN:]
    o_ref, refs = refs[0], refs[1:]
    if with_proj:
        proj_out, refs = refs[:N_PROJ_OUT], refs[N_PROJ_OUT:]
        u_ref, refs = refs[0], refs[1:]
    pool_scratch = refs
    tm = o_ref.shape[0]
    step = pl.program_id(0)
    i = lax.rem(jnp.minimum(step, pl.num_programs(0) - (2 if with_proj else 1)), tpb)

    if with_proj:
        @pl.when(step == 0)
        def _():
            u_ref[...] = jnp.zeros(u_ref.shape, F32)

        _in_proj_epilogue(lambda sub: u_ref[sub * st:(sub + 1) * st, :], tm, st, cos_ref, sin_ref, gq_ref, gk_ref,
                          qfill_ref, *proj_out)

    x_mid = _out_proj_tile(i > 0, i < tpb - 1, *op_in, *pool_scratch)
    y = _ffn_tile(x_mid, op_in[-1], *ffn_in, chunk)
    o_ref[...] = y
    if with_proj:
        u_ref[...] = _project(y, modi_ref, gin_ref, win_ref)


def _tail(x, a, inv_count, o_win, o_glb, w_pool_bd, pool_scale, w_out, g_post, mod, g_pre_f, g_post_f, w_gate, w_up, w_down,
          proj, tm, chunk):
    b, n, d = x.shape
    pw, aw = a.shape[2], o_win.shape[2]
    hb = tm // POOL_HALO
    tpb = n // tm
    last = b * tpb - 1
    with_proj = proj is not None
    cur = lambda s: (jnp.minimum(s, last) // tpb, jnp.minimum(s, last) % tpb)
    prev = lambda s: (jnp.maximum(s - 1, 0) // tpb, jnp.maximum(s - 1, 0) % tpb)
    row = lambda s: (0, 0)
    tile = lambda s: (cur(s)[0], cur(s)[1], 0)
    in_specs = [
        pl.BlockSpec((None, tm, d), tile),
        pl.BlockSpec((None, tm, pw), tile),
        pl.BlockSpec((None, POOL_HALO, pw), lambda s: (cur(s)[0], jnp.maximum(cur(s)[1] * hb - 1, 0), 0)),
        pl.BlockSpec((None, POOL_HALO, pw), lambda s: (cur(s)[0], jnp.minimum((cur(s)[1] + 1) * hb, n // POOL_HALO - 1), 0)),
        pl.BlockSpec((tm, pw), lambda s: (cur(s)[1], 0)),
        pl.BlockSpec((None, tm, aw), tile),
        pl.BlockSpec((None, tm, aw), tile),
        _resident((pw, pw), row),
        pl.BlockSpec((1, pw), row),
        _layer_spec(w_out),
        pl.BlockSpec((1, d), row),
        pl.BlockSpec((None, N_MOD, d), lambda s: (cur(s)[0], 0, 0)),
        pl.BlockSpec((1, d), row),
        pl.BlockSpec((1, d), row),
        _layer_spec(w_gate),
        _layer_spec(w_up),
        _layer_spec(w_down),
    ]
    args = [x, a, a, a, inv_count, o_win, o_glb, w_pool_bd, pool_scale, w_out[0], g_post, mod,
            g_pre_f, g_post_f, w_gate[0], w_up[0], w_down[0]]
    out_shape = (jax.ShapeDtypeStruct((b, n, d), F32),)
    out_specs = (pl.BlockSpec((None, tm, d), tile),)
    scratch = [pltpu.VMEM((tm + 2 * POOL_HALO + POOL_HALO // 2, pw), F32),
               pltpu.VMEM((tm + 2 * POOL_HALO, pw), F32), pltpu.VMEM((tm + 2 * POOL_HALO, pw), F32),
               pltpu.VMEM((tm + 2 * POOL_HALO, LANES), F32)]
    if with_proj:
        mod_i, g_pre_i, w_in, cos, sin, gq, gk, qfill = proj
        nw = w_in[0].shape[2]
        in_specs += [
            pl.BlockSpec((None, N_MOD, d), lambda s: (cur(s)[0], 0, 0)),
            pl.BlockSpec((1, d), row),
            _layer_spec(w_in),
            pl.BlockSpec((tm, LANES), lambda s: (prev(s)[1], 0)),
            pl.BlockSpec((tm, LANES), lambda s: (prev(s)[1], 0)),
            pl.BlockSpec((1, LANES), row),
            pl.BlockSpec((1, LANES), row),
            pl.BlockSpec((1, LANES), row),
        ]
        args += [mod_i, g_pre_i, w_in[0], cos, sin, gq, gk, qfill]
        proj_shape, proj_specs = _in_proj_outputs(b, n, tm, prev)
        out_shape += proj_shape
        out_specs += proj_specs
        scratch = [pltpu.VMEM((tm, nw), F32)] + scratch
    outs = pl.pallas_call(
        functools.partial(_tail_kernel, tpb=tpb, chunk=chunk, st=min(tm, 512), with_proj=with_proj),
        out_shape=out_shape,
        grid=(b * tpb + (1 if with_proj else 0),),
        in_specs=in_specs,
        out_specs=out_specs,
        scratch_shapes=scratch,
        compiler_params=_cparams("arbitrary"),
        name="tail_proj" if with_proj else "tail",
    )(*args)
    return outs[0], outs[1:]


def _rope_tables(n):
    rows = n // GRID_W
    row = jnp.broadcast_to(jnp.arange(rows)[:, None], (rows, GRID_W)).reshape(-1).astype(F32)
    col = jnp.broadcast_to(jnp.arange(GRID_W)[None, :], (rows, GRID_W)).reshape(-1).astype(F32)
    freq = ROPE_THETA ** (-jnp.arange(ROPE_FREQS, dtype=F32) / ROPE_FREQS)
    ar, ac = row[:, None] * freq, col[:, None] * freq
    cos = jnp.concatenate([jnp.cos(ar), jnp.cos(ar), jnp.cos(ac), jnp.cos(ac)], axis=1)
    sin = jnp.concatenate([-jnp.sin(ar), jnp.sin(ar), -jnp.sin(ac), jnp.sin(ac)], axis=1)
    return jnp.tile(cos, (1, LANES // HEAD_DIM)), jnp.tile(sin, (1, LANES // HEAD_DIM))


def _shift_row(m):
    pieces, rest = [], m.astype(F32)
    for _ in range(N_SHIFT):
        part = rest.astype(BF16).astype(F32)
        pieces.append(-part)
        rest = rest - part
    return jnp.zeros((1, LANES), F32).at[0, HEAD_DIM:HEAD_DIM + N_SHIFT].set(jnp.stack(pieces))


def _pool_inv_counts(n):
    t = jnp.arange(n)
    cols = []
    for w in POOL_WINDOWS:
        hi = jnp.clip(t + (w - 1 - w // 2), 0, n - 1)
        lo = jnp.clip(t - w // 2, 0, n - 1)
        cols.append(jnp.broadcast_to((1.0 / (hi - lo + 1).astype(F32))[:, None], (n, HEAD_DIM)))
    return jnp.concatenate(cols, axis=1)


def _tile(n, pref):
    t = min(n, pref)
    assert n % t == 0, (n, t)
    return t


def kernel(x, c, ctx, c_ctx, w_mod, b_mod, g_pre_mix, g_post_mix, g_pre_ffn, g_post_ffn, w_in, w_pool, pool_scale, win_sink, g_qnorm, g_knorm, w_out, w_gate, w_up, w_down):
    b, n, d = x.shape
    lc = ctx.shape[1]
    depth = w_mod.shape[0]
    assert w_in.shape[2] == IN_WIDTH and KV_WIDTH == LANES and n % GRID_W == 0

    cos, sin = _rope_tables(n)
    cos_c, sin_c = jnp.ones((lc, LANES), F32), jnp.zeros((lc, LANES), F32)

    mod_rows = -(-(b + 1) // 8) * 8
    cc = jnp.concatenate([c, c_ctx[None, :], jnp.zeros((mod_rows - b - 1, d), F32)], axis=0)
    mod_all = _modulation(cc, w_mod, b_mod)

    tm = _tile(n, 512)
    tq_win = _tile(n, 1024)
    tq_glb = _tile(n, 512)
    tk_glb = n // (N_KV_HEADS * (tq_glb // WINDOW))
    assert tk_glb % LANES == 0
    rb_glb = 512
    ffn_chunk = 256

    def layer(l):
        bound = ATTN_SCALE * HEAD_DIM * jnp.max(jnp.abs(g_qnorm[l])) * jnp.max(jnp.abs(g_knorm[l]))
        return dict(
            m=mod_all[l, :b].reshape(b, N_MOD, d),
            mc=jnp.broadcast_to(mod_all[l, b].reshape(1, N_MOD, d), (b, N_MOD, d)),
            w_in=(w_in_b, l),
            gq=jnp.tile(g_qnorm[l], LANES // HEAD_DIM)[None, :],
            gk=jnp.tile(g_knorm[l], LANES // HEAD_DIM)[None, :],
            g_pre=g_pre_mix[l][None, :],
            bound=bound,
            qfill=jnp.where(bound <= SAFE_SHIFT, _shift_row(bound), 0.0),
        )

    w_in_b, w_out_b, w_gate_b, w_up_b, w_down_b = (w.astype(BF16) for w in (w_in, w_out, w_gate, w_up, w_down))
    no_fill = jnp.zeros((1, LANES), F32)
    icnt, icnt_c = _pool_inv_counts(n), _pool_inv_counts(lc)
    xc = ctx
    p = layer(0)
    proj = _in_proj(x, p["m"], p["g_pre"], p["w_in"], cos, sin, p["gq"], p["gk"], p["qfill"], tm, carried=True)
    for l in range(depth):
        last = l == depth - 1
        m, mc, gq, gk = p["m"], p["mc"], p["gq"], p["gk"]
        w_out_l, w_gate_l, w_up_l, w_down_l = (w_out_b, l), (w_gate_b, l), (w_up_b, l), (w_down_b, l)
        w_pool_bd = jax.scipy.linalg.block_diag(*[w_pool[l, g] for g in range(len(POOL_WINDOWS))]).astype(BF16)
        pscale = pool_scale[l][None, :]
        sink = jnp.zeros((8, LANES), F32).at[:N_Q_HEADS].set(jnp.broadcast_to(win_sink[l][:, None], (N_Q_HEADS, LANES)))
        g_post = g_post_mix[l][None, :]
        g_pre_f, g_post_f = g_pre_ffn[l][None, :], g_post_ffn[l][None, :]

        a, qw, ktw, vw, qg, ktg, vg = proj
        ac, qwc, ktwc, vwc, qgc, ktgc, vgc = _in_proj(xc, mc, p["g_pre"], p["w_in"], cos_c, sin_c, gq, gk, no_fill, lc)

        def attn_bounded(*ops):
            o_g, o_w = _attn(*ops, tq=tq_glb, tk=tk_glb, rb=rb_glb)
            return o_w, o_g

        def attn_exact(qg, ktg, vg, ktgc, vgc, qw, ktw, vw, ktwc, vwc, sink):
            return (_window_attn(qw, ktw, vw, ktwc, vwc, sink, tq_win),
                    _global_attn(qg, ktg, vg, ktgc, vgc, tq=tq_glb, tk=tk_glb, rb=rb_glb))

        o_win, o_glb = lax.cond(p["bound"] <= SAFE_SHIFT, attn_bounded, attn_exact,
                                qg, ktg, vg, ktgc, vgc, qw, ktw, vw, ktwc, vwc, sink)
        tail_args = (x, a, icnt, o_win, o_glb, w_pool_bd, pscale, w_out_l, g_post, m, g_pre_f, g_post_f,
                     w_gate_l, w_up_l, w_down_l)
        if last:
            x, _ = _tail(*tail_args, None, tm, ffn_chunk)
        else:
            oc_win = _ctx_attn(qwc, ktwc, vwc, sink)
            oc_glb = _ctx_attn(qgc, ktgc, vgc, None)
            xc = _out_proj(xc, ac, icnt_c, oc_win, oc_glb, w_pool_bd, pscale, w_out_l, g_post, mc, lc)
            xc = _ffn(xc, mc, g_pre_f, g_post_f, w_gate_l, w_up_l, w_down_l, lc, w_gate.shape[2])
            p = layer(l + 1)
            x, proj = _tail(*tail_args, (p["m"], p["g_pre"], p["w_in"], cos, sin, p["gq"], p["gk"], p["qfill"]),
                            tm, ffn_chunk)
    return x
```

```python
import functools

import jax
import jax.numpy as jnp
from jax import lax
from jax.experimental import pallas as pl
from jax.experimental.pallas import tpu as pltpu

F32 = jnp.float32
BF16 = jnp.bfloat16

HEAD_DIM = 64
GRID_W = 64
ROPE_FREQS = HEAD_DIM // 4
ROPE_THETA = 10000.0
NORM_EPS = 1e-6
NEG_INF = -1e30
ATTN_SCALE = HEAD_DIM ** -0.5
WINDOW = 128
POOL_WINDOWS = (2, 4, 8, 16)
POOL_HALO = 16
N_Q_HEADS = 6
N_KV_HEADS = 2
KV_GROUP = N_Q_HEADS // N_KV_HEADS
N_MOD = 6
POOL_WIDTH = len(POOL_WINDOWS) * HEAD_DIM
Q_WIDTH = N_Q_HEADS * HEAD_DIM
KV_WIDTH = N_KV_HEADS * HEAD_DIM
COL_QW = POOL_WIDTH
COL_KW = COL_QW + Q_WIDTH
COL_VW = COL_KW + KV_WIDTH
COL_QG = COL_VW + KV_WIDTH
COL_KG = COL_QG + Q_WIDTH
COL_VG = COL_KG + KV_WIDTH
IN_WIDTH = COL_VG + KV_WIDTH
N_SHIFT = 3
SAFE_SHIFT = 30.0

LANES = 128
VMEM_LIMIT = 56 * 1024 * 1024


def _cparams(*sem):
    return pltpu.CompilerParams(dimension_semantics=sem, vmem_limit_bytes=VMEM_LIMIT)


def _resident(block_shape, index_map):
    return pl.BlockSpec(block_shape, index_map, pipeline_mode=pl.Buffered(1))


def _layer_spec(wl):
    w, l = wl
    return pl.BlockSpec((None,) + w.shape[1:], lambda *g: (l,) + (0,) * (w.ndim - 1), pipeline_mode=pl.Buffered(1))


def _rms(x, g):
    ms = jnp.mean(x * x, axis=-1, keepdims=True)
    return (x * lax.rsqrt(ms + NORM_EPS)) * g


def _mod_kernel(c_ref, w_ref, b_ref, o_ref):
    c = c_ref[...]
    s = c * jax.nn.sigmoid(c)
    o_ref[...] = jnp.dot(s.astype(BF16), w_ref[...].astype(BF16), preferred_element_type=F32) + b_ref[...]


def _modulation(cc, w_mod, b_mod):
    depth, d, dm = w_mod.shape
    rows = cc.shape[0]
    tn = dm // 4
    return pl.pallas_call(
        _mod_kernel,
        out_shape=jax.ShapeDtypeStruct((depth, rows, dm), F32),
        grid=(depth, dm // tn),
        in_specs=[
            pl.BlockSpec((rows, d), lambda l, j: (0, 0)),
            pl.BlockSpec((None, d, tn), lambda l, j: (l, 0, j)),
            pl.BlockSpec((None, 1, tn), lambda l, j: (l, 0, j)),
        ],
        out_specs=pl.BlockSpec((None, rows, tn), lambda l, j: (l, 0, j)),
        compiler_params=_cparams("arbitrary", "arbitrary"),
        name="modulation",
    )(cc, w_mod, b_mod.reshape(depth, 1, dm))


def _in_proj_epilogue(u_of, n_rows, st, cos_ref, sin_ref, gq_ref, gk_ref, qfill_ref,
                      a_ref, qw_ref, ktw_ref, vw_ref, qg_ref, ktg_ref, vg_ref):
    lane = lax.broadcasted_iota(jnp.int32, (st, LANES), 1)
    lo = lane < HEAD_DIM
    first = (lane & ROPE_FREQS) == 0
    shift_rows = (lax.broadcasted_iota(jnp.int32, (HEAD_DIM, LANES), 0) < N_SHIFT).astype(BF16)
    sum_lane = (lane == HEAD_DIM).astype(F32)
    gq, gk = gq_ref[...], gk_ref[...]
    qfill = qfill_ref[...]

    def head_norm(t, g):
        sq = t * t
        s_lo = jnp.sum(jnp.where(lo, sq, 0.0), axis=-1, keepdims=True)
        s_hi = jnp.sum(jnp.where(lo, 0.0, sq), axis=-1, keepdims=True)
        ms = jnp.where(lo, s_lo, s_hi) * (1.0 / HEAD_DIM)
        return (t * lax.rsqrt(ms + NORM_EPS)) * g

    for sub in range(n_rows // st):
        rs = slice(sub * st, (sub + 1) * st)
        u = u_of(sub)
        cos = cos_ref[rs, :]
        sin = sin_ref[rs, :]

        def rope(t):
            partner = jnp.where(first, pltpu.roll(t, LANES - ROPE_FREQS, 1), pltpu.roll(t, ROPE_FREQS, 1))
            return t * cos + partner * sin

        def store_branch(q_chunks, k, v, fill, q_ref, kt_ref, v_ref):
            for hd in range(N_Q_HEADS):
                t = q_chunks[hd // 2]
                if hd % 2:
                    t = pltpu.roll(t, HEAD_DIM, 1)
                q_ref[hd, rs, :] = jnp.where(lo, t, fill).astype(BF16)
            for r in range(st // LANES):
                kb = sub * (st // LANES) + r
                kt = jnp.transpose(k[r * LANES:(r + 1) * LANES, :]).astype(BF16)
                for kv in range(N_KV_HEADS):
                    kt_ref[kv, kb, 0:HEAD_DIM, :] = kt[kv * HEAD_DIM:(kv + 1) * HEAD_DIM, :]
                    kt_ref[kv, kb, HEAD_DIM:LANES, :] = shift_rows
            v_ref[0, rs, :] = jnp.where(lo, v, sum_lane).astype(BF16)
            v_ref[1, rs, :] = jnp.where(lo, pltpu.roll(v, HEAD_DIM, 1), sum_lane).astype(BF16)

        def cols(start, width):
            return u[:, start:start + width]

        a_ref[rs, :] = cols(0, POOL_WIDTH)
        store_branch([rope(cols(COL_QW + LANES * c, LANES)) * ATTN_SCALE for c in range(Q_WIDTH // LANES)],
                     rope(cols(COL_KW, KV_WIDTH)), cols(COL_VW, KV_WIDTH), 0.0, qw_ref, ktw_ref, vw_ref)
        store_branch([rope(head_norm(cols(COL_QG + LANES * c, LANES), gq)) * ATTN_SCALE for c in range(Q_WIDTH // LANES)],
                     rope(head_norm(cols(COL_KG, KV_WIDTH), gk)), cols(COL_VG, KV_WIDTH), qfill, qg_ref, ktg_ref, vg_ref)


def _project(x, mod_ref, g_ref, w_ref):
    h = _rms(x, g_ref[...])
    h = h * (1.0 + mod_ref[1:2, :]) + mod_ref[0:1, :]
    return jnp.dot(h.astype(BF16), w_ref[...], preferred_element_type=F32)


def _in_proj_kernel(x_ref, mod_ref, g_ref, w_ref, cos_ref, sin_ref, gq_ref, gk_ref, qfill_ref, *out_refs, st):
    tm = x_ref.shape[0]
    us = [_project(x_ref[sub * st:(sub + 1) * st, :], mod_ref, g_ref, w_ref) for sub in range(tm // st)]
    _in_proj_epilogue(lambda sub: us[sub], tm, st, cos_ref, sin_ref, gq_ref, gk_ref, qfill_ref, *out_refs)


def _in_proj_outputs(b, n, tm, tile_of):
    qshape = jax.ShapeDtypeStruct((b, N_Q_HEADS, n, LANES), BF16)
    ktshape = jax.ShapeDtypeStruct((b, N_KV_HEADS, n // LANES, LANES, LANES), BF16)
    vshape = jax.ShapeDtypeStruct((b, N_KV_HEADS, n, LANES), BF16)

    def spec(block, where):
        return pl.BlockSpec(block, lambda *g: where(*tile_of(*g)))

    aspec = spec((None, tm, POOL_WIDTH), lambda bi, i: (bi, i, 0))
    qspec = spec((None, N_Q_HEADS, tm, LANES), lambda bi, i: (bi, 0, i, 0))
    ktspec = spec((None, N_KV_HEADS, tm // LANES, LANES, LANES), lambda bi, i: (bi, 0, i, 0, 0))
    vspec = spec((None, N_KV_HEADS, tm, LANES), lambda bi, i: (bi, 0, i, 0))
    shapes = (jax.ShapeDtypeStruct((b, n, POOL_WIDTH), F32), qshape, ktshape, vshape, qshape, ktshape, vshape)
    return shapes, (aspec, qspec, ktspec, vspec, qspec, ktspec, vspec)


def _in_proj_carried_kernel(x_ref, mod_ref, g_ref, w_ref, cos_ref, sin_ref, gq_ref, gk_ref, qfill_ref, *refs, st):
    out_refs, u_ref = refs[:-1], refs[-1]

    @pl.when(pl.program_id(0) == 0)
    def _():
        u_ref[...] = jnp.zeros(u_ref.shape, F32)

    _in_proj_epilogue(lambda sub: u_ref[sub * st:(sub + 1) * st, :], x_ref.shape[0], st, cos_ref, sin_ref,
                      gq_ref, gk_ref, qfill_ref, *out_refs)
    u_ref[...] = _project(x_ref[...], mod_ref, g_ref, w_ref)


def _in_proj(x, mod, g_pre, w_in, cos, sin, gq, gk, qfill, tm, carried=False):
    b, n, d = x.shape
    nw = w_in[0].shape[2]
    tpb = n // tm
    last = b * tpb - 1
    if carried:
        grid = (b * tpb + 1,)
        cur = lambda s: (jnp.minimum(s, last) // tpb, jnp.minimum(s, last) % tpb)
        prev = lambda s: (jnp.maximum(s - 1, 0) // tpb, jnp.maximum(s - 1, 0) % tpb)
        kern, scratch, sem = _in_proj_carried_kernel, [pltpu.VMEM((tm, nw), F32)], ("arbitrary",)
    else:
        grid = (b, tpb)
        cur = prev = lambda bi, i: (bi, i)
        kern, scratch, sem = _in_proj_kernel, [], ("arbitrary", "arbitrary")
    out_shape, out_specs = _in_proj_outputs(b, n, tm, prev)
    row = lambda *g: (0, 0)
    return pl.pallas_call(
        functools.partial(kern, st=min(tm, 512)),
        out_shape=out_shape,
        grid=grid,
        in_specs=[
            pl.BlockSpec((None, tm, d), lambda *g: (cur(*g)[0], cur(*g)[1], 0)),
            pl.BlockSpec((None, N_MOD, d), lambda *g: (cur(*g)[0], 0, 0)),
            pl.BlockSpec((1, d), row),
            _layer_spec(w_in),
            pl.BlockSpec((tm, LANES), lambda *g: (prev(*g)[1], 0)),
            pl.BlockSpec((tm, LANES), lambda *g: (prev(*g)[1], 0)),
            pl.BlockSpec((1, LANES), row),
            pl.BlockSpec((1, LANES), row),
            pl.BlockSpec((1, LANES), row),
        ],
        out_specs=out_specs,
        scratch_shapes=scratch,
        compiler_params=_cparams(*sem),
        name="in_proj",
    )(x, mod, g_pre, w_in[0], cos, sin, gq, gk, qfill)


def _merge_heads(outs, lo):
    return [jnp.where(lo, outs[2 * c], pltpu.roll(outs[2 * c + 1], HEAD_DIM, 1)) for c in range(N_Q_HEADS // 2)]


def _key_tile(kt_ref, first_block, n_blocks):
    return jnp.concatenate([kt_ref[first_block + j] for j in range(n_blocks)], axis=1)


def _stacked_sink(sink_ref, kv, rows_per_head):
    return jnp.concatenate([jnp.broadcast_to(sink_ref[hd:hd + 1, 0:1], (rows_per_head, 1))
                            for hd in range(kv * KV_GROUP, (kv + 1) * KV_GROUP)], axis=0)


def _window_units(q_ref, kt_ref, v_ref, ktc_ref, vc_ref, sink_ref, first_blk):
    nkb = kt_ref.shape[1]
    sub = WINDOW
    rows = KV_GROUP * sub
    r_in = lax.broadcasted_iota(jnp.int32, (rows, LANES), 0) & (sub - 1)
    c_in = lax.broadcasted_iota(jnp.int32, (rows, LANES), 1)
    ctx = [(_key_tile(ktc_ref.at[kv], 0, ktc_ref.shape[1]), vc_ref[kv], _stacked_sink(sink_ref, kv, sub))
           for kv in range(N_KV_HEADS)]

    def blocks(j):
        blk = first_blk + j
        return blk, jnp.maximum(blk - 1, 0), jnp.minimum(blk + 1, nkb - 1)

    def logits(j, kv):
        blk, prev_blk, next_blk = blocks(j)
        q = q_ref[kv * KV_GROUP:(kv + 1) * KV_GROUP, pl.ds(pl.multiple_of(j * sub, sub), sub), :].reshape(rows, LANES)
        kt = jnp.concatenate([kt_ref[kv, prev_blk], kt_ref[kv, blk], kt_ref[kv, next_blk]], axis=1)
        s = jnp.dot(q, kt, preferred_element_type=F32)
        sc = jnp.dot(q, ctx[kv][0], preferred_element_type=F32)
        return s, sc

    def attend(j, kv, s, sc):
        blk, prev_blk, next_blk = blocks(j)
        _, vc, sk = ctx[kv]
        v = jnp.concatenate([v_ref[kv, pl.ds(pl.multiple_of(kb * sub, sub), sub), :]
                             for kb in (prev_blk, blk, next_blk)], axis=0)
        s_prev = jnp.where(jnp.logical_and(c_in >= r_in, blk > 0), s[:, 0:sub], NEG_INF)
        s_own = s[:, sub:2 * sub]
        s_next = jnp.where(jnp.logical_and(c_in <= r_in, blk < nkb - 1), s[:, 2 * sub:3 * sub], NEG_INF)
        mx = jnp.maximum(jnp.maximum(s_prev, s_own), s_next)
        for t in range(sc.shape[1] // LANES):
            mx = jnp.maximum(mx, sc[:, t * LANES:(t + 1) * LANES])
        m = jnp.maximum(jnp.max(mx, axis=-1, keepdims=True), sk)
        p = jnp.concatenate([jnp.exp(s_prev - m), jnp.exp(s_own - m), jnp.exp(s_next - m)], axis=1)
        pv = jnp.dot(p.astype(BF16), v, preferred_element_type=F32)
        pv = pv + jnp.dot(jnp.exp(sc - m).astype(BF16), vc, preferred_element_type=F32)
        l = pv[:, HEAD_DIM:HEAD_DIM + 1] + jnp.exp(sk - m)
        o = pv / l
        return [o[g * sub:(g + 1) * sub] for g in range(KV_GROUP)]

    return logits, attend


def _store_heads(o_ref, row0, outs):
    nrows = outs[0].shape[0]
    lo = lax.broadcasted_iota(jnp.int32, (nrows, LANES), 1) < HEAD_DIM
    for c, chunk in enumerate(_merge_heads(outs, lo)):
        o_ref[pl.ds(row0, nrows), c * LANES:(c + 1) * LANES] = chunk.astype(BF16)


def _window_attn_kernel(q_ref, kt_ref, v_ref, ktc_ref, vc_ref, sink_ref, o_ref):
    tq = q_ref.shape[1]
    sub = WINDOW
    logits, attend = _window_units(q_ref, kt_ref, v_ref, ktc_ref, vc_ref, sink_ref, pl.program_id(1) * (tq // sub))
    units = [(j, kv) for j in range(tq // sub) for kv in range(N_KV_HEADS)]
    pending = logits(*units[0])
    outs = []
    for idx, (j, kv) in enumerate(units):
        ahead = logits(*units[idx + 1]) if idx + 1 < len(units) else None
        outs += attend(j, kv, *pending)
        pending = ahead
        if kv == N_KV_HEADS - 1:
            _store_heads(o_ref, j * sub, outs)
            outs = []


def _window_attn(q, kt, v, ktc, vc, sink, tq):
    b, _, n, _ = q.shape
    lc = vc.shape[2]
    assert tq % WINDOW == 0 and WINDOW == LANES
    batch4 = lambda bi, i: (bi, 0, 0, 0)
    batch5 = lambda bi, i: (bi, 0, 0, 0, 0)
    return pl.pallas_call(
        _window_attn_kernel,
        out_shape=jax.ShapeDtypeStruct((b, n, N_Q_HEADS * HEAD_DIM), BF16),
        grid=(b, n // tq),
        in_specs=[
            pl.BlockSpec((None, N_Q_HEADS, tq, LANES), lambda bi, i: (bi, 0, i, 0)),
            pl.BlockSpec((None, N_KV_HEADS, n // LANES, LANES, LANES), batch5),
            pl.BlockSpec((None, N_KV_HEADS, n, LANES), batch4),
            pl.BlockSpec((None, N_KV_HEADS, lc // LANES, LANES, LANES), batch5),
            pl.BlockSpec((None, N_KV_HEADS, lc, LANES), batch4),
            pl.BlockSpec(sink.shape, lambda bi, i: (0, 0)),
        ],
        out_specs=pl.BlockSpec((None, tq, N_Q_HEADS * HEAD_DIM), lambda bi, i: (bi, i, 0)),
        compiler_params=_cparams("arbitrary", "arbitrary"),
        name="window_attn",
    )(q, kt, v, ktc, vc, sink)


def _ctx_attn_kernel(*refs, has_sink):
    if has_sink:
        q_ref, kt_ref, v_ref, sink_ref, o_ref = refs
    else:
        q_ref, kt_ref, v_ref, o_ref = refs
    tq = q_ref.shape[1]
    rows = KV_GROUP * tq
    outs = []
    for kv in range(N_KV_HEADS):
        q = q_ref[kv * KV_GROUP:(kv + 1) * KV_GROUP].reshape(rows, LANES)
        s = jnp.dot(q, _key_tile(kt_ref.at[kv], 0, kt_ref.shape[1]), preferred_element_type=F32)
        m = jnp.max(s, axis=-1, keepdims=True)
        if has_sink:
            sk = _stacked_sink(sink_ref, kv, tq)
            m = jnp.maximum(m, sk)
        pv = jnp.dot(jnp.exp(s - m).astype(BF16), v_ref[kv], preferred_element_type=F32)
        l = pv[:, HEAD_DIM:HEAD_DIM + 1]
        if has_sink:
            l = l + jnp.exp(sk - m)
        o = pv / l
        outs += [o[g * tq:(g + 1) * tq] for g in range(KV_GROUP)]
    _store_heads(o_ref, 0, outs)


def _ctx_attn(q, kt, v, sink):
    b, _, n, _ = q.shape
    args = [q, kt, v]
    in_specs = [pl.BlockSpec((None,) + a.shape[1:], lambda bi, nd=a.ndim: (bi,) + (0,) * (nd - 1)) for a in args]
    if sink is not None:
        args.append(sink)
        in_specs.append(pl.BlockSpec(sink.shape, lambda bi: (0, 0)))
    return pl.pallas_call(
        functools.partial(_ctx_attn_kernel, has_sink=sink is not None),
        out_shape=jax.ShapeDtypeStruct((b, n, N_Q_HEADS * HEAD_DIM), BF16),
        grid=(b,),
        in_specs=in_specs,
        out_specs=pl.BlockSpec((None, n, N_Q_HEADS * HEAD_DIM), lambda bi: (bi, 0, 0)),
        compiler_params=_cparams("arbitrary"),
        name="ctx_attn",
    )(*args)


def _global_attn_kernel(q_ref, kt_ref, v_ref, ktc_ref, vc_ref, o_ref, qs_ref, acc_ref, mx_ref, *, tk, rb):
    tq = q_ref.shape[1]
    rows = KV_GROUP * tq
    kb_per_step = tk // LANES
    n_steps = kt_ref.shape[1] // kb_per_step
    lane = lax.broadcasted_iota(jnp.int32, (rows, LANES), 1)

    def key_steps(fn):
        def step(kc, carry):
            for kv in range(N_KV_HEADS):
                fn(kv, _key_tile(kt_ref.at[kv], kc * kb_per_step, kb_per_step),
                   v_ref[kv, pl.ds(pl.multiple_of(kc * tk, tk), tk), :])
            return carry
        lax.fori_loop(0, n_steps, step, 0)
        for kv in range(N_KV_HEADS):
            fn(kv, _key_tile(ktc_ref.at[kv], 0, ktc_ref.shape[1]), vc_ref[kv])

    def q_rows(kv):
        return q_ref[kv * KV_GROUP:(kv + 1) * KV_GROUP].reshape(rows, LANES)

    mx_ref[...] = jnp.full(mx_ref.shape, NEG_INF, F32)

    def track_max(kv, kt, v):
        s = jnp.dot(q_rows(kv), kt, preferred_element_type=F32)
        mx = mx_ref[kv]
        for j in range(kt.shape[1] // LANES):
            mx = jnp.maximum(mx, s[:, j * LANES:(j + 1) * LANES])
        mx_ref[kv] = mx

    key_steps(track_max)

    for kv in range(N_KV_HEADS):
        qs = q_rows(kv).astype(F32)
        rest = jnp.max(mx_ref[kv], axis=-1, keepdims=True)
        for piece in range(N_SHIFT):
            part = rest.astype(BF16).astype(F32)
            qs = jnp.where(lane == HEAD_DIM + piece, -part, qs)
            rest = rest - part
        qs_ref[kv] = qs.astype(BF16)

    acc_ref[...] = jnp.zeros(acc_ref.shape, F32)

    def accumulate(kv, kt, v):
        for r in range(rows // rb):
            rsl = slice(r * rb, (r + 1) * rb)
            s = jnp.dot(qs_ref[kv, rsl, :], kt, preferred_element_type=F32)
            p = jnp.exp(s).astype(BF16)
            acc_ref[kv, rsl, :] += jnp.dot(p, v, preferred_element_type=F32)

    key_steps(accumulate)

    outs = []
    for hd in range(N_Q_HEADS):
        a = acc_ref[hd // KV_GROUP, (hd % KV_GROUP) * tq:(hd % KV_GROUP + 1) * tq, :]
        outs.append(a / a[:, HEAD_DIM:HEAD_DIM + 1])
    _store_heads(o_ref, 0, outs)


def _global_attn(q, kt, v, ktc, vc, *, tq, tk, rb):
    b, _, n, _ = q.shape
    lc = vc.shape[2]
    rows = KV_GROUP * tq
    batch4 = lambda bi, i: (bi, 0, 0, 0)
    batch5 = lambda bi, i: (bi, 0, 0, 0, 0)
    return pl.pallas_call(
        functools.partial(_global_attn_kernel, tk=tk, rb=rb),
        out_shape=jax.ShapeDtypeStruct((b, n, N_Q_HEADS * HEAD_DIM), BF16),
        grid=(b, n // tq),
        in_specs=[
            pl.BlockSpec((None, N_Q_HEADS, tq, LANES), lambda bi, i: (bi, 0, i, 0)),
            pl.BlockSpec((None, N_KV_HEADS, n // LANES, LANES, LANES), batch5),
            pl.BlockSpec((None, N_KV_HEADS, n, LANES), batch4),
            pl.BlockSpec((None, N_KV_HEADS, lc // LANES, LANES, LANES), batch5),
            pl.BlockSpec((None, N_KV_HEADS, lc, LANES), batch4),
        ],
        out_specs=pl.BlockSpec((None, tq, N_Q_HEADS * HEAD_DIM), lambda bi, i: (bi, i, 0)),
        scratch_shapes=[pltpu.VMEM((N_KV_HEADS, rows, LANES), BF16), pltpu.VMEM((N_KV_HEADS, rows, LANES), F32),
                        pltpu.VMEM((N_KV_HEADS, rows, LANES), F32)],
        compiler_params=_cparams("arbitrary", "arbitrary"),
        name="global_attn_exact_max",
    )(q, kt, v, ktc, vc)


def _attn_kernel(qg_ref, ktg_ref, vg_ref, ktgc_ref, vgc_ref, qw_ref, ktw_ref, vw_ref, ktwc_ref, vwc_ref, sink_ref,
                 og_ref, ow_ref, acc_ref, *, tk, rb):
    tq = qg_ref.shape[1]
    rows = KV_GROUP * tq
    sub = WINDOW
    n_sub = tq // sub
    kb_per_step = tk // LANES
    steps_per_sub = ktg_ref.shape[1] // kb_per_step // n_sub
    assert steps_per_sub * n_sub * kb_per_step == ktg_ref.shape[1] and tq % rb == 0

    def global_keys(kv, kt, v, assign):
        for r in range(rows // rb):
            hd, off = kv * KV_GROUP + (r * rb) // tq, (r * rb) % tq
            s = jnp.dot(qg_ref[hd, off:off + rb, :], kt, preferred_element_type=F32)
            pv = jnp.dot(jnp.exp(s).astype(BF16), v, preferred_element_type=F32)
            if assign:
                acc_ref[kv, r * rb:(r + 1) * rb, :] = pv
            else:
                acc_ref[kv, r * rb:(r + 1) * rb, :] += pv

    def global_step(kc):
        for kv in range(N_KV_HEADS):
            global_keys(kv, _key_tile(ktg_ref.at[kv], kc * kb_per_step, kb_per_step),
                        vg_ref[kv, pl.ds(pl.multiple_of(kc * tk, tk), tk), :], False)

    for kv in range(N_KV_HEADS):
        global_keys(kv, _key_tile(ktgc_ref.at[kv], 0, ktgc_ref.shape[1]), vgc_ref[kv], True)

    logits, attend = _window_units(qw_ref, ktw_ref, vw_ref, ktwc_ref, vwc_ref, sink_ref, pl.program_id(1) * n_sub)

    def body(j, carry):
        pending = logits(j, 0)
        outs = []
        for kv in range(N_KV_HEADS):
            for t in range(steps_per_sub // N_KV_HEADS):
                global_step(j * steps_per_sub + kv * (steps_per_sub // N_KV_HEADS) + t)
            ahead = logits(j, kv + 1) if kv + 1 < N_KV_HEADS else None
            outs += attend(j, kv, *pending)
            pending = ahead
        _store_heads(ow_ref, pl.multiple_of(j * sub, sub), outs)
        return carry

    assert steps_per_sub % N_KV_HEADS == 0
    lax.fori_loop(0, n_sub, body, 0, unroll=True)

    outs = []
    for hd in range(N_Q_HEADS):
        a = acc_ref[hd // KV_GROUP, (hd % KV_GROUP) * tq:(hd % KV_GROUP + 1) * tq, :]
        outs.append(a / a[:, HEAD_DIM:HEAD_DIM + 1])
    _store_heads(og_ref, 0, outs)


def _attn(qg, ktg, vg, ktgc, vgc, qw, ktw, vw, ktwc, vwc, sink, *, tq, tk, rb):
    b, _, n, _ = qg.shape
    lc = vgc.shape[2]
    batch4 = lambda bi, i: (bi, 0, 0, 0)
    batch5 = lambda bi, i: (bi, 0, 0, 0, 0)
    branch_specs = [
        pl.BlockSpec((None, N_Q_HEADS, tq, LANES), lambda bi, i: (bi, 0, i, 0)),
        pl.BlockSpec((None, N_KV_HEADS, n // LANES, LANES, LANES), batch5),
        pl.BlockSpec((None, N_KV_HEADS, n, LANES), batch4),
        pl.BlockSpec((None, N_KV_HEADS, lc // LANES, LANES, LANES), batch5),
        pl.BlockSpec((None, N_KV_HEADS, lc, LANES), batch4),
    ]
    oshape = jax.ShapeDtypeStruct((b, n, N_Q_HEADS * HEAD_DIM), BF16)
    ospec = pl.BlockSpec((None, tq, N_Q_HEADS * HEAD_DIM), lambda bi, i: (bi, i, 0))
    return pl.pallas_call(
        functools.partial(_attn_kernel, tk=tk, rb=rb),
        out_shape=(oshape, oshape),
        grid=(b, n // tq),
        in_specs=branch_specs + branch_specs + [pl.BlockSpec(sink.shape, lambda bi, i: (0, 0))],
        out_specs=(ospec, ospec),
        scratch_shapes=[pltpu.VMEM((N_KV_HEADS, KV_GROUP * tq, LANES), F32)],
        compiler_params=_cparams("arbitrary", "arbitrary"),
        name="attn",
    )(qg, ktg, vg, ktgc, vgc, qw, ktw, vw, ktwc, vwc, sink)


def _out_proj_tile(has_prev, has_next, x_ref, a_ref, ap_ref, an_ref, icnt_ref, ow_ref, og_ref, wpool_ref, pscale_ref,
                   wo_ref, g_ref, mod_ref, ext_ref, s2_ref, s4_ref, s8_ref):
    tm = x_ref.shape[0]
    ext_ref[0:POOL_HALO, :] = jnp.where(has_prev, ap_ref[...], 0.0)
    ext_ref[POOL_HALO:POOL_HALO + tm, :] = a_ref[...]
    ext_ref[POOL_HALO + tm:2 * POOL_HALO + tm, :] = jnp.where(has_next, an_ref[...], 0.0)

    pw = a_ref.shape[1]
    aw = ow_ref.shape[1]
    y = jnp.dot(ow_ref[...], wo_ref[pw:pw + aw, :], preferred_element_type=F32)
    y = y + jnp.dot(og_ref[...], wo_ref[pw + aw:pw + 2 * aw, :], preferred_element_type=F32)

    lo = lax.broadcasted_iota(jnp.int32, (tm, LANES), 1) < HEAD_DIM

    pad = POOL_HALO // 2
    ext_ref[2 * POOL_HALO + tm:2 * POOL_HALO + tm + pad, :] = jnp.zeros((pad, ext_ref.shape[1]), F32)
    r2, r4, r8 = tm + 2 * POOL_HALO, tm + 2 * POOL_HALO - pad, tm + POOL_HALO
    s2_ref[0:r2, :] = ext_ref[0:r2, :] + ext_ref[1:r2 + 1, :]
    s4_ref[0:r4, :] = s2_ref[0:r4, :] + s2_ref[2:r4 + 2, :]
    s8_ref[0:r8, :] = s4_ref[0:r8, LANES:2 * LANES] + s4_ref[4:r8 + 4, LANES:2 * LANES]

    def window_sum(w, c):
        first = POOL_HALO - w // 2
        cols = slice(c * LANES, (c + 1) * LANES)
        if w == 2:
            return s2_ref[first:first + tm, cols]
        if w == 4:
            return s4_ref[first:first + tm, cols]
        if w == 8:
            return s8_ref[first:first + tm, :]
        return s8_ref[first:first + tm, :] + s8_ref[first + 8:first + 8 + tm, :]

    feats = []
    for c in range(2):
        w_a, w_b = POOL_WINDOWS[2 * c], POOL_WINDOWS[2 * c + 1]
        x0 = ext_ref[POOL_HALO:POOL_HALO + tm, c * LANES:(c + 1) * LANES]
        mean = jnp.where(lo, window_sum(w_a, c), window_sum(w_b, c)) * icnt_ref[:, c * LANES:(c + 1) * LANES]
        feats.append(mean - x0)
    feats = jnp.concatenate(feats, axis=1).astype(BF16)
    y_pool = jnp.dot(feats, wpool_ref[...], preferred_element_type=F32) * pscale_ref[...]

    y = y + jnp.dot(y_pool.astype(BF16), wo_ref[0:pw, :], preferred_element_type=F32)
    return x_ref[...] + mod_ref[2:3, :] * _rms(y, g_ref[...])


N_OUT_PROJ_IN = 12


def _out_proj_kernel(*refs):
    ins, o_ref, scratch = refs[:N_OUT_PROJ_IN], refs[N_OUT_PROJ_IN], refs[N_OUT_PROJ_IN + 1:]
    i = pl.program_id(1)
    o_ref[...] = _out_proj_tile(i > 0, i < pl.num_programs(1) - 1, *ins, *scratch)


def _out_proj(x, a, inv_count, o_win, o_glb, w_pool_bd, pool_scale, w_out, g_post, mod, tm):
    b, n, d = x.shape
    pw = a.shape[2]
    aw = o_win.shape[2]
    hb = tm // POOL_HALO
    row = lambda bi, i: (0, 0)
    tile = lambda bi, i: (bi, i, 0)
    return pl.pallas_call(
        _out_proj_kernel,
        out_shape=jax.ShapeDtypeStruct((b, n, d), F32),
        grid=(b, n // tm),
        in_specs=[
            pl.BlockSpec((None, tm, d), tile),
            pl.BlockSpec((None, tm, pw), tile),
            pl.BlockSpec((None, POOL_HALO, pw), lambda bi, i: (bi, jnp.maximum(i * hb - 1, 0), 0)),
            pl.BlockSpec((None, POOL_HALO, pw), lambda bi, i: (bi, jnp.minimum((i + 1) * hb, n // POOL_HALO - 1), 0)),
            pl.BlockSpec((tm, pw), lambda bi, i: (i, 0)),
            pl.BlockSpec((None, tm, aw), tile),
            pl.BlockSpec((None, tm, aw), tile),
            _resident((pw, pw), row),
            pl.BlockSpec((1, pw), row),
            _layer_spec(w_out),
            pl.BlockSpec((1, d), row),
            pl.BlockSpec((None, N_MOD, d), lambda bi, i: (bi, 0, 0)),
        ],
        out_specs=pl.BlockSpec((None, tm, d), tile),
        scratch_shapes=[pltpu.VMEM((tm + 2 * POOL_HALO + POOL_HALO // 2, pw), F32),
                        pltpu.VMEM((tm + 2 * POOL_HALO, pw), F32), pltpu.VMEM((tm + 2 * POOL_HALO, pw), F32),
                        pltpu.VMEM((tm + 2 * POOL_HALO, LANES), F32)],
        compiler_params=_cparams("arbitrary", "arbitrary"),
        name="out_proj",
    )(x, a, a, a, inv_count, o_win, o_glb, w_pool_bd, pool_scale, w_out[0], g_post, mod)


def _ffn_tile(x, mod_ref, gpre_ref, gpost_ref, wg_ref, wu_ref, wd_ref, chunk):
    h = _rms(x, gpre_ref[...])
    h = (h * (1.0 + mod_ref[4:5, :]) + mod_ref[3:4, :]).astype(BF16)
    f = None
    for c in range(wg_ref.shape[1] // chunk):
        cols = slice(c * chunk, (c + 1) * chunk)
        g = jnp.dot(h, wg_ref[:, cols], preferred_element_type=F32)
        u = jnp.dot(h, wu_ref[:, cols], preferred_element_type=F32)
        act = ((g * jax.nn.sigmoid(g)) * u).astype(BF16)
        part = jnp.dot(act, wd_ref[cols, :], preferred_element_type=F32)
        f = part if f is None else f + part
    return x + mod_ref[5:6, :] * _rms(f, gpost_ref[...])


def _ffn_kernel(x_ref, mod_ref, gpre_ref, gpost_ref, wg_ref, wu_ref, wd_ref, o_ref, *, chunk):
    o_ref[...] = _ffn_tile(x_ref[...], mod_ref, gpre_ref, gpost_ref, wg_ref, wu_ref, wd_ref, chunk)


def _ffn(x, mod, g_pre, g_post, w_gate, w_up, w_down, tm, chunk):
    b, n, d = x.shape
    row = lambda bi, i: (0, 0)
    tile = lambda bi, i: (bi, i, 0)
    return pl.pallas_call(
        functools.partial(_ffn_kernel, chunk=chunk),
        out_shape=jax.ShapeDtypeStruct((b, n, d), F32),
        grid=(b, n // tm),
        in_specs=[
            pl.BlockSpec((None, tm, d), tile),
            pl.BlockSpec((None, N_MOD, d), lambda bi, i: (bi, 0, 0)),
            pl.BlockSpec((1, d), row),
            pl.BlockSpec((1, d), row),
            _layer_spec(w_gate),
            _layer_spec(w_up),
            _layer_spec(w_down),
        ],
        out_specs=pl.BlockSpec((None, tm, d), tile),
        compiler_params=_cparams("arbitrary", "arbitrary"),
        name="ffn",
    )(x, mod, g_pre, g_post, w_gate[0], w_up[0], w_down[0])


N_FFN_IN = 5
N_PROJ_IN = 8
N_PROJ_OUT = 7


def _tail_kernel(*refs, tpb, chunk, st, with_proj):
    refs = list(refs)
    op_in, refs = refs[:N_OUT_PROJ_IN], refs[N_OUT_PROJ_IN:]
    ffn_in, refs = refs[:N_FFN_IN], refs[N_FFN_IN:]
    if with_proj:
        (modi_ref, gin_ref, win_ref, cos_ref, sin_ref, gq_ref, gk_ref, qfill_ref), refs = refs[:N_PROJ_IN], refs[N_PROJ_IN:]
    o_ref, refs = refs[0], refs[1:]
    if with_proj:
        proj_out, refs = refs[:N_PROJ_OUT], refs[N_PROJ_OUT:]
        u_ref, refs = refs[0], refs[1:]
    pool_scratch = refs
    tm = o_ref.shape[0]
    step = pl.program_id(0)
    i = lax.rem(jnp.minimum(step, pl.num_programs(0) - (2 if with_proj else 1)), tpb)

    if with_proj:
        @pl.when(step == 0)
        def _():
            u_ref[...] = jnp.zeros(u_ref.shape, F32)

        _in_proj_epilogue(lambda sub: u_ref[sub * st:(sub + 1) * st, :], tm, st, cos_ref, sin_ref, gq_ref, gk_ref,
                          qfill_ref, *proj_out)

    x_mid = _out_proj_tile(i > 0, i < tpb - 1, *op_in, *pool_scratch)
    y = _ffn_tile(x_mid, op_in[-1], *ffn_in, chunk)
    o_ref[...] = y
    if with_proj:
        u_ref[...] = _project(y, modi_ref, gin_ref, win_ref)


def _tail(x, a, inv_count, o_win, o_glb, w_pool_bd, pool_scale, w_out, g_post, mod, g_pre_f, g_post_f, w_gate, w_up, w_down,
          proj, tm, chunk):
    b, n, d = x.shape
    pw, aw = a.shape[2], o_win.shape[2]
    hb = tm // POOL_HALO
    tpb = n // tm
    last = b * tpb - 1
    with_proj = proj is not None
    cur = lambda s: (jnp.minimum(s, last) // tpb, jnp.minimum(s, last) % tpb)
    prev = lambda s: (jnp.maximum(s - 1, 0) // tpb, jnp.maximum(s - 1, 0) % tpb)
    row = lambda s: (0, 0)
    tile = lambda s: (cur(s)[0], cur(s)[1], 0)
    in_specs = [
        pl.BlockSpec((None, tm, d), tile),
        pl.BlockSpec((None, tm, pw), tile),
        pl.BlockSpec((None, POOL_HALO, pw), lambda s: (cur(s)[0], jnp.maximum(cur(s)[1] * hb - 1, 0), 0)),
        pl.BlockSpec((None, POOL_HALO, pw), lambda s: (cur(s)[0], jnp.minimum((cur(s)[1] + 1) * hb, n // POOL_HALO - 1), 0)),
        pl.BlockSpec((tm, pw), lambda s: (cur(s)[1], 0)),
        pl.BlockSpec((None, tm, aw), tile),
        pl.BlockSpec((None, tm, aw), tile),
        _resident((pw, pw), row),
        pl.BlockSpec((1, pw), row),
        _layer_spec(w_out),
        pl.BlockSpec((1, d), row),
        pl.BlockSpec((None, N_MOD, d), lambda s: (cur(s)[0], 0, 0)),
        pl.BlockSpec((1, d), row),
        pl.BlockSpec((1, d), row),
        _layer_spec(w_gate),
        _layer_spec(w_up),
        _layer_spec(w_down),
    ]
    args = [x, a, a, a, inv_count, o_win, o_glb, w_pool_bd, pool_scale, w_out[0], g_post, mod,
            g_pre_f, g_post_f, w_gate[0], w_up[0], w_down[0]]
    out_shape = (jax.ShapeDtypeStruct((b, n, d), F32),)
    out_specs = (pl.BlockSpec((None, tm, d), tile),)
    scratch = [pltpu.VMEM((tm + 2 * POOL_HALO + POOL_HALO // 2, pw), F32),
               pltpu.VMEM((tm + 2 * POOL_HALO, pw), F32), pltpu.VMEM((tm + 2 * POOL_HALO, pw), F32),
               pltpu.VMEM((tm + 2 * POOL_HALO, LANES), F32)]
    if with_proj:
        mod_i, g_pre_i, w_in, cos, sin, gq, gk, qfill = proj
        nw = w_in[0].shape[2]
        in_specs += [
            pl.BlockSpec((None, N_MOD, d), lambda s: (cur(s)[0], 0, 0)),
            pl.BlockSpec((1, d), row),
            _layer_spec(w_in),
            pl.BlockSpec((tm, LANES), lambda s: (prev(s)[1], 0)),
            pl.BlockSpec((tm, LANES), lambda s: (prev(s)[1], 0)),
            pl.BlockSpec((1, LANES), row),
            pl.BlockSpec((1, LANES), row),
            pl.BlockSpec((1, LANES), row),
        ]
        args += [mod_i, g_pre_i, w_in[0], cos, sin, gq, gk, qfill]
        proj_shape, proj_specs = _in_proj_outputs(b, n, tm, prev)
        out_shape += proj_shape
        out_specs += proj_specs
        scratch = [pltpu.VMEM((tm, nw), F32)] + scratch
    outs = pl.pallas_call(
        functools.partial(_tail_kernel, tpb=tpb, chunk=chunk, st=min(tm, 512), with_proj=with_proj),
        out_shape=out_shape,
        grid=(b * tpb + (1 if with_proj else 0),),
        in_specs=in_specs,
        out_specs=out_specs,
        scratch_shapes=scratch,
        compiler_params=_cparams("arbitrary"),
        name="tail_proj" if with_proj else "tail",
    )(*args)
    return outs[0], outs[1:]


def _rope_tables(n):
    rows = n // GRID_W
    row = jnp.broadcast_to(jnp.arange(rows)[:, None], (rows, GRID_W)).reshape(-1).astype(F32)
    col = jnp.broadcast_to(jnp.arange(GRID_W)[None, :], (rows, GRID_W)).reshape(-1).astype(F32)
    freq = ROPE_THETA ** (-jnp.arange(ROPE_FREQS, dtype=F32) / ROPE_FREQS)
    ar, ac = row[:, None] * freq, col[:, None] * freq
    cos = jnp.concatenate([jnp.cos(ar), jnp.cos(ar), jnp.cos(ac), jnp.cos(ac)], axis=1)
    sin = jnp.concatenate([-jnp.sin(ar), jnp.sin(ar), -jnp.sin(ac), jnp.sin(ac)], axis=1)
    return jnp.tile(cos, (1, LANES // HEAD_DIM)), jnp.tile(sin, (1, LANES // HEAD_DIM))


def _shift_row(m):
    pieces, rest = [], m.astype(F32)
    for _ in range(N_SHIFT):
        part = rest.astype(BF16).astype(F32)
        pieces.append(-part)
        rest = rest - part
    return jnp.zeros((1, LANES), F32).at[0, HEAD_DIM:HEAD_DIM + N_SHIFT].set(jnp.stack(pieces))


def _pool_inv_counts(n):
    t = jnp.arange(n)
    cols = []
    for w in POOL_WINDOWS:
        hi = jnp.clip(t + (w - 1 - w // 2), 0, n - 1)
        lo = jnp.clip(t - w // 2, 0, n - 1)
        cols.append(jnp.broadcast_to((1.0 / (hi - lo + 1).astype(F32))[:, None], (n, HEAD_DIM)))
    return jnp.concatenate(cols, axis=1)


def _tile(n, pref):
    t = min(n, pref)
    assert n % t == 0, (n, t)
    return t


def kernel(x, c, ctx, c_ctx, w_mod, b_mod, g_pre_mix, g_post_mix, g_pre_ffn, g_post_ffn, w_in, w_pool, pool_scale, win_sink, g_qnorm, g_knorm, w_out, w_gate, w_up, w_down):
    b, n, d = x.shape
    lc = ctx.shape[1]
    depth = w_mod.shape[0]
    assert w_in.shape[2] == IN_WIDTH and KV_WIDTH == LANES and n % GRID_W == 0

    cos, sin = _rope_tables(n)
    cos_c, sin_c = jnp.ones((lc, LANES), F32), jnp.zeros((lc, LANES), F32)

    mod_rows = -(-(b + 1) // 8) * 8
    cc = jnp.concatenate([c, c_ctx[None, :], jnp.zeros((mod_rows - b - 1, d), F32)], axis=0)
    mod_all = _modulation(cc, w_mod, b_mod)

    tm = _tile(n, 512)
    tq_win = _tile(n, 1024)
    tq_glb = _tile(n, 512)
    tk_glb = n // (N_KV_HEADS * (tq_glb // WINDOW))
    assert tk_glb % LANES == 0
    rb_glb = 512
    ffn_chunk = 256

    def layer(l):
        bound = ATTN_SCALE * HEAD_DIM * jnp.max(jnp.abs(g_qnorm[l])) * jnp.max(jnp.abs(g_knorm[l]))
        return dict(
            m=mod_all[l, :b].reshape(b, N_MOD, d),
            mc=jnp.broadcast_to(mod_all[l, b].reshape(1, N_MOD, d), (b, N_MOD, d)),
            w_in=(w_in_b, l),
            gq=jnp.tile(g_qnorm[l], LANES // HEAD_DIM)[None, :],
            gk=jnp.tile(g_knorm[l], LANES // HEAD_DIM)[None, :],
            g_pre=g_pre_mix[l][None, :],
            bound=bound,
            qfill=jnp.where(bound <= SAFE_SHIFT, _shift_row(bound), 0.0),
        )

    w_in_b, w_out_b, w_gate_b, w_up_b, w_down_b = (w.astype(BF16) for w in (w_in, w_out, w_gate, w_up, w_down))
    no_fill = jnp.zeros((1, LANES), F32)
    icnt, icnt_c = _pool_inv_counts(n), _pool_inv_counts(lc)
    xc = ctx
    p = layer(0)
    proj = _in_proj(x, p["m"], p["g_pre"], p["w_in"], cos, sin, p["gq"], p["gk"], p["qfill"], tm, carried=True)
    proj_c = _in_proj(xc, p["mc"], p["g_pre"], p["w_in"], cos_c, sin_c, p["gq"], p["gk"], no_fill, lc)
    for l in range(depth):
        last = l == depth - 1
        m, mc, gq, gk = p["m"], p["mc"], p["gq"], p["gk"]
        w_out_l, w_gate_l, w_up_l, w_down_l = (w_out_b, l), (w_gate_b, l), (w_up_b, l), (w_down_b, l)
        w_pool_bd = jax.scipy.linalg.block_diag(*[w_pool[l, g] for g in range(len(POOL_WINDOWS))]).astype(BF16)
        pscale = pool_scale[l][None, :]
        sink = jnp.zeros((8, LANES), F32).at[:N_Q_HEADS].set(jnp.broadcast_to(win_sink[l][:, None], (N_Q_HEADS, LANES)))
        g_post = g_post_mix[l][None, :]
        g_pre_f, g_post_f = g_pre_ffn[l][None, :], g_post_ffn[l][None, :]

        a, qw, ktw, vw, qg, ktg, vg = proj
        ac, qwc, ktwc, vwc, qgc, ktgc, vgc = proj_c

        def attn_bounded(*ops):
            o_g, o_w = _attn(*ops, tq=tq_glb, tk=tk_glb, rb=rb_glb)
            return o_w, o_g

        def attn_exact(qg, ktg, vg, ktgc, vgc, qw, ktw, vw, ktwc, vwc, sink):
            return (_window_attn(qw, ktw, vw, ktwc, vwc, sink, tq_win),
                    _global_attn(qg, ktg, vg, ktgc, vgc, tq=tq_glb, tk=tk_glb, rb=rb_glb))

        o_win, o_glb = lax.cond(p["bound"] <= SAFE_SHIFT, attn_bounded, attn_exact,
                                qg, ktg, vg, ktgc, vgc, qw, ktw, vw, ktwc, vwc, sink)
        tail_args = (x, a, icnt, o_win, o_glb, w_pool_bd, pscale, w_out_l, g_post, m, g_pre_f, g_post_f,
                     w_gate_l, w_up_l, w_down_l)
        if last:
            x, _ = _tail(*tail_args, None, tm, ffn_chunk)
        else:
            oc_win = _ctx_attn(qwc, ktwc, vwc, sink)
            oc_glb = _ctx_attn(qgc, ktgc, vgc, None)
            p = layer(l + 1)
            xc, proj_c = _tail(xc, ac, icnt_c, oc_win, oc_glb, w_pool_bd, pscale, w_out_l, g_post, mc, g_pre_f, g_post_f,
                               w_gate_l, w_up_l, w_down_l,
                               (p["mc"], p["g_pre"], p["w_in"], cos_c, sin_c, p["gq"], p["gk"], no_fill),
                               lc, w_gate.shape[2])
            x, proj = _tail(*tail_args, (p["m"], p["g_pre"], p["w_in"], cos, sin, p["gq"], p["gk"], p["qfill"]),
                            tm, ffn_chunk)
    return x
```
